```python
import jax, jax.numpy as jnp
from jax import lax
import numpy as np

D_MODEL = 1024
BATCH = 8
SEQ = 4096
DEPTH = 2

CHUNK = 64
N_MIXERS = 2
EPS = 1e-6

RET_HEADS = 4
RET_DK = D_MODEL // RET_HEADS
RET_DV = 2 * RET_DK
RET_QK_DIM = RET_HEADS * RET_DK
RET_V_DIM = RET_HEADS * RET_DV
RET_IN_DIM = 2 * RET_QK_DIM + 2 * RET_V_DIM
ROPE_BASE = 10000.0

POOL_WINDOWS = (2, 4, 8, 16)
POOL_GROUPS = len(POOL_WINDOWS)
POOL_GROUP_DIM = D_MODEL // POOL_GROUPS

MOE_GROUPS = 4
MOE_EXPERTS_PER_GROUP = 8
MOE_EXPERTS = MOE_GROUPS * MOE_EXPERTS_PER_GROUP
MOE_TOP_K = 2
MOE_D_FF = 256
MOE_BLOCK_ROWS = 128

kernel_name = "hybrid_retention_pool_hier_moe"


def rmsnorm(x, gain):
    xf = x.astype(jnp.float32)
    inv = lax.rsqrt(jnp.mean(xf * xf, axis=-1, keepdims=True) + EPS)
    return (xf * inv * gain.astype(jnp.float32)).astype(x.dtype)


def rope(x, pos):
    half = x.shape[-1] // 2
    inv_freq = 1.0 / (ROPE_BASE ** (jnp.arange(half, dtype=jnp.float32) / half))
    ang = pos[:, None] * inv_freq[None, :]
    cos = jnp.cos(ang)[None, :, None, :]
    sin = jnp.sin(ang)[None, :, None, :]
    x1, x2 = x[..., :half], x[..., half:]
    return jnp.concatenate([x1 * cos - x2 * sin, x1 * sin + x2 * cos], axis=-1)


def chunk_retention(q, k, v):
    B, H, S, dk = q.shape
    dv = v.shape[-1]
    n_chunks = S // CHUNK
    log_g = jnp.log(1.0 - jnp.exp2(-5.0 - jnp.arange(H, dtype=jnp.float32)))[:, None]
    j = jnp.arange(CHUNK, dtype=jnp.float32)
    intra_decay = jnp.exp(log_g[:, :, None] * jnp.abs(j[:, None] - j[None, :]))
    q_dec = jnp.exp(log_g * (j + 1.0))[None, :, :, None]
    k_dec = jnp.exp(log_g * (CHUNK - 1.0 - j))[None, :, :, None]
    chunk_dec = jnp.exp(log_g * CHUNK)[None, :, :, None]

    qc = q.reshape(B, H, n_chunks, CHUNK, dk)
    kc = k.reshape(B, H, n_chunks, CHUNK, dk)
    vc = v.reshape(B, H, n_chunks, CHUNK, dv)
    scores = jnp.einsum('bhncd,bhnmd->bhncm', qc, kc) * intra_decay[None, :, None]
    intra = jnp.einsum('bhncm,bhnme->bhnce', scores, vc)

    def step(state, inp):
        qn, kn, vn = inp
        cross = jnp.einsum('bhcd,bhde->bhce', qn * q_dec, state)
        state = state * chunk_dec + jnp.einsum('bhcd,bhce->bhde', kn * k_dec, vn)
        return state, cross

    xs = (jnp.moveaxis(qc, 2, 0), jnp.moveaxis(kc, 2, 0), jnp.moveaxis(vc, 2, 0))
    state0 = jnp.zeros((B, H, dk, dv), jnp.float32)
    _, cross = lax.scan(step, state0, xs)
    out = intra + jnp.moveaxis(cross, 0, 2)
    return out.reshape(B, H, S, dv)


def retention_mixer(h, w_in, w_out):
    B, S, _ = h.shape
    proj = h @ w_in
    q = proj[..., :RET_QK_DIM].astype(jnp.float32).reshape(B, S, RET_HEADS, RET_DK)
    k = proj[..., RET_QK_DIM:2 * RET_QK_DIM].astype(jnp.float32).reshape(B, S, RET_HEADS, RET_DK)
    v = proj[..., 2 * RET_QK_DIM:2 * RET_QK_DIM + RET_V_DIM].astype(jnp.float32).reshape(B, S, RET_HEADS, RET_DV)
    g = proj[..., 2 * RET_QK_DIM + RET_V_DIM:].astype(jnp.float32)
    pos = jnp.arange(S, dtype=jnp.float32)
    q = rope(q, pos)
    k = rope(k, pos) * (RET_DK ** -0.5)
    o = chunk_retention(q.transpose(0, 2, 1, 3), k.transpose(0, 2, 1, 3), v.transpose(0, 2, 1, 3))
    mu = jnp.mean(o, axis=-1, keepdims=True)
    var = jnp.mean(jnp.square(o - mu), axis=-1, keepdims=True)
    o = (o - mu) * lax.rsqrt(var + EPS)
    o = o.transpose(0, 2, 1, 3).reshape(B, S, RET_V_DIM)
    return (jax.nn.silu(g) * o).astype(h.dtype) @ w_out


def pool_mixer(h, w_group, scale):
    B, S, D = h.shape
    hg = h.astype(jnp.float32).reshape(B, S, POOL_GROUPS, POOL_GROUP_DIM)
    cs = jnp.concatenate([jnp.zeros((B, 1, POOL_GROUPS, POOL_GROUP_DIM), jnp.float32),
                          jnp.cumsum(hg, axis=1)], axis=1)
    t = jnp.arange(S)[:, None]
    win = jnp.array(POOL_WINDOWS, dtype=jnp.int32)[None, :]
    lo = jnp.maximum(t + 1 - win, 0)
    cnt = (t + 1 - lo).astype(jnp.float32)
    gidx = jnp.arange(POOL_GROUPS)[None, :]
    window_sum = cs[:, 1:] - cs[:, lo, gidx]
    pooled = window_sum / cnt[None, :, :, None] - hg
    y = jnp.einsum('bsgc,gcd->bsgd', pooled.astype(h.dtype), w_group).reshape(B, S, D)
    return y * scale


def hier_moe(h, w_group_router, b_group_router, w_expert_router, b_expert_router,
             w_gate, w_up, w_down):
    B, S, D = h.shape
    T = B * S
    xf = h.reshape(T, D)
    g_logits = (xf @ w_group_router).astype(jnp.float32) + b_group_router.astype(jnp.float32)
    g_prob = jax.nn.softmax(g_logits, axis=-1)
    grp = jnp.argmax(g_logits, axis=-1)
    g_gate = jnp.take_along_axis(g_prob, grp[:, None], axis=-1)
    e_logits = (xf @ w_expert_router).astype(jnp.float32).reshape(T, MOE_GROUPS, MOE_EXPERTS_PER_GROUP)
    e_logits = e_logits + b_expert_router.astype(jnp.float32)[None]
    e_sel = jnp.take_along_axis(e_logits, grp[:, None, None], axis=1)[:, 0]
    top_p, top_i = lax.top_k(jax.nn.softmax(e_sel, axis=-1), MOE_TOP_K)
    top_p = top_p / jnp.sum(top_p, axis=-1, keepdims=True)
    weights = g_gate * top_p
    expert = grp[:, None] * MOE_EXPERTS_PER_GROUP + top_i

    P = T * MOE_TOP_K
    n_blocks = -(-P // MOE_BLOCK_ROWS) + MOE_EXPERTS
    e_flat = expert.reshape(P)
    tok_flat = jnp.repeat(jnp.arange(T, dtype=jnp.int32), MOE_TOP_K)
    w_flat = weights.reshape(P)
    order = jnp.argsort(e_flat)
    e_s, tok_s, w_s = e_flat[order], tok_flat[order], w_flat[order]
    counts = jax.ops.segment_sum(jnp.ones((P,), jnp.int32), e_flat, num_segments=MOE_EXPERTS)
    starts = jnp.cumsum(counts) - counts
    padded = ((counts + MOE_BLOCK_ROWS - 1) // MOE_BLOCK_ROWS) * MOE_BLOCK_ROWS
    pad_ends = jnp.cumsum(padded)
    pad_starts = pad_ends - padded
    dest = pad_starts[e_s] + (jnp.arange(P, dtype=jnp.int32) - starts[e_s])
    buf_tok = jnp.full((n_blocks * MOE_BLOCK_ROWS,), T, jnp.int32).at[dest].set(tok_s)
    buf_w = jnp.zeros((n_blocks * MOE_BLOCK_ROWS,), jnp.float32).at[dest].set(w_s)
    blk_start = jnp.arange(n_blocks, dtype=jnp.int32) * MOE_BLOCK_ROWS
    blk_expert = jnp.minimum(jnp.searchsorted(pad_ends, blk_start, side='right'), MOE_EXPERTS - 1)
    x_pad = jnp.concatenate([xf, jnp.zeros((1, D), xf.dtype)], axis=0)

    def run_block(args):
        idx, e = args
        xb = x_pad[idx]
        hid = jax.nn.silu(xb @ w_gate[e]) * (xb @ w_up[e])
        return hid @ w_down[e]

    ys = lax.map(run_block, (buf_tok.reshape(n_blocks, MOE_BLOCK_ROWS), blk_expert))
    ys = ys * buf_w.reshape(n_blocks, MOE_BLOCK_ROWS, 1).astype(ys.dtype)
    out = jnp.zeros((T + 1, D), xf.dtype).at[buf_tok].add(ys.reshape(-1, D))[:T]
    return out.reshape(B, S, D)


def setup_inputs(seed: int = 0) -> dict:
    key = jax.random.key(seed)
    ks = jax.random.split(key, 16)
    n_ret = (DEPTH + N_MIXERS - 1) // N_MIXERS
    n_pool = DEPTH // N_MIXERS
    nrm = jax.random.normal
    x = nrm(ks[0], (BATCH, SEQ, D_MODEL), jnp.float32)
    norm_mix = 1.0 + 0.02 * nrm(ks[1], (DEPTH, D_MODEL), jnp.float32)
    norm_ffn = 1.0 + 0.02 * nrm(ks[2], (DEPTH, D_MODEL), jnp.float32)
    ret_w_in = nrm(ks[3], (n_ret, D_MODEL, RET_IN_DIM), jnp.float32) * D_MODEL ** -0.5
    ret_w_out = nrm(ks[4], (n_ret, RET_V_DIM, D_MODEL), jnp.float32) * RET_V_DIM ** -0.5
    pool_w = nrm(ks[5], (n_pool, POOL_GROUPS, POOL_GROUP_DIM, POOL_GROUP_DIM), jnp.float32) * POOL_GROUP_DIM ** -0.5
    pool_scale = 0.5 + 0.1 * nrm(ks[6], (n_pool, D_MODEL), jnp.float32)
    moe_w_group = nrm(ks[7], (DEPTH, D_MODEL, MOE_GROUPS), jnp.float32) * D_MODEL ** -0.5
    moe_b_group = 0.01 * nrm(ks[8], (DEPTH, MOE_GROUPS), jnp.float32)
    moe_w_expert = nrm(ks[9], (DEPTH, D_MODEL, MOE_EXPERTS), jnp.float32) * D_MODEL ** -0.5
    moe_b_expert = 0.01 * nrm(ks[10], (DEPTH, MOE_GROUPS, MOE_EXPERTS_PER_GROUP), jnp.float32)
    moe_w_gate = nrm(ks[11], (DEPTH, MOE_EXPERTS, D_MODEL, MOE_D_FF), jnp.float32) * D_MODEL ** -0.5
    moe_w_up = nrm(ks[12], (DEPTH, MOE_EXPERTS, D_MODEL, MOE_D_FF), jnp.float32) * D_MODEL ** -0.5
    moe_w_down = nrm(ks[13], (DEPTH, MOE_EXPERTS, MOE_D_FF, D_MODEL), jnp.float32) * MOE_D_FF ** -0.5
    final_norm = 1.0 + 0.02 * nrm(ks[14], (D_MODEL,), jnp.float32)
    return {"x": x, "norm_mix": norm_mix, "norm_ffn": norm_ffn,
            "ret_w_in": ret_w_in, "ret_w_out": ret_w_out,
            "pool_w": pool_w, "pool_scale": pool_scale,
            "moe_w_group": moe_w_group, "moe_b_group": moe_b_group,
            "moe_w_expert": moe_w_expert, "moe_b_expert": moe_b_expert,
            "moe_w_gate": moe_w_gate, "moe_w_up": moe_w_up, "moe_w_down": moe_w_down,
            "final_norm": final_norm}


def reference(x, norm_mix, norm_ffn, ret_w_in, ret_w_out, pool_w, pool_scale,
              moe_w_group, moe_b_group, moe_w_expert, moe_b_expert,
              moe_w_gate, moe_w_up, moe_w_down, final_norm):
    for layer in range(DEPTH):
        h = rmsnorm(x, norm_mix[layer])
        slot = layer // N_MIXERS
        if layer % N_MIXERS == 0:
            y = retention_mixer(h, ret_w_in[slot], ret_w_out[slot])
        else:
            y = pool_mixer(h, pool_w[slot], pool_scale[slot])
        x = x + y
        h = rmsnorm(x, norm_ffn[layer])
        x = x + hier_moe(h, moe_w_group[layer], moe_b_group[layer],
                         moe_w_expert[layer], moe_b_expert[layer],
                         moe_w_gate[layer], moe_w_up[layer], moe_w_down[layer])
    return rmsnorm(x, final_norm)
```

```python
import functools

import numpy as np
import jax
import jax.numpy as jnp
from jax import lax
from jax.experimental import pallas as pl
from jax.experimental.pallas import tpu as pltpu

D_MODEL = 1024
EPS = 1e-6
CHUNK = 64

RET_HEADS = 4
RET_DK = 256
RET_DV = 512
RET_QK_DIM = RET_HEADS * RET_DK
RET_V_DIM = RET_HEADS * RET_DV
ROPE_BASE = 10000.0

POOL_WINDOWS = (2, 4, 8, 16)
POOL_GROUP_DIM = 256
POOL_HIST = 16

MOE_GROUPS = 4
MOE_EPG = 8
MOE_EXPERTS = MOE_GROUPS * MOE_EPG
MOE_D_FF = 256
PAIRS_PER_GROUP = MOE_EPG * (MOE_EPG - 1) // 2
N_CLASSES = MOE_GROUPS * PAIRS_PER_GROUP

LANES = 128
SUBLANES = 8

RET_TILE = 256
POOL_TILE = 512
MOE_ROWS = 128
VMEM_LIMIT = 56 * 1024 * 1024


def _rms(x, gain):
    ms = jnp.mean(x * x, axis=-1, keepdims=True)
    return x * lax.rsqrt(ms + EPS) * gain


def _silu(x):
    return x * jax.nn.sigmoid(x)


def _dot(a, b):
    return jnp.dot(a, b, preferred_element_type=jnp.float32)


def _route(x1, gain2, wr, br):
    rows = x1.shape[0]
    h2 = _rms(x1, gain2)
    logits = jnp.dot(h2, wr, preferred_element_type=jnp.float32,
                     precision=lax.Precision.HIGHEST) + br
    lane = lax.broadcasted_iota(jnp.int32, (rows, LANES), 1)
    neg = jnp.float32(-jnp.inf)
    gmask = lane < MOE_GROUPS
    gl = jnp.where(gmask, logits, neg)
    gmax = jnp.max(gl, axis=1, keepdims=True)
    grp = jnp.min(jnp.where(gl == gmax, lane, LANES), axis=1, keepdims=True)
    gsum = jnp.sum(jnp.where(gmask, jnp.exp(gl - gmax), 0.0), axis=1, keepdims=True)
    g_gate = 1.0 / gsum
    first = MOE_GROUPS + MOE_EPG * grp
    emask = (lane >= first) & (lane < first + MOE_EPG)
    el = jnp.where(emask, logits, neg)
    m1 = jnp.max(el, axis=1, keepdims=True)
    i1 = jnp.min(jnp.where(el == m1, lane, LANES), axis=1, keepdims=True)
    el2 = jnp.where(lane == i1, neg, el)
    m2 = jnp.max(el2, axis=1, keepdims=True)
    i2 = jnp.min(jnp.where(el2 == m2, lane, LANES), axis=1, keepdims=True)
    t = jnp.exp(m2 - m1)
    w1 = g_gate / (1.0 + t)
    w2 = g_gate * t / (1.0 + t)
    first_lo = i1 < i2
    lo = jnp.where(first_lo, i1, i2) - MOE_GROUPS
    hi = jnp.where(first_lo, i2, i1) - MOE_GROUPS
    w_lo = jnp.where(first_lo, w1, w2)
    w_hi = jnp.where(first_lo, w2, w1)
    packed = jnp.where(lane == 0, lo.astype(jnp.float32),
             jnp.where(lane == 1, hi.astype(jnp.float32),
             jnp.where(lane == 2, w_lo,
             jnp.where(lane == 3, w_hi, 0.0))))
    return packed.T[:SUBLANES, :]


def _ret_kernel(x_ref, gain_ref, win_ref, wout_ref, cos_ref, sin_ref, dmat_ref,
                qdec_ref, kdec_ref, cdec_ref, gain2_ref, wr_ref, br_ref,
                x1_ref, route_ref, state_ref, gated_ref):
    @pl.when(pl.program_id(1) == 0)
    def _():
        state_ref[...] = jnp.zeros_like(state_ref)

    x = x_ref[...]
    h = _rms(x, gain_ref[...]).astype(jnp.bfloat16)
    cos = cos_ref[...]
    sin = sin_ref[...]
    half = RET_DK // 2

    def rope(t):
        t1, t2 = t[:, :half], t[:, half:]
        return jnp.concatenate([t1 * cos - t2 * sin, t1 * sin + t2 * cos], axis=1)

    for hd in range(RET_HEADS):
        q = _dot(h, win_ref[:, hd * RET_DK:(hd + 1) * RET_DK])
        k = _dot(h, win_ref[:, RET_QK_DIM + hd * RET_DK:RET_QK_DIM + (hd + 1) * RET_DK])
        v0 = 2 * RET_QK_DIM + hd * RET_DV
        v = _dot(h, win_ref[:, v0:v0 + RET_DV])
        g0 = 2 * RET_QK_DIM + RET_V_DIM + hd * RET_DV
        g = _dot(h, win_ref[:, g0:g0 + RET_DV])
        q = rope(q)
        k = rope(k) * (RET_DK ** -0.5)
        qb = q.astype(jnp.bfloat16)
        kb = k.astype(jnp.bfloat16)
        vb = v.astype(jnp.bfloat16)
        sc = lax.dot_general(qb, kb, (((1,), (1,)), ((), ())),
                             preferred_element_type=jnp.float32)
        sc = sc * dmat_ref[hd]
        intra = _dot(sc.astype(jnp.bfloat16), vb)
        st = state_ref[hd]
        cross = _dot(qb, st.astype(jnp.bfloat16)) * qdec_ref[hd]
        o = intra + cross
        kd = (k * kdec_ref[hd]).astype(jnp.bfloat16)
        upd = lax.dot_general(kd, vb, (((0,), (0,)), ((), ())),
                              preferred_element_type=jnp.float32)
        state_ref[hd] = st * cdec_ref[hd] + upd
        mu = jnp.mean(o, axis=-1, keepdims=True)
        oc = o - mu
        var = jnp.mean(oc * oc, axis=-1, keepdims=True)
        on = oc * lax.rsqrt(var + EPS)
        gated_ref[:, hd * RET_DV:(hd + 1) * RET_DV] = (_silu(g) * on).astype(jnp.bfloat16)

    x1 = x + _dot(gated_ref[...], wout_ref[...])
    x1_ref[...] = x1
    route_ref[...] = _route(x1, gain2_ref[...], wr_ref[...], br_ref[...])


def _const_spec(shape):
    nd = len(shape)
    return pl.BlockSpec(shape, lambda *_: (0,) * nd, pipeline_mode=pl.Buffered(1))


def _retention_layer(x2d, batch, seq, gain, w_in, w_out, gain2, wr, br):
    tb = RET_TILE
    n_s = seq // tb
    total = batch * seq
    log_g = jnp.log(1.0 - jnp.exp2(-5.0 - jnp.arange(RET_HEADS, dtype=jnp.float32)))
    n = jnp.arange(tb, dtype=jnp.float32)
    diff = n[:, None] - n[None, :]
    cn = jnp.arange(tb)[:, None] // CHUNK
    cm = jnp.arange(tb)[None, :] // CHUNK
    expo = jnp.where(cn == cm, jnp.abs(diff), diff)
    dmat = jnp.where((cm <= cn)[None], jnp.exp(log_g[:, None, None] * expo[None]), 0.0)
    qdec = jnp.exp(log_g[:, None] * (n + 1.0)[None])[:, :, None]
    kdec = jnp.exp(log_g[:, None] * (tb - 1.0 - n)[None])[:, :, None]
    cdec = jnp.broadcast_to(jnp.exp(log_g * tb)[:, None, None], (RET_HEADS, 1, RET_DV))
    half = RET_DK // 2
    inv_freq = 1.0 / (ROPE_BASE ** (jnp.arange(half, dtype=jnp.float32) / half))
    ang = jnp.arange(seq, dtype=jnp.float32)[:, None] * inv_freq[None, :]
    cos, sin = jnp.cos(ang), jnp.sin(ang)

    tile = lambda b, s: (b * n_s + s, 0)
    return pl.pallas_call(
        _ret_kernel,
        grid=(batch, n_s),
        in_specs=[
            pl.BlockSpec((tb, D_MODEL), tile),
            _const_spec((1, D_MODEL)),
            _const_spec(w_in.shape),
            _const_spec(w_out.shape),
            pl.BlockSpec((tb, half), lambda b, s: (s, 0)),
            pl.BlockSpec((tb, half), lambda b, s: (s, 0)),
            _const_spec(dmat.shape),
            _const_spec(qdec.shape),
            _const_spec(kdec.shape),
            _const_spec(cdec.shape),
            _const_spec((1, D_MODEL)),
            _const_spec(wr.shape),
            _const_spec(br.shape),
        ],
        out_specs=[
            pl.BlockSpec((tb, D_MODEL), tile),
            pl.BlockSpec((SUBLANES, tb), lambda b, s: (0, b * n_s + s)),
        ],
        out_shape=[
            jax.ShapeDtypeStruct((total, D_MODEL), jnp.float32),
            jax.ShapeDtypeStruct((SUBLANES, total), jnp.float32),
        ],
        scratch_shapes=[
            pltpu.VMEM((RET_HEADS, RET_DK, RET_DV), jnp.float32),
            pltpu.VMEM((tb, RET_V_DIM), jnp.bfloat16),
        ],
        compiler_params=pltpu.CompilerParams(
            dimension_semantics=("arbitrary", "arbitrary"),
            vmem_limit_bytes=VMEM_LIMIT),
        name="retention_layer",
    )(x2d, gain, w_in, w_out, cos, sin, dmat, qdec, kdec, cdec, gain2, wr, br)


def _pool_kernel(x_ref, gain_ref, pw_ref, scale_ref, gain2_ref, wr_ref, br_ref,
                 x1_ref, route_ref, ext_ref, y_ref):
    s = pl.program_id(1)
    tb = x_ref.shape[0]

    @pl.when(s == 0)
    def _():
        ext_ref[0:POOL_HIST, :] = jnp.zeros((POOL_HIST, D_MODEL), jnp.float32)

    x = x_ref[...]
    ext_ref[POOL_HIST:, :] = _rms(x, gain_ref[...])
    pos = s * tb + lax.broadcasted_iota(jnp.int32, (tb, 1), 0)
    for g, w in enumerate(POOL_WINDOWS):
        cs = slice(g * POOL_GROUP_DIM, (g + 1) * POOL_GROUP_DIM)
        e = ext_ref[:, cs]
        acc = e
        k = 1
        while k < w:
            acc = acc + pltpu.roll(acc, k, axis=0)
            k *= 2
        inv_cnt = 1.0 / jnp.minimum(pos + 1, w).astype(jnp.float32)
        pooled = acc[POOL_HIST:, :] * inv_cnt - e[POOL_HIST:, :]
        y_ref[:, cs] = _dot(pooled.astype(jnp.bfloat16), pw_ref[g])
    ext_ref[0:POOL_HIST, :] = ext_ref[tb:tb + POOL_HIST, :]
    x1 = x + y_ref[...] * scale_ref[...]
    x1_ref[...] = x1
    route_ref[...] = _route(x1, gain2_ref[...], wr_ref[...], br_ref[...])


def _pool_layer(x2d, batch, seq, gain, pw, scale, gain2, wr, br):
    tb = POOL_TILE
    n_s = seq // tb
    total = batch * seq
    tile = lambda b, s: (b * n_s + s, 0)
    return pl.pallas_call(
        _pool_kernel,
        grid=(batch, n_s),
        in_specs=[
            pl.BlockSpec((tb, D_MODEL), tile),
            _const_spec((1, D_MODEL)),
            _const_spec(pw.shape),
            _const_spec((1, D_MODEL)),
            _const_spec((1, D_MODEL)),
            _const_spec(wr.shape),
            _const_spec(br.shape),
        ],
        out_specs=[
            pl.BlockSpec((tb, D_MODEL), tile),
            pl.BlockSpec((SUBLANES, tb), lambda b, s: (0, b * n_s + s)),
        ],
        out_shape=[
            jax.ShapeDtypeStruct((total, D_MODEL), jnp.float32),
            jax.ShapeDtypeStruct((SUBLANES, total), jnp.float32),
        ],
        scratch_shapes=[
            pltpu.VMEM((POOL_HIST + tb, D_MODEL), jnp.float32),
            pltpu.VMEM((tb, D_MODEL), jnp.float32),
        ],
        compiler_params=pltpu.CompilerParams(
            dimension_semantics=("arbitrary", "arbitrary"),
            vmem_limit_bytes=VMEM_LIMIT),
        name="pool_layer",
    )(x2d, gain, pw, scale, gain2, wr, br)


def _moe_kernel(tok_ref, nv_ref, ea_ref, eb_ref,
                x_hbm, wts_ref, gain_ref, fgain_ref, wgu_a_ref, wd_a_ref, wgu_b_ref, wd_b_ref,
                out_hbm, xbuf, obuf, sem_in, sem_out, *, final_norm):
    b = pl.program_id(0)
    nb = pl.num_programs(0)
    slot = b % 2

    def gather_row(blk, r, sl):
        tok = tok_ref[blk * MOE_ROWS + r]
        return pltpu.make_async_copy(x_hbm.at[pl.ds(tok, 1)], xbuf.at[sl, pl.ds(r, 1)],
                                     sem_in.at[sl])

    def scatter_row(blk, r, sl):
        tok = tok_ref[blk * MOE_ROWS + r]
        return pltpu.make_async_copy(obuf.at[sl, pl.ds(r, 1)], out_hbm.at[pl.ds(tok, 1)],
                                     sem_out.at[sl])

    def start_gather(blk, sl):
        def body(r, c):
            gather_row(blk, r, sl).start()
            return c
        lax.fori_loop(0, MOE_ROWS, body, 0)

    def wait_gather(blk, sl):
        def body(r, c):
            gather_row(blk, r, sl).wait()
            return c
        lax.fori_loop(0, MOE_ROWS, body, 0)

    def start_scatter(blk, sl):
        def body(r, c):
            scatter_row(blk, r, sl).start()
            return c
        lax.fori_loop(0, nv_ref[blk], body, 0)

    def wait_scatter(blk, sl):
        def body(r, c):
            scatter_row(blk, r, sl).wait()
            return c
        lax.fori_loop(0, nv_ref[blk], body, 0)

    @pl.when(jnp.logical_and(b == 0, nv_ref[0] > 0))
    def _():
        start_gather(0, 0)

    nxt = jnp.minimum(b + 1, nb - 1)

    @pl.when(jnp.logical_and(b + 1 < nb, nv_ref[nxt] > 0))
    def _():
        start_gather(b + 1, 1 - slot)

    prev2 = jnp.maximum(b - 2, 0)

    @pl.when(jnp.logical_and(b >= 2, nv_ref[prev2] > 0))
    def _():
        wait_scatter(b - 2, slot)

    @pl.when(nv_ref[b] > 0)
    def _():
        wait_gather(b, slot)
        xb = xbuf[slot]
        hb = _rms(xb, gain_ref[...]).astype(jnp.bfloat16)
        wt = wts_ref[0].T

        def expert(wgu_ref, wd_ref):
            gu = _dot(hb, wgu_ref[0])
            hid = _silu(gu[:, :MOE_D_FF]) * gu[:, MOE_D_FF:]
            return _dot(hid.astype(jnp.bfloat16), wd_ref[0])

        o = xb + wt[:, 0:1] * expert(wgu_a_ref, wd_a_ref) + wt[:, 1:2] * expert(wgu_b_ref, wd_b_ref)
        if final_norm:
            o = _rms(o, fgain_ref[...])
        obuf[slot] = o
        start_scatter(b, slot)

    prev1 = jnp.maximum(b - 1, 0)

    @pl.when(b == nb - 1)
    def _():
        @pl.when(jnp.logical_and(b >= 1, nv_ref[prev1] > 0))
        def _():
            wait_scatter(b - 1, 1 - slot)

        @pl.when(nv_ref[b] > 0)
        def _():
            wait_scatter(b, slot)


def _class_tables():
    ea = np.zeros((N_CLASSES,), np.int32)
    eb = np.zeros((N_CLASSES,), np.int32)
    for g in range(MOE_GROUPS):
        c = g * PAIRS_PER_GROUP
        for a in range(MOE_EPG):
            for b in range(a + 1, MOE_EPG):
                ea[c], eb[c] = g * MOE_EPG + a, g * MOE_EPG + b
                c += 1
    return ea, eb


def _moe_layer(x2d, route, gain, fgain, wgu, wd, final_norm):
    total = x2d.shape[0]
    nb = total // MOE_ROWS + N_CLASSES
    lo = route[0].astype(jnp.int32)
    hi = route[1].astype(jnp.int32)
    w_lo, w_hi = route[2], route[3]
    a = lo % MOE_EPG
    bb = hi % MOE_EPG
    cls = (lo // MOE_EPG) * PAIRS_PER_GROUP + (a * (2 * MOE_EPG - 1 - a)) // 2 + (bb - a - 1)
    order = jnp.argsort(cls, stable=True).astype(jnp.int32)
    counts = jnp.sum((cls[:, None] == jnp.arange(N_CLASSES, dtype=jnp.int32)[None, :]).astype(jnp.int32),
                     axis=0)
    starts = jnp.cumsum(counts) - counts
    nblk = (counts + MOE_ROWS - 1) // MOE_ROWS
    blk_end = jnp.cumsum(nblk)
    blk_first = blk_end - nblk
    bidx = jnp.arange(nb, dtype=jnp.int32)
    used = bidx < blk_end[-1]
    last_cls = jnp.max(jnp.where(counts > 0, jnp.arange(N_CLASSES, dtype=jnp.int32), 0))
    c = jnp.where(used, jnp.searchsorted(blk_end, bidx, side='right').astype(jnp.int32), last_cls)
    c = jnp.minimum(c, N_CLASSES - 1)
    j = bidx - blk_first[c]
    base = starts[c] + j * MOE_ROWS
    nvalid = jnp.where(used, jnp.clip(counts[c] - j * MOE_ROWS, 0, MOE_ROWS), 0).astype(jnp.int32)
    r = jnp.arange(MOE_ROWS, dtype=jnp.int32)
    valid = r[None, :] < nvalid[:, None]
    src = jnp.clip(base[:, None] + r[None, :], 0, total - 1)
    tok = jnp.where(valid, order[src], 0)
    wts = jnp.zeros((nb, SUBLANES, MOE_ROWS), jnp.float32)
    wts = wts.at[:, 0, :].set(jnp.where(valid, w_lo[tok], 0.0))
    wts = wts.at[:, 1, :].set(jnp.where(valid, w_hi[tok], 0.0))
    ea_tab, eb_tab = _class_tables()
    ea = jnp.asarray(ea_tab)[c]
    eb = jnp.asarray(eb_tab)[c]

    grid_spec = pltpu.PrefetchScalarGridSpec(
        num_scalar_prefetch=4,
        grid=(nb,),
        in_specs=[
            pl.BlockSpec(memory_space=pl.ANY),
            pl.BlockSpec((1, SUBLANES, MOE_ROWS), lambda b, *_: (b, 0, 0)),
            pl.BlockSpec((1, D_MODEL), lambda b, *_: (0, 0)),
            pl.BlockSpec((1, D_MODEL), lambda b, *_: (0, 0)),
            pl.BlockSpec((1, D_MODEL, 2 * MOE_D_FF), lambda b, tok, nv, ea, eb: (ea[b], 0, 0)),
            pl.BlockSpec((1, MOE_D_FF, D_MODEL), lambda b, tok, nv, ea, eb: (ea[b], 0, 0)),
            pl.BlockSpec((1, D_MODEL, 2 * MOE_D_FF), lambda b, tok, nv, ea, eb: (eb[b], 0, 0)),
            pl.BlockSpec((1, MOE_D_FF, D_MODEL), lambda b, tok, nv, ea, eb: (eb[b], 0, 0)),
        ],
        out_specs=pl.BlockSpec(memory_space=pl.ANY),
        scratch_shapes=[
            pltpu.VMEM((2, MOE_ROWS, D_MODEL), jnp.float32),
            pltpu.VMEM((2, MOE_ROWS, D_MODEL), jnp.float32),
            pltpu.SemaphoreType.DMA((2,)),
            pltpu.SemaphoreType.DMA((2,)),
        ],
    )
    return pl.pallas_call(
        functools.partial(_moe_kernel, final_norm=final_norm),
        grid_spec=grid_spec,
        out_shape=jax.ShapeDtypeStruct((total, D_MODEL), jnp.float32),
        compiler_params=pltpu.CompilerParams(
            dimension_semantics=("arbitrary",),
            vmem_limit_bytes=VMEM_LIMIT),
        name="moe_layer",
    )(tok.reshape(-1), nvalid, ea, eb, x2d, wts, gain, fgain, wgu, wd, wgu, wd)


def _router_params(w_group, b_group, w_expert, b_expert):
    pad = LANES - MOE_GROUPS - MOE_EXPERTS
    wr = jnp.concatenate([w_group, w_expert, jnp.zeros((D_MODEL, pad), jnp.float32)], axis=1)
    br = jnp.concatenate([b_group, b_expert.reshape(-1), jnp.zeros((pad,), jnp.float32)])[None, :]
    return wr, br


def kernel(x, norm_mix, norm_ffn, ret_w_in, ret_w_out, pool_w, pool_scale, moe_w_group, moe_b_group, moe_w_expert, moe_b_expert, moe_w_gate, moe_w_up, moe_w_down, final_norm):
    batch, seq, _ = x.shape
    x2d = x.reshape(batch * seq, D_MODEL)
    bf = jnp.bfloat16
    wgu = jnp.concatenate([moe_w_gate, moe_w_up], axis=-1).astype(bf)
    wd = moe_w_down.astype(bf)
    fgain = final_norm[None, :]

    wr0, br0 = _router_params(moe_w_group[0], moe_b_group[0], moe_w_expert[0], moe_b_expert[0])
    x1, route0 = _retention_layer(x2d, batch, seq, norm_mix[0][None, :], ret_w_in[0].astype(bf),
                                  ret_w_out[0].astype(bf), norm_ffn[0][None, :], wr0, br0)
    x2 = _moe_layer(x1, route0, norm_ffn[0][None, :], fgain, wgu[0], wd[0], final_norm=False)

    wr1, br1 = _router_params(moe_w_group[1], moe_b_group[1], moe_w_expert[1], moe_b_expert[1])
    x3, route1 = _pool_layer(x2, batch, seq, norm_mix[1][None, :], pool_w[0].astype(bf),
                             pool_scale[0][None, :], norm_ffn[1][None, :], wr1, br1)
    out = _moe_layer(x3, route1, norm_ffn[1][None, :], fgain, wgu[1], wd[1], final_norm=True)
    return out.reshape(batch, seq, D_MODEL)
```

```python
import functools

import numpy as np
import jax
import jax.numpy as jnp
from jax import lax
from jax.experimental import pallas as pl
from jax.experimental.pallas import tpu as pltpu

D_MODEL = 1024
EPS = 1e-6
CHUNK = 64

RET_HEADS = 4
RET_DK = 256
RET_DV = 512
RET_QK_DIM = RET_HEADS * RET_DK
RET_V_DIM = RET_HEADS * RET_DV
ROPE_BASE = 10000.0

POOL_WINDOWS = (2, 4, 8, 16)
POOL_GROUP_DIM = 256
POOL_HIST = 16

MOE_GROUPS = 4
MOE_EPG = 8
MOE_EXPERTS = MOE_GROUPS * MOE_EPG
MOE_D_FF = 256
PAIRS_PER_GROUP = MOE_EPG * (MOE_EPG - 1) // 2
N_CLASSES = MOE_GROUPS * PAIRS_PER_GROUP

LANES = 128
SUBLANES = 8

RET_TILE = 256
POOL_TILE = 512
MOE_ROWS = 128
VMEM_LIMIT = 56 * 1024 * 1024


def _rms(x, gain):
    ms = jnp.mean(x * x, axis=-1, keepdims=True)
    return x * lax.rsqrt(ms + EPS) * gain


def _silu(x):
    return x * jax.nn.sigmoid(x)


def _dot(a, b):
    return jnp.dot(a, b, preferred_element_type=jnp.float32)


def _route(x1, gain2, wr, br):
    rows = x1.shape[0]
    h2 = _rms(x1, gain2)
    logits = jnp.dot(h2, wr, preferred_element_type=jnp.float32,
                     precision=lax.Precision.HIGHEST) + br
    lane = lax.broadcasted_iota(jnp.int32, (rows, LANES), 1)
    neg = jnp.float32(-jnp.inf)
    gmask = lane < MOE_GROUPS
    gl = jnp.where(gmask, logits, neg)
    gmax = jnp.max(gl, axis=1, keepdims=True)
    grp = jnp.min(jnp.where(gl == gmax, lane, LANES), axis=1, keepdims=True)
    gsum = jnp.sum(jnp.where(gmask, jnp.exp(gl - gmax), 0.0), axis=1, keepdims=True)
    g_gate = 1.0 / gsum
    first = MOE_GROUPS + MOE_EPG * grp
    emask = (lane >= first) & (lane < first + MOE_EPG)
    el = jnp.where(emask, logits, neg)
    m1 = jnp.max(el, axis=1, keepdims=True)
    i1 = jnp.min(jnp.where(el == m1, lane, LANES), axis=1, keepdims=True)
    el2 = jnp.where(lane == i1, neg, el)
    m2 = jnp.max(el2, axis=1, keepdims=True)
    i2 = jnp.min(jnp.where(el2 == m2, lane, LANES), axis=1, keepdims=True)
    t = jnp.exp(m2 - m1)
    w1 = g_gate / (1.0 + t)
    w2 = g_gate * t / (1.0 + t)
    first_lo = i1 < i2
    lo = jnp.where(first_lo, i1, i2) - MOE_GROUPS
    hi = jnp.where(first_lo, i2, i1) - MOE_GROUPS
    w_lo = jnp.where(first_lo, w1, w2)
    w_hi = jnp.where(first_lo, w2, w1)
    packed = jnp.where(lane == 0, lo.astype(jnp.float32),
             jnp.where(lane == 1, hi.astype(jnp.float32),
             jnp.where(lane == 2, w_lo,
             jnp.where(lane == 3, w_hi, 0.0))))
    return packed.T[:SUBLANES, :]


def _ret_kernel(x_ref, gain_ref, win_ref, wout_ref, cos_ref, sin_ref, dmat_ref,
                qdec_ref, kdec_ref, cdec_ref, gain2_ref, wr_ref, br_ref,
                x1_ref, route_ref, state_ref, gated_ref):
    @pl.when(pl.program_id(1) == 0)
    def _():
        state_ref[...] = jnp.zeros_like(state_ref)

    x = x_ref[...]
    h = _rms(x, gain_ref[...]).astype(jnp.bfloat16)
    cos = cos_ref[...]
    sin = sin_ref[...]
    half = RET_DK // 2

    def rope(t):
        t1, t2 = t[:, :half], t[:, half:]
        return jnp.concatenate([t1 * cos - t2 * sin, t1 * sin + t2 * cos], axis=1)

    for hd in range(RET_HEADS):
        q = _dot(h, win_ref[:, hd * RET_DK:(hd + 1) * RET_DK])
        k = _dot(h, win_ref[:, RET_QK_DIM + hd * RET_DK:RET_QK_DIM + (hd + 1) * RET_DK])
        v0 = 2 * RET_QK_DIM + hd * RET_DV
        v = _dot(h, win_ref[:, v0:v0 + RET_DV])
        g0 = 2 * RET_QK_DIM + RET_V_DIM + hd * RET_DV
        g = _dot(h, win_ref[:, g0:g0 + RET_DV])
        q = rope(q)
        k = rope(k) * (RET_DK ** -0.5)
        qb = q.astype(jnp.bfloat16)
        kb = k.astype(jnp.bfloat16)
        vb = v.astype(jnp.bfloat16)
        sc = lax.dot_general(qb, kb, (((1,), (1,)), ((), ())),
                             preferred_element_type=jnp.float32)
        sc = sc * dmat_ref[hd]
        intra = _dot(sc.astype(jnp.bfloat16), vb)
        st = state_ref[hd]
        cross = _dot(qb, st.astype(jnp.bfloat16)) * qdec_ref[hd]
        o = intra + cross
        kd = (k * kdec_ref[hd]).astype(jnp.bfloat16)
        upd = lax.dot_general(kd, vb, (((0,), (0,)), ((), ())),
                              preferred_element_type=jnp.float32)
        state_ref[hd] = st * cdec_ref[hd] + upd
        mu = jnp.mean(o, axis=-1, keepdims=True)
        oc = o - mu
        var = jnp.mean(oc * oc, axis=-1, keepdims=True)
        on = oc * lax.rsqrt(var + EPS)
        gated_ref[:, hd * RET_DV:(hd + 1) * RET_DV] = (_silu(g) * on).astype(jnp.bfloat16)

    x1 = x + _dot(gated_ref[...], wout_ref[...])
    x1_ref[...] = x1
    route_ref[...] = _route(x1, gain2_ref[...], wr_ref[...], br_ref[...])


def _const_spec(shape):
    nd = len(shape)
    return pl.BlockSpec(shape, lambda *_: (0,) * nd, pipeline_mode=pl.Buffered(1))


def _retention_layer(x2d, batch, seq, gain, w_in, w_out, gain2, wr, br):
    tb = RET_TILE
    n_s = seq // tb
    total = batch * seq
    log_g = jnp.log(1.0 - jnp.exp2(-5.0 - jnp.arange(RET_HEADS, dtype=jnp.float32)))
    n = jnp.arange(tb, dtype=jnp.float32)
    diff = n[:, None] - n[None, :]
    cn = jnp.arange(tb)[:, None] // CHUNK
    cm = jnp.arange(tb)[None, :] // CHUNK
    expo = jnp.where(cn == cm, jnp.abs(diff), diff)
    dmat = jnp.where((cm <= cn)[None], jnp.exp(log_g[:, None, None] * expo[None]), 0.0)
    qdec = jnp.exp(log_g[:, None] * (n + 1.0)[None])[:, :, None]
    kdec = jnp.exp(log_g[:, None] * (tb - 1.0 - n)[None])[:, :, None]
    cdec = jnp.broadcast_to(jnp.exp(log_g * tb)[:, None, None], (RET_HEADS, 1, RET_DV))
    half = RET_DK // 2
    inv_freq = 1.0 / (ROPE_BASE ** (jnp.arange(half, dtype=jnp.float32) / half))
    ang = jnp.arange(seq, dtype=jnp.float32)[:, None] * inv_freq[None, :]
    cos, sin = jnp.cos(ang), jnp.sin(ang)

    tile = lambda b, s: (b * n_s + s, 0)
    return pl.pallas_call(
        _ret_kernel,
        grid=(batch, n_s),
        in_specs=[
            pl.BlockSpec((tb, D_MODEL), tile),
            _const_spec((1, D_MODEL)),
            _const_spec(w_in.shape),
            _const_spec(w_out.shape),
            pl.BlockSpec((tb, half), lambda b, s: (s, 0)),
            pl.BlockSpec((tb, half), lambda b, s: (s, 0)),
            _const_spec(dmat.shape),
            _const_spec(qdec.shape),
            _const_spec(kdec.shape),
            _const_spec(cdec.shape),
            _const_spec((1, D_MODEL)),
            _const_spec(wr.shape),
            _const_spec(br.shape),
        ],
        out_specs=[
            pl.BlockSpec((tb, D_MODEL), tile),
            pl.BlockSpec((SUBLANES, tb), lambda b, s: (0, b * n_s + s)),
        ],
        out_shape=[
            jax.ShapeDtypeStruct((total, D_MODEL), jnp.float32),
            jax.ShapeDtypeStruct((SUBLANES, total), jnp.float32),
        ],
        scratch_shapes=[
            pltpu.VMEM((RET_HEADS, RET_DK, RET_DV), jnp.float32),
            pltpu.VMEM((tb, RET_V_DIM), jnp.bfloat16),
        ],
        compiler_params=pltpu.CompilerParams(
            dimension_semantics=("arbitrary", "arbitrary"),
            vmem_limit_bytes=VMEM_LIMIT),
        name="retention_layer",
    )(x2d, gain, w_in, w_out, cos, sin, dmat, qdec, kdec, cdec, gain2, wr, br)


def _pool_kernel(x_ref, gain_ref, pw_ref, scale_ref, gain2_ref, wr_ref, br_ref,
                 x1_ref, route_ref, ext_ref, y_ref):
    s = pl.program_id(1)
    tb = x_ref.shape[0]

    @pl.when(s == 0)
    def _():
        ext_ref[0:POOL_HIST, :] = jnp.zeros((POOL_HIST, D_MODEL), jnp.float32)

    x = x_ref[...]
    ext_ref[POOL_HIST:, :] = _rms(x, gain_ref[...])
    pos = s * tb + lax.broadcasted_iota(jnp.int32, (tb, 1), 0)
    for g, w in enumerate(POOL_WINDOWS):
        cs = slice(g * POOL_GROUP_DIM, (g + 1) * POOL_GROUP_DIM)
        e = ext_ref[:, cs]
        acc = e
        k = 1
        while k < w:
            acc = acc + pltpu.roll(acc, k, axis=0)
            k *= 2
        inv_cnt = 1.0 / jnp.minimum(pos + 1, w).astype(jnp.float32)
        pooled = acc[POOL_HIST:, :] * inv_cnt - e[POOL_HIST:, :]
        y_ref[:, cs] = _dot(pooled.astype(jnp.bfloat16), pw_ref[g])
    ext_ref[0:POOL_HIST, :] = ext_ref[tb:tb + POOL_HIST, :]
    x1 = x + y_ref[...] * scale_ref[...]
    x1_ref[...] = x1
    route_ref[...] = _route(x1, gain2_ref[...], wr_ref[...], br_ref[...])


def _pool_layer(x2d, batch, seq, gain, pw, scale, gain2, wr, br):
    tb = POOL_TILE
    n_s = seq // tb
    total = batch * seq
    tile = lambda b, s: (b * n_s + s, 0)
    return pl.pallas_call(
        _pool_kernel,
        grid=(batch, n_s),
        in_specs=[
            pl.BlockSpec((tb, D_MODEL), tile),
            _const_spec((1, D_MODEL)),
            _const_spec(pw.shape),
            _const_spec((1, D_MODEL)),
            _const_spec((1, D_MODEL)),
            _const_spec(wr.shape),
            _const_spec(br.shape),
        ],
        out_specs=[
            pl.BlockSpec((tb, D_MODEL), tile),
            pl.BlockSpec((SUBLANES, tb), lambda b, s: (0, b * n_s + s)),
        ],
        out_shape=[
            jax.ShapeDtypeStruct((total, D_MODEL), jnp.float32),
            jax.ShapeDtypeStruct((SUBLANES, total), jnp.float32),
        ],
        scratch_shapes=[
            pltpu.VMEM((POOL_HIST + tb, D_MODEL), jnp.float32),
            pltpu.VMEM((tb, D_MODEL), jnp.float32),
        ],
        compiler_params=pltpu.CompilerParams(
            dimension_semantics=("arbitrary", "arbitrary"),
            vmem_limit_bytes=VMEM_LIMIT),
        name="pool_layer",
    )(x2d, gain, pw, scale, gain2, wr, br)


def _moe_kernel(src_ref, dst_ref, nv_ref, ea_ref, eb_ref,
                x_hbm, wts_ref, gain_ref, fgain_ref, wgu_a_ref, wd_a_ref, wgu_b_ref, wd_b_ref,
                out_hbm, xbuf0, xbuf1, obuf0, obuf1, hb_ref, hid_ref, sem_in, sem_out,
                *, final_norm):
    b = pl.program_id(0)
    n = nv_ref[b]
    n_prev = nv_ref[jnp.maximum(b - 1, 0)]

    def start_gather(blk, xdst, sem, first=0, last=MOE_ROWS):
        for r in range(first, last):
            tok = src_ref[blk * MOE_ROWS + r]
            pltpu.make_async_copy(x_hbm.at[pl.ds(tok, 1)], xdst.at[pl.ds(r, 1)], sem).start()

    def wait_gather(xdst, sem):
        pltpu.make_async_copy(x_hbm.at[pl.ds(0, MOE_ROWS)], xdst, sem).wait()

    def start_scatter(blk, osrc, first=0, last=MOE_ROWS):
        for r in range(first, last):
            tok = dst_ref[blk * MOE_ROWS + r]
            pltpu.make_async_copy(osrc.at[pl.ds(r, 1)], out_hbm.at[pl.ds(tok, 1)],
                                  sem_out.at[0]).start()

    def wait_scatter(osrc):
        pltpu.make_async_copy(osrc, out_hbm.at[pl.ds(0, MOE_ROWS)], sem_out.at[0]).wait()

    @pl.when(b == 0)
    def _():
        obuf1[...] = jnp.zeros((MOE_ROWS, D_MODEL), jnp.float32)
        start_gather(0, xbuf0, sem_in.at[0])

    def step(xcur, xnext, ocur, oprev, sem_cur, sem_next):
        @pl.when(n > 0)
        def _():
            half = MOE_ROWS // 2
            wait_gather(xcur, sem_cur)
            start_gather(b + 1, xnext, sem_next, 0, half)
            xb = xcur[...]
            hb_ref[...] = _rms(xb, gain_ref[...]).astype(jnp.bfloat16)
            wt = wts_ref[0].T

            def hidden(wgu_ref):
                gu = _dot(hb_ref[...], wgu_ref[0])
                return (_silu(gu[:, :MOE_D_FF]) * gu[:, MOE_D_FF:]).astype(jnp.bfloat16)

            start_gather(b + 1, xnext, sem_next, half, MOE_ROWS)
            hid_ref[0] = hidden(wgu_a_ref)
            start_scatter(b, oprev, 0, half)
            hid_ref[1] = hidden(wgu_b_ref)
            start_scatter(b, oprev, half, MOE_ROWS)
            o = (xb + wt[:, 0:1] * _dot(hid_ref[0], wd_a_ref[0])
                 + wt[:, 1:2] * _dot(hid_ref[1], wd_b_ref[0]))
            if final_norm:
                o = _rms(o, fgain_ref[...])
            ocur[...] = o
            wait_scatter(oprev)

        @pl.when(jnp.logical_and(n == 0, n_prev > 0))
        def _():
            wait_gather(xcur, sem_cur)
            start_scatter(b, oprev)
            wait_scatter(oprev)

    @pl.when(b % 2 == 0)
    def _():
        step(xbuf0, xbuf1, obuf0, obuf1, sem_in.at[0], sem_in.at[1])

    @pl.when(b % 2 == 1)
    def _():
        step(xbuf1, xbuf0, obuf1, obuf0, sem_in.at[1], sem_in.at[0])


def _class_tables():
    ea = np.zeros((N_CLASSES,), np.int32)
    eb = np.zeros((N_CLASSES,), np.int32)
    for g in range(MOE_GROUPS):
        c = g * PAIRS_PER_GROUP
        for a in range(MOE_EPG):
            for b in range(a + 1, MOE_EPG):
                ea[c], eb[c] = g * MOE_EPG + a, g * MOE_EPG + b
                c += 1
    return ea, eb


def _moe_layer(x2d, route, gain, fgain, wgu, wd, final_norm):
    total = x2d.shape[0]
    nb = total // MOE_ROWS + N_CLASSES + 1
    lo = route[0].astype(jnp.int32)
    hi = route[1].astype(jnp.int32)
    w_lo, w_hi = route[2], route[3]
    a = lo % MOE_EPG
    bb = hi % MOE_EPG
    cls = (lo // MOE_EPG) * PAIRS_PER_GROUP + (a * (2 * MOE_EPG - 1 - a)) // 2 + (bb - a - 1)
    order = jnp.argsort(cls, stable=True).astype(jnp.int32)
    counts = jnp.sum((cls[:, None] == jnp.arange(N_CLASSES, dtype=jnp.int32)[None, :]).astype(jnp.int32),
                     axis=0)
    starts = jnp.cumsum(counts) - counts
    nblk = (counts + MOE_ROWS - 1) // MOE_ROWS
    blk_end = jnp.cumsum(nblk)
    blk_first = blk_end - nblk
    bidx = jnp.arange(nb, dtype=jnp.int32)
    used = bidx < blk_end[-1]
    last_cls = jnp.max(jnp.where(counts > 0, jnp.arange(N_CLASSES, dtype=jnp.int32), 0))
    c = jnp.where(used, jnp.searchsorted(blk_end, bidx, side='right').astype(jnp.int32), last_cls)
    c = jnp.minimum(c, N_CLASSES - 1)
    j = bidx - blk_first[c]
    base = starts[c] + j * MOE_ROWS
    nvalid = jnp.where(used, jnp.clip(counts[c] - j * MOE_ROWS, 0, MOE_ROWS), 0).astype(jnp.int32)
    r = jnp.arange(MOE_ROWS, dtype=jnp.int32)
    valid = r[None, :] < nvalid[:, None]
    src = jnp.clip(base[:, None] + r[None, :], 0, total - 1)
    tok = jnp.where(valid, order[src], 0)
    dump = jnp.broadcast_to(total + r[None, :], (nb, MOE_ROWS))
    dst = jnp.where(valid, tok, dump)
    dst = jnp.concatenate([dump[:1], dst[:-1]], axis=0)
    wts = jnp.zeros((nb, SUBLANES, MOE_ROWS), jnp.float32)
    wts = wts.at[:, 0, :].set(jnp.where(valid, w_lo[tok], 0.0))
    wts = wts.at[:, 1, :].set(jnp.where(valid, w_hi[tok], 0.0))
    ea_tab, eb_tab = _class_tables()
    ea = jnp.asarray(ea_tab)[c]
    eb = jnp.asarray(eb_tab)[c]

    grid_spec = pltpu.PrefetchScalarGridSpec(
        num_scalar_prefetch=5,
        grid=(nb,),
        in_specs=[
            pl.BlockSpec(memory_space=pl.ANY),
            pl.BlockSpec((1, SUBLANES, MOE_ROWS), lambda b, *_: (b, 0, 0)),
            pl.BlockSpec((1, D_MODEL), lambda b, *_: (0, 0)),
            pl.BlockSpec((1, D_MODEL), lambda b, *_: (0, 0)),
            pl.BlockSpec((1, D_MODEL, 2 * MOE_D_FF), lambda b, src, dst, nv, ea, eb: (ea[b], 0, 0)),
            pl.BlockSpec((1, MOE_D_FF, D_MODEL), lambda b, src, dst, nv, ea, eb: (ea[b], 0, 0)),
            pl.BlockSpec((1, D_MODEL, 2 * MOE_D_FF), lambda b, src, dst, nv, ea, eb: (eb[b], 0, 0)),
            pl.BlockSpec((1, MOE_D_FF, D_MODEL), lambda b, src, dst, nv, ea, eb: (eb[b], 0, 0)),
        ],
        out_specs=pl.BlockSpec(memory_space=pl.ANY),
        scratch_shapes=[
            pltpu.VMEM((MOE_ROWS, D_MODEL), jnp.float32),
            pltpu.VMEM((MOE_ROWS, D_MODEL), jnp.float32),
            pltpu.VMEM((MOE_ROWS, D_MODEL), jnp.float32),
            pltpu.VMEM((MOE_ROWS, D_MODEL), jnp.float32),
            pltpu.VMEM((MOE_ROWS, D_MODEL), jnp.bfloat16),
            pltpu.VMEM((2, MOE_ROWS, MOE_D_FF), jnp.bfloat16),
            pltpu.SemaphoreType.DMA((2,)),
            pltpu.SemaphoreType.DMA((1,)),
        ],
    )
    return pl.pallas_call(
        functools.partial(_moe_kernel, final_norm=final_norm),
        grid_spec=grid_spec,
        out_shape=jax.ShapeDtypeStruct((total + MOE_ROWS, D_MODEL), jnp.float32),
        compiler_params=pltpu.CompilerParams(
            dimension_semantics=("arbitrary",),
            vmem_limit_bytes=VMEM_LIMIT),
        name="moe_layer",
    )(tok.reshape(-1), dst.reshape(-1), nvalid, ea, eb, x2d, wts, gain, fgain, wgu, wd, wgu, wd)


def _router_params(w_group, b_group, w_expert, b_expert):
    pad = LANES - MOE_GROUPS - MOE_EXPERTS
    wr = jnp.concatenate([w_group, w_expert, jnp.zeros((D_MODEL, pad), jnp.float32)], axis=1)
    br = jnp.concatenate([b_group, b_expert.reshape(-1), jnp.zeros((pad,), jnp.float32)])[None, :]
    return wr, br


def kernel(x, norm_mix, norm_ffn, ret_w_in, ret_w_out, pool_w, pool_scale, moe_w_group, moe_b_group, moe_w_expert, moe_b_expert, moe_w_gate, moe_w_up, moe_w_down, final_norm):
    batch, seq, _ = x.shape
    x2d = x.reshape(batch * seq, D_MODEL)
    bf = jnp.bfloat16
    wgu = jnp.concatenate([moe_w_gate, moe_w_up], axis=-1).astype(bf)
    wd = moe_w_down.astype(bf)
    fgain = final_norm[None, :]

    wr0, br0 = _router_params(moe_w_group[0], moe_b_group[0], moe_w_expert[0], moe_b_expert[0])
    x1, route0 = _retention_layer(x2d, batch, seq, norm_mix[0][None, :], ret_w_in[0].astype(bf),
                                  ret_w_out[0].astype(bf), norm_ffn[0][None, :], wr0, br0)
    x2 = _moe_layer(x1, route0, norm_ffn[0][None, :], fgain, wgu[0], wd[0], final_norm=False)

    wr1, br1 = _router_params(moe_w_group[1], moe_b_group[1], moe_w_expert[1], moe_b_expert[1])
    x3, route1 = _pool_layer(x2, batch, seq, norm_mix[1][None, :], pool_w[0].astype(bf),
                             pool_scale[0][None, :], norm_ffn[1][None, :], wr1, br1)
    out = _moe_layer(x3, route1, norm_ffn[1][None, :], fgain, wgu[1], wd[1], final_norm=True)
    return out[:batch * seq].reshape(batch, seq, D_MODEL)
```

```python
import functools

import numpy as np
import jax
import jax.numpy as jnp
from jax import lax
from jax.experimental import pallas as pl
from jax.experimental.pallas import tpu as pltpu

D_MODEL = 1024
EPS = 1e-6
CHUNK = 64

RET_HEADS = 4
RET_DK = 256
RET_DV = 512
RET_QK_DIM = RET_HEADS * RET_DK
RET_V_DIM = RET_HEADS * RET_DV
ROPE_BASE = 10000.0

POOL_WINDOWS = (2, 4, 8, 16)
POOL_GROUP_DIM = 256
POOL_HIST = 16

MOE_GROUPS = 4
MOE_EPG = 8
MOE_EXPERTS = MOE_GROUPS * MOE_EPG
MOE_D_FF = 256
PAIRS_PER_GROUP = MOE_EPG * (MOE_EPG - 1) // 2
N_CLASSES = MOE_GROUPS * PAIRS_PER_GROUP

LANES = 128
SUBLANES = 8

RET_TILE = 256
POOL_TILE = 512
MOE_ROWS = 128
VMEM_LIMIT = 56 * 1024 * 1024


def _rms(x, gain):
    ms = jnp.mean(x * x, axis=-1, keepdims=True)
    return x * lax.rsqrt(ms + EPS) * gain


def _silu(x):
    return x * jax.nn.sigmoid(x)


def _dot(a, b):
    return jnp.dot(a, b, preferred_element_type=jnp.float32)


TILES_PER_ROW = D_MODEL // LANES


def _load_rows(ref, rows):
    return jnp.concatenate(
        [ref[pl.ds(s, rows, stride=TILES_PER_ROW), :] for s in range(TILES_PER_ROW)], axis=1)


def _store_rows(ref, val):
    rows = val.shape[0]
    for s in range(TILES_PER_ROW):
        ref[pl.ds(s, rows, stride=TILES_PER_ROW), :] = val[:, s * LANES:(s + 1) * LANES]


def _route(x1, gain2, wr, br):
    rows = x1.shape[0]
    h2 = _rms(x1, gain2)
    logits = jnp.dot(h2, wr, preferred_element_type=jnp.float32,
                     precision=lax.Precision.HIGHEST) + br
    lane = lax.broadcasted_iota(jnp.int32, (rows, LANES), 1)
    neg = jnp.float32(-jnp.inf)
    gmask = lane < MOE_GROUPS
    gl = jnp.where(gmask, logits, neg)
    gmax = jnp.max(gl, axis=1, keepdims=True)
    grp = jnp.min(jnp.where(gl == gmax, lane, LANES), axis=1, keepdims=True)
    gsum = jnp.sum(jnp.where(gmask, jnp.exp(gl - gmax), 0.0), axis=1, keepdims=True)
    g_gate = 1.0 / gsum
    first = MOE_GROUPS + MOE_EPG * grp
    emask = (lane >= first) & (lane < first + MOE_EPG)
    el = jnp.where(emask, logits, neg)
    m1 = jnp.max(el, axis=1, keepdims=True)
    i1 = jnp.min(jnp.where(el == m1, lane, LANES), axis=1, keepdims=True)
    el2 = jnp.where(lane == i1, neg, el)
    m2 = jnp.max(el2, axis=1, keepdims=True)
    i2 = jnp.min(jnp.where(el2 == m2, lane, LANES), axis=1, keepdims=True)
    t = jnp.exp(m2 - m1)
    w1 = g_gate / (1.0 + t)
    w2 = g_gate * t / (1.0 + t)
    first_lo = i1 < i2
    lo = jnp.where(first_lo, i1, i2) - MOE_GROUPS
    hi = jnp.where(first_lo, i2, i1) - MOE_GROUPS
    w_lo = jnp.where(first_lo, w1, w2)
    w_hi = jnp.where(first_lo, w2, w1)
    packed = jnp.where(lane == 0, lo.astype(jnp.float32),
             jnp.where(lane == 1, hi.astype(jnp.float32),
             jnp.where(lane == 2, w_lo,
             jnp.where(lane == 3, w_hi, 0.0))))
    return packed.T[:SUBLANES, :]


def _ret_kernel(x_ref, gain_ref, win_ref, wout_ref, cos_ref, sin_ref, dmat_ref,
                qdec_ref, kdec_ref, cdec_ref, gain2_ref, wr_ref, br_ref,
                x1_ref, route_ref, state_ref, gated_ref):
    @pl.when(pl.program_id(1) == 0)
    def _():
        state_ref[...] = jnp.zeros_like(state_ref)

    x = x_ref[...]
    h = _rms(x, gain_ref[...]).astype(jnp.bfloat16)
    cos = cos_ref[...]
    sin = sin_ref[...]
    half = RET_DK // 2

    def rope(t):
        t1, t2 = t[:, :half], t[:, half:]
        return jnp.concatenate([t1 * cos - t2 * sin, t1 * sin + t2 * cos], axis=1)

    for hd in range(RET_HEADS):
        q = _dot(h, win_ref[:, hd * RET_DK:(hd + 1) * RET_DK])
        k = _dot(h, win_ref[:, RET_QK_DIM + hd * RET_DK:RET_QK_DIM + (hd + 1) * RET_DK])
        v0 = 2 * RET_QK_DIM + hd * RET_DV
        v = _dot(h, win_ref[:, v0:v0 + RET_DV])
        g0 = 2 * RET_QK_DIM + RET_V_DIM + hd * RET_DV
        g = _dot(h, win_ref[:, g0:g0 + RET_DV])
        q = rope(q)
        k = rope(k) * (RET_DK ** -0.5)
        qb = q.astype(jnp.bfloat16)
        kb = k.astype(jnp.bfloat16)
        vb = v.astype(jnp.bfloat16)
        sc = lax.dot_general(qb, kb, (((1,), (1,)), ((), ())),
                             preferred_element_type=jnp.float32)
        sc = sc * dmat_ref[hd]
        intra = _dot(sc.astype(jnp.bfloat16), vb)
        st = state_ref[hd]
        cross = _dot(qb, st.astype(jnp.bfloat16)) * qdec_ref[hd]
        o = intra + cross
        kd = (k * kdec_ref[hd]).astype(jnp.bfloat16)
        upd = lax.dot_general(kd, vb, (((0,), (0,)), ((), ())),
                              preferred_element_type=jnp.float32)
        state_ref[hd] = st * cdec_ref[hd] + upd
        mu = jnp.mean(o, axis=-1, keepdims=True)
        oc = o - mu
        var = jnp.mean(oc * oc, axis=-1, keepdims=True)
        on = oc * lax.rsqrt(var + EPS)
        gated_ref[:, hd * RET_DV:(hd + 1) * RET_DV] = (_silu(g) * on).astype(jnp.bfloat16)

    x1 = x + _dot(gated_ref[...], wout_ref[...])
    _store_rows(x1_ref, x1)
    route_ref[...] = _route(x1, gain2_ref[...], wr_ref[...], br_ref[...])


def _const_spec(shape):
    nd = len(shape)
    return pl.BlockSpec(shape, lambda *_: (0,) * nd, pipeline_mode=pl.Buffered(1))


def _retention_layer(x2d, batch, seq, gain, w_in, w_out, gain2, wr, br):
    tb = RET_TILE
    n_s = seq // tb
    total = batch * seq
    log_g = jnp.log(1.0 - jnp.exp2(-5.0 - jnp.arange(RET_HEADS, dtype=jnp.float32)))
    n = jnp.arange(tb, dtype=jnp.float32)
    diff = n[:, None] - n[None, :]
    cn = jnp.arange(tb)[:, None] // CHUNK
    cm = jnp.arange(tb)[None, :] // CHUNK
    expo = jnp.where(cn == cm, jnp.abs(diff), diff)
    dmat = jnp.where((cm <= cn)[None], jnp.exp(log_g[:, None, None] * expo[None]), 0.0)
    qdec = jnp.exp(log_g[:, None] * (n + 1.0)[None])[:, :, None]
    kdec = jnp.exp(log_g[:, None] * (tb - 1.0 - n)[None])[:, :, None]
    cdec = jnp.broadcast_to(jnp.exp(log_g * tb)[:, None, None], (RET_HEADS, 1, RET_DV))
    half = RET_DK // 2
    inv_freq = 1.0 / (ROPE_BASE ** (jnp.arange(half, dtype=jnp.float32) / half))
    ang = jnp.arange(seq, dtype=jnp.float32)[:, None] * inv_freq[None, :]
    cos, sin = jnp.cos(ang), jnp.sin(ang)

    tile = lambda b, s: (b * n_s + s, 0)
    return pl.pallas_call(
        _ret_kernel,
        grid=(batch, n_s),
        in_specs=[
            pl.BlockSpec((tb, D_MODEL), tile),
            _const_spec((1, D_MODEL)),
            _const_spec(w_in.shape),
            _const_spec(w_out.shape),
            pl.BlockSpec((tb, half), lambda b, s: (s, 0)),
            pl.BlockSpec((tb, half), lambda b, s: (s, 0)),
            _const_spec(dmat.shape),
            _const_spec(qdec.shape),
            _const_spec(kdec.shape),
            _const_spec(cdec.shape),
            _const_spec((1, D_MODEL)),
            _const_spec(wr.shape),
            _const_spec(br.shape),
        ],
        out_specs=[
            pl.BlockSpec((tb * TILES_PER_ROW, LANES), tile),
            pl.BlockSpec((SUBLANES, tb), lambda b, s: (0, b * n_s + s)),
        ],
        out_shape=[
            jax.ShapeDtypeStruct((total * TILES_PER_ROW, LANES), jnp.float32),
            jax.ShapeDtypeStruct((SUBLANES, total), jnp.float32),
        ],
        scratch_shapes=[
            pltpu.VMEM((RET_HEADS, RET_DK, RET_DV), jnp.float32),
            pltpu.VMEM((tb, RET_V_DIM), jnp.bfloat16),
        ],
        compiler_params=pltpu.CompilerParams(
            dimension_semantics=("arbitrary", "arbitrary"),
            vmem_limit_bytes=VMEM_LIMIT),
        name="retention_layer",
    )(x2d, gain, w_in, w_out, cos, sin, dmat, qdec, kdec, cdec, gain2, wr, br)


def _pool_kernel(x_ref, gain_ref, pw_ref, scale_ref, gain2_ref, wr_ref, br_ref,
                 x1_ref, route_ref, ext_ref, y_ref):
    s = pl.program_id(1)
    tb = x_ref.shape[0] // TILES_PER_ROW

    @pl.when(s == 0)
    def _():
        ext_ref[0:POOL_HIST, :] = jnp.zeros((POOL_HIST, D_MODEL), jnp.float32)

    x = _load_rows(x_ref, tb)
    ext_ref[POOL_HIST:, :] = _rms(x, gain_ref[...])
    pos = s * tb + lax.broadcasted_iota(jnp.int32, (tb, 1), 0)
    for g, w in enumerate(POOL_WINDOWS):
        cs = slice(g * POOL_GROUP_DIM, (g + 1) * POOL_GROUP_DIM)
        e = ext_ref[:, cs]
        acc = e
        k = 1
        while k < w:
            acc = acc + pltpu.roll(acc, k, axis=0)
            k *= 2
        inv_cnt = 1.0 / jnp.minimum(pos + 1, w).astype(jnp.float32)
        pooled = acc[POOL_HIST:, :] * inv_cnt - e[POOL_HIST:, :]
        y_ref[:, cs] = _dot(pooled.astype(jnp.bfloat16), pw_ref[g])
    ext_ref[0:POOL_HIST, :] = ext_ref[tb:tb + POOL_HIST, :]
    x1 = x + y_ref[...] * scale_ref[...]
    _store_rows(x1_ref, x1)
    route_ref[...] = _route(x1, gain2_ref[...], wr_ref[...], br_ref[...])


def _pool_layer(x2d, batch, seq, gain, pw, scale, gain2, wr, br):
    tb = POOL_TILE
    n_s = seq // tb
    total = batch * seq
    tile = lambda b, s: (b * n_s + s, 0)
    return pl.pallas_call(
        _pool_kernel,
        grid=(batch, n_s),
        in_specs=[
            pl.BlockSpec((tb * TILES_PER_ROW, LANES), tile),
            _const_spec((1, D_MODEL)),
            _const_spec(pw.shape),
            _const_spec((1, D_MODEL)),
            _const_spec((1, D_MODEL)),
            _const_spec(wr.shape),
            _const_spec(br.shape),
        ],
        out_specs=[
            pl.BlockSpec((tb * TILES_PER_ROW, LANES), tile),
            pl.BlockSpec((SUBLANES, tb), lambda b, s: (0, b * n_s + s)),
        ],
        out_shape=[
            jax.ShapeDtypeStruct((total * TILES_PER_ROW, LANES), jnp.float32),
            jax.ShapeDtypeStruct((SUBLANES, total), jnp.float32),
        ],
        scratch_shapes=[
            pltpu.VMEM((POOL_HIST + tb, D_MODEL), jnp.float32),
            pltpu.VMEM((tb, D_MODEL), jnp.float32),
        ],
        compiler_params=pltpu.CompilerParams(
            dimension_semantics=("arbitrary", "arbitrary"),
            vmem_limit_bytes=VMEM_LIMIT),
        name="pool_layer",
    )(x2d, gain, pw, scale, gain2, wr, br)


def _moe_kernel(src_ref, dst_ref, nv_ref, ea_ref, eb_ref,
                x_hbm, wts_ref, gain_ref, fgain_ref, wgu_a_ref, wd_a_ref, wgu_b_ref, wd_b_ref,
                out_hbm, xbuf0, xbuf1, obuf0, obuf1, sem_in, sem_out, *, final_norm):
    b = pl.program_id(0)
    n = nv_ref[b]
    n_prev = nv_ref[jnp.maximum(b - 1, 0)]

    def tile_of(row):
        return pl.ds(pl.multiple_of(row * TILES_PER_ROW, TILES_PER_ROW), TILES_PER_ROW)

    def start_gather(blk, xdst, sem):
        for r in range(MOE_ROWS):
            tok = src_ref[blk * MOE_ROWS + r]
            pltpu.make_async_copy(x_hbm.at[tile_of(tok)], xdst.at[tile_of(r)], sem).start()

    def wait_gather(xdst, sem):
        pltpu.make_async_copy(x_hbm.at[pl.ds(0, MOE_ROWS * TILES_PER_ROW)], xdst, sem).wait()

    def start_scatter(blk, osrc):
        for r in range(MOE_ROWS):
            tok = dst_ref[blk * MOE_ROWS + r]
            if final_norm:
                cp = pltpu.make_async_copy(osrc.at[pl.ds(r, 1)], out_hbm.at[pl.ds(tok, 1)],
                                           sem_out.at[0])
            else:
                cp = pltpu.make_async_copy(osrc.at[tile_of(r)], out_hbm.at[tile_of(tok)],
                                           sem_out.at[0])
            cp.start()

    def wait_scatter(osrc):
        pltpu.make_async_copy(osrc, out_hbm.at[pl.ds(0, osrc.shape[0])], sem_out.at[0]).wait()

    @pl.when(b == 0)
    def _():
        obuf1[...] = jnp.zeros(obuf1.shape, jnp.float32)
        start_gather(0, xbuf0, sem_in.at[0])

    def step(xcur, xnext, ocur, oprev, sem_cur, sem_next):
        @pl.when(n > 0)
        def _():
            wait_gather(xcur, sem_cur)
            start_gather(b + 1, xnext, sem_next)
            start_scatter(b, oprev)
            xb = _load_rows(xcur, MOE_ROWS)
            hb = _rms(xb, gain_ref[...]).astype(jnp.bfloat16)
            wt = wts_ref[0].T

            def expert(wgu_ref, wd_ref):
                gu = _dot(hb, wgu_ref[0])
                hid = _silu(gu[:, :MOE_D_FF]) * gu[:, MOE_D_FF:]
                return _dot(hid.astype(jnp.bfloat16), wd_ref[0])

            o = (xb + wt[:, 0:1] * expert(wgu_a_ref, wd_a_ref)
                 + wt[:, 1:2] * expert(wgu_b_ref, wd_b_ref))
            if final_norm:
                ocur[...] = _rms(o, fgain_ref[...])
            else:
                _store_rows(ocur, o)
            wait_scatter(oprev)

        @pl.when(jnp.logical_and(n == 0, n_prev > 0))
        def _():
            wait_gather(xcur, sem_cur)
            start_scatter(b, oprev)
            wait_scatter(oprev)

    @pl.when(b % 2 == 0)
    def _():
        step(xbuf0, xbuf1, obuf0, obuf1, sem_in.at[0], sem_in.at[1])

    @pl.when(b % 2 == 1)
    def _():
        step(xbuf1, xbuf0, obuf1, obuf0, sem_in.at[1], sem_in.at[0])


def _class_tables():
    ea = np.zeros((N_CLASSES,), np.int32)
    eb = np.zeros((N_CLASSES,), np.int32)
    for g in range(MOE_GROUPS):
        c = g * PAIRS_PER_GROUP
        for a in range(MOE_EPG):
            for b in range(a + 1, MOE_EPG):
                ea[c], eb[c] = g * MOE_EPG + a, g * MOE_EPG + b
                c += 1
    return ea, eb


def _moe_layer(x_tiled, route, gain, fgain, wgu, wd, final_norm):
    total = route.shape[1]
    nb = total // MOE_ROWS + N_CLASSES + 1
    lo = route[0].astype(jnp.int32)
    hi = route[1].astype(jnp.int32)
    w_lo, w_hi = route[2], route[3]
    a = lo % MOE_EPG
    bb = hi % MOE_EPG
    cls = (lo // MOE_EPG) * PAIRS_PER_GROUP + (a * (2 * MOE_EPG - 1 - a)) // 2 + (bb - a - 1)
    order = jnp.argsort(cls, stable=True).astype(jnp.int32)
    counts = jnp.sum((cls[:, None] == jnp.arange(N_CLASSES, dtype=jnp.int32)[None, :]).astype(jnp.int32),
                     axis=0)
    starts = jnp.cumsum(counts) - counts
    nblk = (counts + MOE_ROWS - 1) // MOE_ROWS
    blk_end = jnp.cumsum(nblk)
    blk_first = blk_end - nblk
    bidx = jnp.arange(nb, dtype=jnp.int32)
    used = bidx < blk_end[-1]
    last_cls = jnp.max(jnp.where(counts > 0, jnp.arange(N_CLASSES, dtype=jnp.int32), 0))
    c = jnp.where(used, jnp.searchsorted(blk_end, bidx, side='right').astype(jnp.int32), last_cls)
    c = jnp.minimum(c, N_CLASSES - 1)
    j = bidx - blk_first[c]
    base = starts[c] + j * MOE_ROWS
    nvalid = jnp.where(used, jnp.clip(counts[c] - j * MOE_ROWS, 0, MOE_ROWS), 0).astype(jnp.int32)
    r = jnp.arange(MOE_ROWS, dtype=jnp.int32)
    valid = r[None, :] < nvalid[:, None]
    src = jnp.clip(base[:, None] + r[None, :], 0, total - 1)
    tok = jnp.where(valid, order[src], 0)
    dump = jnp.broadcast_to(total + r[None, :], (nb, MOE_ROWS))
    dst = jnp.where(valid, tok, dump)
    dst = jnp.concatenate([dump[:1], dst[:-1]], axis=0)
    wts = jnp.zeros((nb, SUBLANES, MOE_ROWS), jnp.float32)
    wts = wts.at[:, 0, :].set(jnp.where(valid, w_lo[tok], 0.0))
    wts = wts.at[:, 1, :].set(jnp.where(valid, w_hi[tok], 0.0))
    ea_tab, eb_tab = _class_tables()
    ea = jnp.asarray(ea_tab)[c]
    eb = jnp.asarray(eb_tab)[c]

    tiled_block = (MOE_ROWS * TILES_PER_ROW, LANES)
    if final_norm:
        out_block = (MOE_ROWS, D_MODEL)
        out_shape = (total + MOE_ROWS, D_MODEL)
    else:
        out_block = tiled_block
        out_shape = ((total + MOE_ROWS) * TILES_PER_ROW, LANES)

    grid_spec = pltpu.PrefetchScalarGridSpec(
        num_scalar_prefetch=5,
        grid=(nb,),
        in_specs=[
            pl.BlockSpec(memory_space=pl.ANY),
            pl.BlockSpec((1, SUBLANES, MOE_ROWS), lambda b, *_: (b, 0, 0)),
            pl.BlockSpec((1, D_MODEL), lambda b, *_: (0, 0)),
            pl.BlockSpec((1, D_MODEL), lambda b, *_: (0, 0)),
            pl.BlockSpec((1, D_MODEL, 2 * MOE_D_FF), lambda b, src, dst, nv, ea, eb: (ea[b], 0, 0)),
            pl.BlockSpec((1, MOE_D_FF, D_MODEL), lambda b, src, dst, nv, ea, eb: (ea[b], 0, 0)),
            pl.BlockSpec((1, D_MODEL, 2 * MOE_D_FF), lambda b, src, dst, nv, ea, eb: (eb[b], 0, 0)),
            pl.BlockSpec((1, MOE_D_FF, D_MODEL), lambda b, src, dst, nv, ea, eb: (eb[b], 0, 0)),
        ],
        out_specs=pl.BlockSpec(memory_space=pl.ANY),
        scratch_shapes=[
            pltpu.VMEM(tiled_block, jnp.float32),
            pltpu.VMEM(tiled_block, jnp.float32),
            pltpu.VMEM(out_block, jnp.float32),
            pltpu.VMEM(out_block, jnp.float32),
            pltpu.SemaphoreType.DMA((2,)),
            pltpu.SemaphoreType.DMA((1,)),
        ],
    )
    return pl.pallas_call(
        functools.partial(_moe_kernel, final_norm=final_norm),
        grid_spec=grid_spec,
        out_shape=jax.ShapeDtypeStruct(out_shape, jnp.float32),
        compiler_params=pltpu.CompilerParams(
            dimension_semantics=("arbitrary",),
            vmem_limit_bytes=VMEM_LIMIT),
        name="moe_layer",
    )(tok.reshape(-1), dst.reshape(-1), nvalid, ea, eb, x_tiled, wts, gain, fgain, wgu, wd, wgu, wd)


def _router_params(w_group, b_group, w_expert, b_expert):
    pad = LANES - MOE_GROUPS - MOE_EXPERTS
    wr = jnp.concatenate([w_group, w_expert, jnp.zeros((D_MODEL, pad), jnp.float32)], axis=1)
    br = jnp.concatenate([b_group, b_expert.reshape(-1), jnp.zeros((pad,), jnp.float32)])[None, :]
    return wr, br


def kernel(x, norm_mix, norm_ffn, ret_w_in, ret_w_out, pool_w, pool_scale, moe_w_group, moe_b_group, moe_w_expert, moe_b_expert, moe_w_gate, moe_w_up, moe_w_down, final_norm):
    batch, seq, _ = x.shape
    x2d = x.reshape(batch * seq, D_MODEL)
    bf = jnp.bfloat16
    wgu = jnp.concatenate([moe_w_gate, moe_w_up], axis=-1).astype(bf)
    wd = moe_w_down.astype(bf)
    fgain = final_norm[None, :]

    wr0, br0 = _router_params(moe_w_group[0], moe_b_group[0], moe_w_expert[0], moe_b_expert[0])
    x1, route0 = _retention_layer(x2d, batch, seq, norm_mix[0][None, :], ret_w_in[0].astype(bf),
                                  ret_w_out[0].astype(bf), norm_ffn[0][None, :], wr0, br0)
    x2 = _moe_layer(x1, route0, norm_ffn[0][None, :], fgain, wgu[0], wd[0], final_norm=False)

    wr1, br1 = _router_params(moe_w_group[1], moe_b_group[1], moe_w_expert[1], moe_b_expert[1])
    x3, route1 = _pool_layer(x2, batch, seq, norm_mix[1][None, :], pool_w[0].astype(bf),
                             pool_scale[0][None, :], norm_ffn[1][None, :], wr1, br1)
    out = _moe_layer(x3, route1, norm_ffn[1][None, :], fgain, wgu[1], wd[1], final_norm=True)
    return out[:batch * seq].reshape(batch, seq, D_MODEL)
```

```python
import functools

import numpy as np
import jax
import jax.numpy as jnp
from jax import lax
from jax.experimental import pallas as pl
from jax.experimental.pallas import tpu as pltpu

D_MODEL = 1024
EPS = 1e-6
CHUNK = 64

RET_HEADS = 4
RET_DK = 256
RET_DV = 512
RET_QK_DIM = RET_HEADS * RET_DK
RET_V_DIM = RET_HEADS * RET_DV
ROPE_BASE = 10000.0

POOL_WINDOWS = (2, 4, 8, 16)
POOL_GROUP_DIM = 256
POOL_HIST = 16

MOE_GROUPS = 4
MOE_EPG = 8
MOE_EXPERTS = MOE_GROUPS * MOE_EPG
MOE_D_FF = 256
PAIRS_PER_GROUP = MOE_EPG * (MOE_EPG - 1) // 2
N_CLASSES = MOE_GROUPS * PAIRS_PER_GROUP

LANES = 128
SUBLANES = 8

RET_TILE = 256
POOL_TILE = 512
MOE_ROWS = 128
DMA_THREADS = 2
VMEM_LIMIT = 56 * 1024 * 1024


def _rms(x, gain):
    ms = jnp.mean(x * x, axis=-1, keepdims=True)
    return x * lax.rsqrt(ms + EPS) * gain


def _silu(x):
    return x * jax.nn.sigmoid(x)


def _dot(a, b):
    return jnp.dot(a, b, preferred_element_type=jnp.float32)


TILES_PER_ROW = D_MODEL // LANES


def _load_rows(ref, rows):
    return jnp.concatenate(
        [ref[pl.ds(s, rows, stride=TILES_PER_ROW), :] for s in range(TILES_PER_ROW)], axis=1)


def _store_rows(ref, val):
    rows = val.shape[0]
    for s in range(TILES_PER_ROW):
        ref[pl.ds(s, rows, stride=TILES_PER_ROW), :] = val[:, s * LANES:(s + 1) * LANES]


def _route(x1, gain2, wr, br):
    rows = x1.shape[0]
    h2 = _rms(x1, gain2)
    logits = jnp.dot(h2, wr, preferred_element_type=jnp.float32,
                     precision=lax.Precision.HIGHEST) + br
    lane = lax.broadcasted_iota(jnp.int32, (rows, LANES), 1)
    neg = jnp.float32(-jnp.inf)
    gmask = lane < MOE_GROUPS
    gl = jnp.where(gmask, logits, neg)
    gmax = jnp.max(gl, axis=1, keepdims=True)
    grp = jnp.min(jnp.where(gl == gmax, lane, LANES), axis=1, keepdims=True)
    gsum = jnp.sum(jnp.where(gmask, jnp.exp(gl - gmax), 0.0), axis=1, keepdims=True)
    g_gate = 1.0 / gsum
    first = MOE_GROUPS + MOE_EPG * grp
    emask = (lane >= first) & (lane < first + MOE_EPG)
    el = jnp.where(emask, logits, neg)
    m1 = jnp.max(el, axis=1, keepdims=True)
    i1 = jnp.min(jnp.where(el == m1, lane, LANES), axis=1, keepdims=True)
    el2 = jnp.where(lane == i1, neg, el)
    m2 = jnp.max(el2, axis=1, keepdims=True)
    i2 = jnp.min(jnp.where(el2 == m2, lane, LANES), axis=1, keepdims=True)
    t = jnp.exp(m2 - m1)
    w1 = g_gate / (1.0 + t)
    w2 = g_gate * t / (1.0 + t)
    first_lo = i1 < i2
    lo = jnp.where(first_lo, i1, i2) - MOE_GROUPS
    hi = jnp.where(first_lo, i2, i1) - MOE_GROUPS
    w_lo = jnp.where(first_lo, w1, w2)
    w_hi = jnp.where(first_lo, w2, w1)
    packed = jnp.where(lane == 0, lo.astype(jnp.float32),
             jnp.where(lane == 1, hi.astype(jnp.float32),
             jnp.where(lane == 2, w_lo,
             jnp.where(lane == 3, w_hi, 0.0))))
    return packed.T[:SUBLANES, :]


def _ret_kernel(x_ref, gain_ref, win_ref, wout_ref, cos_ref, sin_ref, dmat_ref,
                qdec_ref, kdec_ref, cdec_ref, gain2_ref, wr_ref, br_ref,
                x1_ref, route_ref, state_ref, gated_ref):
    @pl.when(pl.program_id(1) == 0)
    def _():
        state_ref[...] = jnp.zeros_like(state_ref)

    x = x_ref[...]
    h = _rms(x, gain_ref[...]).astype(jnp.bfloat16)
    cos = cos_ref[...]
    sin = sin_ref[...]
    half = RET_DK // 2

    def rope(t):
        t1, t2 = t[:, :half], t[:, half:]
        return jnp.concatenate([t1 * cos - t2 * sin, t1 * sin + t2 * cos], axis=1)

    for hd in range(RET_HEADS):
        q = _dot(h, win_ref[:, hd * RET_DK:(hd + 1) * RET_DK])
        k = _dot(h, win_ref[:, RET_QK_DIM + hd * RET_DK:RET_QK_DIM + (hd + 1) * RET_DK])
        v0 = 2 * RET_QK_DIM + hd * RET_DV
        v = _dot(h, win_ref[:, v0:v0 + RET_DV])
        g0 = 2 * RET_QK_DIM + RET_V_DIM + hd * RET_DV
        g = _dot(h, win_ref[:, g0:g0 + RET_DV])
        q = rope(q)
        k = rope(k) * (RET_DK ** -0.5)
        qb = q.astype(jnp.bfloat16)
        kb = k.astype(jnp.bfloat16)
        vb = v.astype(jnp.bfloat16)
        sc = lax.dot_general(qb, kb, (((1,), (1,)), ((), ())),
                             preferred_element_type=jnp.float32)
        sc = sc * dmat_ref[hd]
        intra = _dot(sc.astype(jnp.bfloat16), vb)
        st = state_ref[hd]
        cross = _dot(qb, st.astype(jnp.bfloat16)) * qdec_ref[hd]
        o = intra + cross
        kd = (k * kdec_ref[hd]).astype(jnp.bfloat16)
        upd = lax.dot_general(kd, vb, (((0,), (0,)), ((), ())),
                              preferred_element_type=jnp.float32)
        state_ref[hd] = st * cdec_ref[hd] + upd
        mu = jnp.mean(o, axis=-1, keepdims=True)
        oc = o - mu
        var = jnp.mean(oc * oc, axis=-1, keepdims=True)
        on = oc * lax.rsqrt(var + EPS)
        gated_ref[:, hd * RET_DV:(hd + 1) * RET_DV] = (_silu(g) * on).astype(jnp.bfloat16)

    x1 = x + _dot(gated_ref[...], wout_ref[...])
    _store_rows(x1_ref, x1)
    route_ref[...] = _route(x1, gain2_ref[...], wr_ref[...], br_ref[...])


def _const_spec(shape):
    nd = len(shape)
    return pl.BlockSpec(shape, lambda *_: (0,) * nd, pipeline_mode=pl.Buffered(1))


def _retention_layer(x2d, batch, seq, gain, w_in, w_out, gain2, wr, br):
    tb = RET_TILE
    n_s = seq // tb
    total = batch * seq
    log_g = jnp.log(1.0 - jnp.exp2(-5.0 - jnp.arange(RET_HEADS, dtype=jnp.float32)))
    n = jnp.arange(tb, dtype=jnp.float32)
    diff = n[:, None] - n[None, :]
    cn = jnp.arange(tb)[:, None] // CHUNK
    cm = jnp.arange(tb)[None, :] // CHUNK
    expo = jnp.where(cn == cm, jnp.abs(diff), diff)
    dmat = jnp.where((cm <= cn)[None], jnp.exp(log_g[:, None, None] * expo[None]), 0.0)
    qdec = jnp.exp(log_g[:, None] * (n + 1.0)[None])[:, :, None]
    kdec = jnp.exp(log_g[:, None] * (tb - 1.0 - n)[None])[:, :, None]
    cdec = jnp.broadcast_to(jnp.exp(log_g * tb)[:, None, None], (RET_HEADS, 1, RET_DV))
    half = RET_DK // 2
    inv_freq = 1.0 / (ROPE_BASE ** (jnp.arange(half, dtype=jnp.float32) / half))
    ang = jnp.arange(seq, dtype=jnp.float32)[:, None] * inv_freq[None, :]
    cos, sin = jnp.cos(ang), jnp.sin(ang)

    tile = lambda b, s: (b * n_s + s, 0)
    return pl.pallas_call(
        _ret_kernel,
        grid=(batch, n_s),
        in_specs=[
            pl.BlockSpec((tb, D_MODEL), tile),
            _const_spec((1, D_MODEL)),
            _const_spec(w_in.shape),
            _const_spec(w_out.shape),
            pl.BlockSpec((tb, half), lambda b, s: (s, 0)),
            pl.BlockSpec((tb, half), lambda b, s: (s, 0)),
            _const_spec(dmat.shape),
            _const_spec(qdec.shape),
            _const_spec(kdec.shape),
            _const_spec(cdec.shape),
            _const_spec((1, D_MODEL)),
            _const_spec(wr.shape),
            _const_spec(br.shape),
        ],
        out_specs=[
            pl.BlockSpec((tb * TILES_PER_ROW, LANES), tile),
            pl.BlockSpec((SUBLANES, tb), lambda b, s: (0, b * n_s + s)),
        ],
        out_shape=[
            jax.ShapeDtypeStruct((total * TILES_PER_ROW, LANES), jnp.float32),
            jax.ShapeDtypeStruct((SUBLANES, total), jnp.float32),
        ],
        scratch_shapes=[
            pltpu.VMEM((RET_HEADS, RET_DK, RET_DV), jnp.float32),
            pltpu.VMEM((tb, RET_V_DIM), jnp.bfloat16),
        ],
        compiler_params=pltpu.CompilerParams(
            dimension_semantics=("arbitrary", "arbitrary"),
            vmem_limit_bytes=VMEM_LIMIT),
        name="retention_layer",
    )(x2d, gain, w_in, w_out, cos, sin, dmat, qdec, kdec, cdec, gain2, wr, br)


def _pool_kernel(x_ref, gain_ref, pw_ref, scale_ref, gain2_ref, wr_ref, br_ref,
                 x1_ref, route_ref, ext_ref, y_ref):
    s = pl.program_id(1)
    tb = x_ref.shape[0] // TILES_PER_ROW

    @pl.when(s == 0)
    def _():
        ext_ref[0:POOL_HIST, :] = jnp.zeros((POOL_HIST, D_MODEL), jnp.float32)

    x = _load_rows(x_ref, tb)
    ext_ref[POOL_HIST:, :] = _rms(x, gain_ref[...])
    pos = s * tb + lax.broadcasted_iota(jnp.int32, (tb, 1), 0)
    for g, w in enumerate(POOL_WINDOWS):
        cs = slice(g * POOL_GROUP_DIM, (g + 1) * POOL_GROUP_DIM)
        e = ext_ref[:, cs]
        acc = e
        k = 1
        while k < w:
            acc = acc + pltpu.roll(acc, k, axis=0)
            k *= 2
        inv_cnt = 1.0 / jnp.minimum(pos + 1, w).astype(jnp.float32)
        pooled = acc[POOL_HIST:, :] * inv_cnt - e[POOL_HIST:, :]
        y_ref[:, cs] = _dot(pooled.astype(jnp.bfloat16), pw_ref[g])
    ext_ref[0:POOL_HIST, :] = ext_ref[tb:tb + POOL_HIST, :]
    x1 = x + y_ref[...] * scale_ref[...]
    _store_rows(x1_ref, x1)
    route_ref[...] = _route(x1, gain2_ref[...], wr_ref[...], br_ref[...])


def _pool_layer(x2d, batch, seq, gain, pw, scale, gain2, wr, br):
    tb = POOL_TILE
    n_s = seq // tb
    total = batch * seq
    tile = lambda b, s: (b * n_s + s, 0)
    return pl.pallas_call(
        _pool_kernel,
        grid=(batch, n_s),
        in_specs=[
            pl.BlockSpec((tb * TILES_PER_ROW, LANES), tile),
            _const_spec((1, D_MODEL)),
            _const_spec(pw.shape),
            _const_spec((1, D_MODEL)),
            _const_spec((1, D_MODEL)),
            _const_spec(wr.shape),
            _const_spec(br.shape),
        ],
        out_specs=[
            pl.BlockSpec((tb * TILES_PER_ROW, LANES), tile),
            pl.BlockSpec((SUBLANES, tb), lambda b, s: (0, b * n_s + s)),
        ],
        out_shape=[
            jax.ShapeDtypeStruct((total * TILES_PER_ROW, LANES), jnp.float32),
            jax.ShapeDtypeStruct((SUBLANES, total), jnp.float32),
        ],
        scratch_shapes=[
            pltpu.VMEM((POOL_HIST + tb, D_MODEL), jnp.float32),
            pltpu.VMEM((tb, D_MODEL), jnp.float32),
        ],
        compiler_params=pltpu.CompilerParams(
            dimension_semantics=("arbitrary", "arbitrary"),
            vmem_limit_bytes=VMEM_LIMIT),
        name="pool_layer",
    )(x2d, gain, pw, scale, gain2, wr, br)


def _moe_kernel(src_ref, dst_ref, nv_ref, ea_ref, eb_ref,
                x_hbm, wts_ref, gain_ref, fgain_ref, wgu_a_ref, wd_a_ref, wgu_b_ref, wd_b_ref,
                out_hbm, xbuf0, xbuf1, obuf0, obuf1, sem_in, sem_out, *, final_norm):
    b = pl.program_id(0)
    n = nv_ref[b]
    n_prev = nv_ref[jnp.maximum(b - 1, 0)]

    def tile_of(row):
        return pl.ds(pl.multiple_of(row * TILES_PER_ROW, TILES_PER_ROW), TILES_PER_ROW)

    def start_gather(blk, xdst, sem):
        for r in range(MOE_ROWS):
            tok = src_ref[blk * MOE_ROWS + r]
            pltpu.make_async_copy(x_hbm.at[tile_of(tok)], xdst.at[tile_of(r)], sem).start(
                priority=r % DMA_THREADS)

    def wait_gather(xdst, sem):
        pltpu.make_async_copy(x_hbm.at[pl.ds(0, MOE_ROWS * TILES_PER_ROW)], xdst, sem).wait()

    def start_scatter(blk, osrc):
        for r in range(MOE_ROWS):
            tok = dst_ref[blk * MOE_ROWS + r]
            if final_norm:
                cp = pltpu.make_async_copy(osrc.at[pl.ds(r, 1)], out_hbm.at[pl.ds(tok, 1)],
                                           sem_out.at[0])
            else:
                cp = pltpu.make_async_copy(osrc.at[tile_of(r)], out_hbm.at[tile_of(tok)],
                                           sem_out.at[0])
            cp.start(priority=r % DMA_THREADS)

    def wait_scatter(osrc):
        pltpu.make_async_copy(osrc, out_hbm.at[pl.ds(0, osrc.shape[0])], sem_out.at[0]).wait()

    @pl.when(b == 0)
    def _():
        obuf1[...] = jnp.zeros(obuf1.shape, jnp.float32)
        start_gather(0, xbuf0, sem_in.at[0])

    def step(xcur, xnext, ocur, oprev, sem_cur, sem_next):
        @pl.when(n > 0)
        def _():
            wait_gather(xcur, sem_cur)
            start_gather(b + 1, xnext, sem_next)
            start_scatter(b, oprev)
            xb = _load_rows(xcur, MOE_ROWS)
            hb = _rms(xb, gain_ref[...]).astype(jnp.bfloat16)
            wt = wts_ref[0].T

            def expert(wgu_ref, wd_ref):
                gu = _dot(hb, wgu_ref[0])
                hid = _silu(gu[:, :MOE_D_FF]) * gu[:, MOE_D_FF:]
                return _dot(hid.astype(jnp.bfloat16), wd_ref[0])

            o = (xb + wt[:, 0:1] * expert(wgu_a_ref, wd_a_ref)
                 + wt[:, 1:2] * expert(wgu_b_ref, wd_b_ref))
            if final_norm:
                ocur[...] = _rms(o, fgain_ref[...])
            else:
                _store_rows(ocur, o)
            wait_scatter(oprev)

        @pl.when(jnp.logical_and(n == 0, n_prev > 0))
        def _():
            wait_gather(xcur, sem_cur)
            start_scatter(b, oprev)
            wait_scatter(oprev)

    @pl.when(b % 2 == 0)
    def _():
        step(xbuf0, xbuf1, obuf0, obuf1, sem_in.at[0], sem_in.at[1])

    @pl.when(b % 2 == 1)
    def _():
        step(xbuf1, xbuf0, obuf1, obuf0, sem_in.at[1], sem_in.at[0])


def _class_tables():
    ea = np.zeros((N_CLASSES,), np.int32)
    eb = np.zeros((N_CLASSES,), np.int32)
    for g in range(MOE_GROUPS):
        c = g * PAIRS_PER_GROUP
        for a in range(MOE_EPG):
            for b in range(a + 1, MOE_EPG):
                ea[c], eb[c] = g * MOE_EPG + a, g * MOE_EPG + b
                c += 1
    return ea, eb


def _moe_layer(x_tiled, route, gain, fgain, wgu, wd, final_norm):
    total = route.shape[1]
    nb = total // MOE_ROWS + N_CLASSES + 1
    lo = route[0].astype(jnp.int32)
    hi = route[1].astype(jnp.int32)
    w_lo, w_hi = route[2], route[3]
    a = lo % MOE_EPG
    bb = hi % MOE_EPG
    cls = (lo // MOE_EPG) * PAIRS_PER_GROUP + (a * (2 * MOE_EPG - 1 - a)) // 2 + (bb - a - 1)
    order = jnp.argsort(cls, stable=True).astype(jnp.int32)
    counts = jnp.sum((cls[:, None] == jnp.arange(N_CLASSES, dtype=jnp.int32)[None, :]).astype(jnp.int32),
                     axis=0)
    starts = jnp.cumsum(counts) - counts
    nblk = (counts + MOE_ROWS - 1) // MOE_ROWS
    blk_end = jnp.cumsum(nblk)
    blk_first = blk_end - nblk
    bidx = jnp.arange(nb, dtype=jnp.int32)
    used = bidx < blk_end[-1]
    last_cls = jnp.max(jnp.where(counts > 0, jnp.arange(N_CLASSES, dtype=jnp.int32), 0))
    c = jnp.where(used, jnp.searchsorted(blk_end, bidx, side='right').astype(jnp.int32), last_cls)
    c = jnp.minimum(c, N_CLASSES - 1)
    j = bidx - blk_first[c]
    base = starts[c] + j * MOE_ROWS
    nvalid = jnp.where(used, jnp.clip(counts[c] - j * MOE_ROWS, 0, MOE_ROWS), 0).astype(jnp.int32)
    r = jnp.arange(MOE_ROWS, dtype=jnp.int32)
    valid = r[None, :] < nvalid[:, None]
    src = jnp.clip(base[:, None] + r[None, :], 0, total - 1)
    tok = jnp.where(valid, order[src], 0)
    dump = jnp.broadcast_to(total + r[None, :], (nb, MOE_ROWS))
    dst = jnp.where(valid, tok, dump)
    dst = jnp.concatenate([dump[:1], dst[:-1]], axis=0)
    wts = jnp.zeros((nb, SUBLANES, MOE_ROWS), jnp.float32)
    wts = wts.at[:, 0, :].set(jnp.where(valid, w_lo[tok], 0.0))
    wts = wts.at[:, 1, :].set(jnp.where(valid, w_hi[tok], 0.0))
    ea_tab, eb_tab = _class_tables()
    ea = jnp.asarray(ea_tab)[c]
    eb = jnp.asarray(eb_tab)[c]

    tiled_block = (MOE_ROWS * TILES_PER_ROW, LANES)
    if final_norm:
        out_block = (MOE_ROWS, D_MODEL)
        out_shape = (total + MOE_ROWS, D_MODEL)
    else:
        out_block = tiled_block
        out_shape = ((total + MOE_ROWS) * TILES_PER_ROW, LANES)

    grid_spec = pltpu.PrefetchScalarGridSpec(
        num_scalar_prefetch=5,
        grid=(nb,),
        in_specs=[
            pl.BlockSpec(memory_space=pl.ANY),
            pl.BlockSpec((1, SUBLANES, MOE_ROWS), lambda b, *_: (b, 0, 0)),
            pl.BlockSpec((1, D_MODEL), lambda b, *_: (0, 0)),
            pl.BlockSpec((1, D_MODEL), lambda b, *_: (0, 0)),
            pl.BlockSpec((1, D_MODEL, 2 * MOE_D_FF), lambda b, src, dst, nv, ea, eb: (ea[b], 0, 0)),
            pl.BlockSpec((1, MOE_D_FF, D_MODEL), lambda b, src, dst, nv, ea, eb: (ea[b], 0, 0)),
            pl.BlockSpec((1, D_MODEL, 2 * MOE_D_FF), lambda b, src, dst, nv, ea, eb: (eb[b], 0, 0)),
            pl.BlockSpec((1, MOE_D_FF, D_MODEL), lambda b, src, dst, nv, ea, eb: (eb[b], 0, 0)),
        ],
        out_specs=pl.BlockSpec(memory_space=pl.ANY),
        scratch_shapes=[
            pltpu.VMEM(tiled_block, jnp.float32),
            pltpu.VMEM(tiled_block, jnp.float32),
            pltpu.VMEM(out_block, jnp.float32),
            pltpu.VMEM(out_block, jnp.float32),
            pltpu.SemaphoreType.DMA((2,)),
            pltpu.SemaphoreType.DMA((1,)),
        ],
    )
    return pl.pallas_call(
        functools.partial(_moe_kernel, final_norm=final_norm),
        grid_spec=grid_spec,
        out_shape=jax.ShapeDtypeStruct(out_shape, jnp.float32),
        compiler_params=pltpu.CompilerParams(
            dimension_semantics=("arbitrary",),
            vmem_limit_bytes=VMEM_LIMIT),
        name="moe_layer",
    )(tok.reshape(-1), dst.reshape(-1), nvalid, ea, eb, x_tiled, wts, gain, fgain, wgu, wd, wgu, wd)


def _router_params(w_group, b_group, w_expert, b_expert):
    pad = LANES - MOE_GROUPS - MOE_EXPERTS
    wr = jnp.concatenate([w_group, w_expert, jnp.zeros((D_MODEL, pad), jnp.float32)], axis=1)
    br = jnp.concatenate([b_group, b_expert.reshape(-1), jnp.zeros((pad,), jnp.float32)])[None, :]
    return wr, br


def kernel(x, norm_mix, norm_ffn, ret_w_in, ret_w_out, pool_w, pool_scale, moe_w_group, moe_b_group, moe_w_expert, moe_b_expert, moe_w_gate, moe_w_up, moe_w_down, final_norm):
    batch, seq, _ = x.shape
    x2d = x.reshape(batch * seq, D_MODEL)
    bf = jnp.bfloat16
    wgu = jnp.concatenate([moe_w_gate, moe_w_up], axis=-1).astype(bf)
    wd = moe_w_down.astype(bf)
    fgain = final_norm[None, :]

    wr0, br0 = _router_params(moe_w_group[0], moe_b_group[0], moe_w_expert[0], moe_b_expert[0])
    x1, route0 = _retention_layer(x2d, batch, seq, norm_mix[0][None, :], ret_w_in[0].astype(bf),
                                  ret_w_out[0].astype(bf), norm_ffn[0][None, :], wr0, br0)
    x2 = _moe_layer(x1, route0, norm_ffn[0][None, :], fgain, wgu[0], wd[0], final_norm=False)

    wr1, br1 = _router_params(moe_w_group[1], moe_b_group[1], moe_w_expert[1], moe_b_expert[1])
    x3, route1 = _pool_layer(x2, batch, seq, norm_mix[1][None, :], pool_w[0].astype(bf),
                             pool_scale[0][None, :], norm_ffn[1][None, :], wr1, br1)
    out = _moe_layer(x3, route1, norm_ffn[1][None, :], fgain, wgu[1], wd[1], final_norm=True)
    return out[:batch * seq].reshape(batch, seq, D_MODEL)
```

```python
import functools

import numpy as np
import jax
import jax.numpy as jnp
from jax import lax
from jax.experimental import pallas as pl
from jax.experimental.pallas import tpu as pltpu

D_MODEL = 1024
EPS = 1e-6
CHUNK = 64

RET_HEADS = 4
RET_DK = 256
RET_DV = 512
RET_QK_DIM = RET_HEADS * RET_DK
RET_V_DIM = RET_HEADS * RET_DV
ROPE_BASE = 10000.0

POOL_WINDOWS = (2, 4, 8, 16)
POOL_GROUP_DIM = 256
POOL_HIST = 16

MOE_GROUPS = 4
MOE_EPG = 8
MOE_EXPERTS = MOE_GROUPS * MOE_EPG
MOE_D_FF = 256
PAIRS_PER_GROUP = MOE_EPG * (MOE_EPG - 1) // 2
N_CLASSES = MOE_GROUPS * PAIRS_PER_GROUP

LANES = 128
SUBLANES = 8

RET_TILE = 256
POOL_TILE = 512
MOE_ROWS = 128
MOE_XBUFS = 3
MOE_OBUFS = 2
DMA_THREADS = 2
VMEM_LIMIT = 56 * 1024 * 1024


def _rms(x, gain):
    ms = jnp.mean(x * x, axis=-1, keepdims=True)
    return x * lax.rsqrt(ms + EPS) * gain


def _silu(x):
    return x * jax.nn.sigmoid(x)


def _dot(a, b):
    return jnp.dot(a, b, preferred_element_type=jnp.float32)


TILES_PER_ROW = D_MODEL // LANES


def _load_rows(ref, rows):
    return jnp.concatenate(
        [ref[pl.ds(s, rows, stride=TILES_PER_ROW), :] for s in range(TILES_PER_ROW)], axis=1)


def _store_rows(ref, val):
    rows = val.shape[0]
    for s in range(TILES_PER_ROW):
        ref[pl.ds(s, rows, stride=TILES_PER_ROW), :] = val[:, s * LANES:(s + 1) * LANES]


def _route(x1, gain2, wr, br):
    rows = x1.shape[0]
    h2 = _rms(x1, gain2)
    logits = jnp.dot(h2, wr, preferred_element_type=jnp.float32,
                     precision=lax.Precision.HIGHEST) + br
    lane = lax.broadcasted_iota(jnp.int32, (rows, LANES), 1)
    neg = jnp.float32(-jnp.inf)
    gmask = lane < MOE_GROUPS
    gl = jnp.where(gmask, logits, neg)
    gmax = jnp.max(gl, axis=1, keepdims=True)
    grp = jnp.min(jnp.where(gl == gmax, lane, LANES), axis=1, keepdims=True)
    gsum = jnp.sum(jnp.where(gmask, jnp.exp(gl - gmax), 0.0), axis=1, keepdims=True)
    g_gate = 1.0 / gsum
    first = MOE_GROUPS + MOE_EPG * grp
    emask = (lane >= first) & (lane < first + MOE_EPG)
    el = jnp.where(emask, logits, neg)
    m1 = jnp.max(el, axis=1, keepdims=True)
    i1 = jnp.min(jnp.where(el == m1, lane, LANES), axis=1, keepdims=True)
    el2 = jnp.where(lane == i1, neg, el)
    m2 = jnp.max(el2, axis=1, keepdims=True)
    i2 = jnp.min(jnp.where(el2 == m2, lane, LANES), axis=1, keepdims=True)
    t = jnp.exp(m2 - m1)
    w1 = g_gate / (1.0 + t)
    w2 = g_gate * t / (1.0 + t)
    first_lo = i1 < i2
    lo = jnp.where(first_lo, i1, i2) - MOE_GROUPS
    hi = jnp.where(first_lo, i2, i1) - MOE_GROUPS
    w_lo = jnp.where(first_lo, w1, w2)
    w_hi = jnp.where(first_lo, w2, w1)
    packed = jnp.where(lane == 0, lo.astype(jnp.float32),
             jnp.where(lane == 1, hi.astype(jnp.float32),
             jnp.where(lane == 2, w_lo,
             jnp.where(lane == 3, w_hi, 0.0))))
    return packed.T[:SUBLANES, :]


def _ret_kernel(x_ref, gain_ref, win_ref, wout_ref, cos_ref, sin_ref, dmat_ref,
                qdec_ref, kdec_ref, cdec_ref, gain2_ref, wr_ref, br_ref,
                x1_ref, route_ref, state_ref, gated_ref):
    @pl.when(pl.program_id(1) == 0)
    def _():
        state_ref[...] = jnp.zeros_like(state_ref)

    x = x_ref[...]
    h = _rms(x, gain_ref[...]).astype(jnp.bfloat16)
    cos = cos_ref[...]
    sin = sin_ref[...]
    half = RET_DK // 2

    def rope(t):
        t1, t2 = t[:, :half], t[:, half:]
        return jnp.concatenate([t1 * cos - t2 * sin, t1 * sin + t2 * cos], axis=1)

    for hd in range(RET_HEADS):
        q = _dot(h, win_ref[:, hd * RET_DK:(hd + 1) * RET_DK])
        k = _dot(h, win_ref[:, RET_QK_DIM + hd * RET_DK:RET_QK_DIM + (hd + 1) * RET_DK])
        v0 = 2 * RET_QK_DIM + hd * RET_DV
        v = _dot(h, win_ref[:, v0:v0 + RET_DV])
        g0 = 2 * RET_QK_DIM + RET_V_DIM + hd * RET_DV
        g = _dot(h, win_ref[:, g0:g0 + RET_DV])
        q = rope(q)
        k = rope(k) * (RET_DK ** -0.5)
        qb = q.astype(jnp.bfloat16)
        kb = k.astype(jnp.bfloat16)
        vb = v.astype(jnp.bfloat16)
        sc = lax.dot_general(qb, kb, (((1,), (1,)), ((), ())),
                             preferred_element_type=jnp.float32)
        sc = sc * dmat_ref[hd]
        intra = _dot(sc.astype(jnp.bfloat16), vb)
        st = state_ref[hd]
        cross = _dot(qb, st.astype(jnp.bfloat16)) * qdec_ref[hd]
        o = intra + cross
        kd = (k * kdec_ref[hd]).astype(jnp.bfloat16)
        upd = lax.dot_general(kd, vb, (((0,), (0,)), ((), ())),
                              preferred_element_type=jnp.float32)
        state_ref[hd] = st * cdec_ref[hd] + upd
        mu = jnp.mean(o, axis=-1, keepdims=True)
        oc = o - mu
        var = jnp.mean(oc * oc, axis=-1, keepdims=True)
        on = oc * lax.rsqrt(var + EPS)
        gated_ref[:, hd * RET_DV:(hd + 1) * RET_DV] = (_silu(g) * on).astype(jnp.bfloat16)

    x1 = x + _dot(gated_ref[...], wout_ref[...])
    _store_rows(x1_ref, x1)
    route_ref[...] = _route(x1, gain2_ref[...], wr_ref[...], br_ref[...])


def _const_spec(shape):
    nd = len(shape)
    return pl.BlockSpec(shape, lambda *_: (0,) * nd, pipeline_mode=pl.Buffered(1))


def _retention_layer(x2d, batch, seq, gain, w_in, w_out, gain2, wr, br):
    tb = RET_TILE
    n_s = seq // tb
    total = batch * seq
    log_g = jnp.log(1.0 - jnp.exp2(-5.0 - jnp.arange(RET_HEADS, dtype=jnp.float32)))
    n = jnp.arange(tb, dtype=jnp.float32)
    diff = n[:, None] - n[None, :]
    cn = jnp.arange(tb)[:, None] // CHUNK
    cm = jnp.arange(tb)[None, :] // CHUNK
    expo = jnp.where(cn == cm, jnp.abs(diff), diff)
    dmat = jnp.where((cm <= cn)[None], jnp.exp(log_g[:, None, None] * expo[None]), 0.0)
    qdec = jnp.exp(log_g[:, None] * (n + 1.0)[None])[:, :, None]
    kdec = jnp.exp(log_g[:, None] * (tb - 1.0 - n)[None])[:, :, None]
    cdec = jnp.broadcast_to(jnp.exp(log_g * tb)[:, None, None], (RET_HEADS, 1, RET_DV))
    half = RET_DK // 2
    inv_freq = 1.0 / (ROPE_BASE ** (jnp.arange(half, dtype=jnp.float32) / half))
    ang = jnp.arange(seq, dtype=jnp.float32)[:, None] * inv_freq[None, :]
    cos, sin = jnp.cos(ang), jnp.sin(ang)

    tile = lambda b, s: (b * n_s + s, 0)
    return pl.pallas_call(
        _ret_kernel,
        grid=(batch, n_s),
        in_specs=[
            pl.BlockSpec((tb, D_MODEL), tile),
            _const_spec((1, D_MODEL)),
            _const_spec(w_in.shape),
            _const_spec(w_out.shape),
            pl.BlockSpec((tb, half), lambda b, s: (s, 0)),
            pl.BlockSpec((tb, half), lambda b, s: (s, 0)),
            _const_spec(dmat.shape),
            _const_spec(qdec.shape),
            _const_spec(kdec.shape),
            _const_spec(cdec.shape),
            _const_spec((1, D_MODEL)),
            _const_spec(wr.shape),
            _const_spec(br.shape),
        ],
        out_specs=[
            pl.BlockSpec((tb * TILES_PER_ROW, LANES), tile),
            pl.BlockSpec((SUBLANES, tb), lambda b, s: (0, b * n_s + s)),
        ],
        out_shape=[
            jax.ShapeDtypeStruct((total * TILES_PER_ROW, LANES), jnp.float32),
            jax.ShapeDtypeStruct((SUBLANES, total), jnp.float32),
        ],
        scratch_shapes=[
            pltpu.VMEM((RET_HEADS, RET_DK, RET_DV), jnp.float32),
            pltpu.VMEM((tb, RET_V_DIM), jnp.bfloat16),
        ],
        compiler_params=pltpu.CompilerParams(
            dimension_semantics=("arbitrary", "arbitrary"),
            vmem_limit_bytes=VMEM_LIMIT),
        name="retention_layer",
    )(x2d, gain, w_in, w_out, cos, sin, dmat, qdec, kdec, cdec, gain2, wr, br)


def _pool_kernel(x_ref, gain_ref, pw_ref, scale_ref, gain2_ref, wr_ref, br_ref,
                 x1_ref, route_ref, ext_ref, y_ref):
    s = pl.program_id(1)
    tb = x_ref.shape[0] // TILES_PER_ROW

    @pl.when(s == 0)
    def _():
        ext_ref[0:POOL_HIST, :] = jnp.zeros((POOL_HIST, D_MODEL), jnp.float32)

    x = _load_rows(x_ref, tb)
    ext_ref[POOL_HIST:, :] = _rms(x, gain_ref[...])
    pos = s * tb + lax.broadcasted_iota(jnp.int32, (tb, 1), 0)
    for g, w in enumerate(POOL_WINDOWS):
        cs = slice(g * POOL_GROUP_DIM, (g + 1) * POOL_GROUP_DIM)
        e = ext_ref[:, cs]
        acc = e
        k = 1
        while k < w:
            acc = acc + pltpu.roll(acc, k, axis=0)
            k *= 2
        inv_cnt = 1.0 / jnp.minimum(pos + 1, w).astype(jnp.float32)
        pooled = acc[POOL_HIST:, :] * inv_cnt - e[POOL_HIST:, :]
        y_ref[:, cs] = _dot(pooled.astype(jnp.bfloat16), pw_ref[g])
    ext_ref[0:POOL_HIST, :] = ext_ref[tb:tb + POOL_HIST, :]
    x1 = x + y_ref[...] * scale_ref[...]
    _store_rows(x1_ref, x1)
    route_ref[...] = _route(x1, gain2_ref[...], wr_ref[...], br_ref[...])


def _pool_layer(x2d, batch, seq, gain, pw, scale, gain2, wr, br):
    tb = POOL_TILE
    n_s = seq // tb
    total = batch * seq
    tile = lambda b, s: (b * n_s + s, 0)
    return pl.pallas_call(
        _pool_kernel,
        grid=(batch, n_s),
        in_specs=[
            pl.BlockSpec((tb * TILES_PER_ROW, LANES), tile),
            _const_spec((1, D_MODEL)),
            _const_spec(pw.shape),
            _const_spec((1, D_MODEL)),
            _const_spec((1, D_MODEL)),
            _const_spec(wr.shape),
            _const_spec(br.shape),
        ],
        out_specs=[
            pl.BlockSpec((tb * TILES_PER_ROW, LANES), tile),
            pl.BlockSpec((SUBLANES, tb), lambda b, s: (0, b * n_s + s)),
        ],
        out_shape=[
            jax.ShapeDtypeStruct((total * TILES_PER_ROW, LANES), jnp.float32),
            jax.ShapeDtypeStruct((SUBLANES, total), jnp.float32),
        ],
        scratch_shapes=[
            pltpu.VMEM((POOL_HIST + tb, D_MODEL), jnp.float32),
            pltpu.VMEM((tb, D_MODEL), jnp.float32),
        ],
        compiler_params=pltpu.CompilerParams(
            dimension_semantics=("arbitrary", "arbitrary"),
            vmem_limit_bytes=VMEM_LIMIT),
        name="pool_layer",
    )(x2d, gain, pw, scale, gain2, wr, br)


def _moe_kernel(src_ref, dst_ref, nv_ref, ea_ref, eb_ref,
                x_hbm, wts_ref, gain_ref, fgain_ref, wgu_a_ref, wd_a_ref, wgu_b_ref, wd_b_ref,
                out_hbm, xbuf, obuf, sem_in, sem_out, *, final_norm, total):
    b = pl.program_id(0)
    n = nv_ref[b]
    n_prev = nv_ref[jnp.maximum(b - 1, 0)]
    xs = b % MOE_XBUFS
    osl = b % MOE_OBUFS

    def tile_of(row):
        return pl.ds(pl.multiple_of(row * TILES_PER_ROW, TILES_PER_ROW), TILES_PER_ROW)

    def start_gather(blk, sl):
        for r in range(MOE_ROWS):
            tok = src_ref[blk * MOE_ROWS + r]
            pltpu.make_async_copy(x_hbm.at[tile_of(tok)], xbuf.at[sl, tile_of(r)],
                                  sem_in.at[sl]).start(priority=r % DMA_THREADS)

    def wait_gather(sl):
        pltpu.make_async_copy(x_hbm.at[pl.ds(0, MOE_ROWS * TILES_PER_ROW)], xbuf.at[sl],
                              sem_in.at[sl]).wait()

    def out_copy(sl, r, tok):
        if final_norm:
            return pltpu.make_async_copy(obuf.at[sl, pl.ds(r, 1)], out_hbm.at[pl.ds(tok, 1)],
                                         sem_out.at[sl])
        return pltpu.make_async_copy(obuf.at[sl, tile_of(r)], out_hbm.at[tile_of(tok)],
                                     sem_out.at[sl])

    def start_scatter(blk, sl):
        for r in range(MOE_ROWS):
            out_copy(sl, r, dst_ref[blk * MOE_ROWS + r]).start(priority=r % DMA_THREADS)

    def wait_scatter(sl):
        pltpu.make_async_copy(obuf.at[sl], out_hbm.at[pl.ds(0, obuf.shape[1])],
                              sem_out.at[sl]).wait()

    @pl.when(b == 0)
    def _():
        obuf[...] = jnp.zeros(obuf.shape, jnp.float32)
        for sl in range(MOE_OBUFS):
            for r in range(MOE_ROWS):
                out_copy(sl, r, total + sl * MOE_ROWS + r).start()
        start_gather(0, 0)
        start_gather(1, 1)

    @pl.when(n > 0)
    def _():
        wait_gather(xs)
        start_gather(b + 2, (b + 2) % MOE_XBUFS)
        xb = _load_rows(xbuf.at[xs], MOE_ROWS)
        hb = _rms(xb, gain_ref[...]).astype(jnp.bfloat16)
        wt = wts_ref[0].T

        def expert(wgu_ref, wd_ref):
            gu = _dot(hb, wgu_ref[0])
            hid = _silu(gu[:, :MOE_D_FF]) * gu[:, MOE_D_FF:]
            return _dot(hid.astype(jnp.bfloat16), wd_ref[0])

        o = (xb + wt[:, 0:1] * expert(wgu_a_ref, wd_a_ref)
             + wt[:, 1:2] * expert(wgu_b_ref, wd_b_ref))
        wait_scatter(osl)
        if final_norm:
            obuf[osl] = _rms(o, fgain_ref[...])
        else:
            _store_rows(obuf.at[osl], o)
        start_scatter(b, osl)

    @pl.when(jnp.logical_and(n == 0, n_prev > 0))
    def _():
        wait_gather(xs)
        wait_gather((b + 1) % MOE_XBUFS)
        for sl in range(MOE_OBUFS):
            wait_scatter(sl)


def _class_tables():
    ea = np.zeros((N_CLASSES,), np.int32)
    eb = np.zeros((N_CLASSES,), np.int32)
    for g in range(MOE_GROUPS):
        c = g * PAIRS_PER_GROUP
        for a in range(MOE_EPG):
            for b in range(a + 1, MOE_EPG):
                ea[c], eb[c] = g * MOE_EPG + a, g * MOE_EPG + b
                c += 1
    return ea, eb


def _moe_layer(x_tiled, route, gain, fgain, wgu, wd, final_norm):
    total = route.shape[1]
    nb = total // MOE_ROWS + N_CLASSES + 1
    lo = route[0].astype(jnp.int32)
    hi = route[1].astype(jnp.int32)
    w_lo, w_hi = route[2], route[3]
    a = lo % MOE_EPG
    bb = hi % MOE_EPG
    cls = (lo // MOE_EPG) * PAIRS_PER_GROUP + (a * (2 * MOE_EPG - 1 - a)) // 2 + (bb - a - 1)
    order = jnp.argsort(cls, stable=True).astype(jnp.int32)
    counts = jnp.sum((cls[:, None] == jnp.arange(N_CLASSES, dtype=jnp.int32)[None, :]).astype(jnp.int32),
                     axis=0)
    starts = jnp.cumsum(counts) - counts
    nblk = (counts + MOE_ROWS - 1) // MOE_ROWS
    blk_end = jnp.cumsum(nblk)
    blk_first = blk_end - nblk
    bidx = jnp.arange(nb, dtype=jnp.int32)
    used = bidx < blk_end[-1]
    last_cls = jnp.max(jnp.where(counts > 0, jnp.arange(N_CLASSES, dtype=jnp.int32), 0))
    c = jnp.where(used, jnp.searchsorted(blk_end, bidx, side='right').astype(jnp.int32), last_cls)
    c = jnp.minimum(c, N_CLASSES - 1)
    j = bidx - blk_first[c]
    base = starts[c] + j * MOE_ROWS
    nvalid = jnp.where(used, jnp.clip(counts[c] - j * MOE_ROWS, 0, MOE_ROWS), 0).astype(jnp.int32)
    r = jnp.arange(MOE_ROWS, dtype=jnp.int32)
    valid = r[None, :] < nvalid[:, None]
    src = jnp.clip(base[:, None] + r[None, :], 0, total - 1)
    tok = jnp.where(valid, order[src], 0)
    dump = total + (bidx[:, None] % MOE_OBUFS) * MOE_ROWS + r[None, :]
    dst = jnp.where(valid, tok, dump)
    src_tab = jnp.concatenate([tok.reshape(-1), jnp.zeros((MOE_ROWS,), jnp.int32)])
    wts = jnp.zeros((nb, SUBLANES, MOE_ROWS), jnp.float32)
    wts = wts.at[:, 0, :].set(jnp.where(valid, w_lo[tok], 0.0))
    wts = wts.at[:, 1, :].set(jnp.where(valid, w_hi[tok], 0.0))
    ea_tab, eb_tab = _class_tables()
    ea = jnp.asarray(ea_tab)[c]
    eb = jnp.asarray(eb_tab)[c]

    tiled_block = (MOE_ROWS * TILES_PER_ROW, LANES)
    out_rows = total + MOE_OBUFS * MOE_ROWS
    if final_norm:
        out_block = (MOE_ROWS, D_MODEL)
        out_shape = (out_rows, D_MODEL)
    else:
        out_block = tiled_block
        out_shape = (out_rows * TILES_PER_ROW, LANES)

    grid_spec = pltpu.PrefetchScalarGridSpec(
        num_scalar_prefetch=5,
        grid=(nb,),
        in_specs=[
            pl.BlockSpec(memory_space=pl.ANY),
            pl.BlockSpec((1, SUBLANES, MOE_ROWS), lambda b, *_: (b, 0, 0)),
            pl.BlockSpec((1, D_MODEL), lambda b, *_: (0, 0)),
            pl.BlockSpec((1, D_MODEL), lambda b, *_: (0, 0)),
            pl.BlockSpec((1, D_MODEL, 2 * MOE_D_FF), lambda b, src, dst, nv, ea, eb: (ea[b], 0, 0)),
            pl.BlockSpec((1, MOE_D_FF, D_MODEL), lambda b, src, dst, nv, ea, eb: (ea[b], 0, 0)),
            pl.BlockSpec((1, D_MODEL, 2 * MOE_D_FF), lambda b, src, dst, nv, ea, eb: (eb[b], 0, 0)),
            pl.BlockSpec((1, MOE_D_FF, D_MODEL), lambda b, src, dst, nv, ea, eb: (eb[b], 0, 0)),
        ],
        out_specs=pl.BlockSpec(memory_space=pl.ANY),
        scratch_shapes=[
            pltpu.VMEM((MOE_XBUFS,) + tiled_block, jnp.float32),
            pltpu.VMEM((MOE_OBUFS,) + out_block, jnp.float32),
            pltpu.SemaphoreType.DMA((MOE_XBUFS,)),
            pltpu.SemaphoreType.DMA((MOE_OBUFS,)),
        ],
    )
    return pl.pallas_call(
        functools.partial(_moe_kernel, final_norm=final_norm, total=total),
        grid_spec=grid_spec,
        out_shape=jax.ShapeDtypeStruct(out_shape, jnp.float32),
        compiler_params=pltpu.CompilerParams(
            dimension_semantics=("arbitrary",),
            vmem_limit_bytes=VMEM_LIMIT),
        name="moe_layer",
    )(src_tab, dst.reshape(-1), nvalid, ea, eb, x_tiled, wts, gain, fgain, wgu, wd, wgu, wd)


def _router_params(w_group, b_group, w_expert, b_expert):
    pad = LANES - MOE_GROUPS - MOE_EXPERTS
    wr = jnp.concatenate([w_group, w_expert, jnp.zeros((D_MODEL, pad), jnp.float32)], axis=1)
    br = jnp.concatenate([b_group, b_expert.reshape(-1), jnp.zeros((pad,), jnp.float32)])[None, :]
    return wr, br


def kernel(x, norm_mix, norm_ffn, ret_w_in, ret_w_out, pool_w, pool_scale, moe_w_group, moe_b_group, moe_w_expert, moe_b_expert, moe_w_gate, moe_w_up, moe_w_down, final_norm):
    batch, seq, _ = x.shape
    x2d = x.reshape(batch * seq, D_MODEL)
    bf = jnp.bfloat16
    wgu = jnp.concatenate([moe_w_gate, moe_w_up], axis=-1).astype(bf)
    wd = moe_w_down.astype(bf)
    fgain = final_norm[None, :]

    wr0, br0 = _router_params(moe_w_group[0], moe_b_group[0], moe_w_expert[0], moe_b_expert[0])
    x1, route0 = _retention_layer(x2d, batch, seq, norm_mix[0][None, :], ret_w_in[0].astype(bf),
                                  ret_w_out[0].astype(bf), norm_ffn[0][None, :], wr0, br0)
    x2 = _moe_layer(x1, route0, norm_ffn[0][None, :], fgain, wgu[0], wd[0], final_norm=False)

    wr1, br1 = _router_params(moe_w_group[1], moe_b_group[1], moe_w_expert[1], moe_b_expert[1])
    x3, route1 = _pool_layer(x2, batch, seq, norm_mix[1][None, :], pool_w[0].astype(bf),
                             pool_scale[0][None, :], norm_ffn[1][None, :], wr1, br1)
    out = _moe_layer(x3, route1, norm_ffn[1][None, :], fgain, wgu[1], wd[1], final_norm=True)
    return out[:batch * seq].reshape(batch, seq, D_MODEL)
```

```python
import functools

import numpy as np
import jax
import jax.numpy as jnp
from jax import lax
from jax.experimental import pallas as pl
from jax.experimental.pallas import tpu as pltpu

D_MODEL = 1024
EPS = 1e-6
CHUNK = 64

RET_HEADS = 4
RET_DK = 256
RET_DV = 512
RET_QK_DIM = RET_HEADS * RET_DK
RET_V_DIM = RET_HEADS * RET_DV
ROPE_BASE = 10000.0

POOL_WINDOWS = (2, 4, 8, 16)
POOL_GROUP_DIM = 256
POOL_HIST = 16

MOE_GROUPS = 4
MOE_EPG = 8
MOE_EXPERTS = MOE_GROUPS * MOE_EPG
MOE_D_FF = 256
PAIRS_PER_GROUP = MOE_EPG * (MOE_EPG - 1) // 2
N_CLASSES = MOE_GROUPS * PAIRS_PER_GROUP

LANES = 128
SUBLANES = 8

RET_TILE = 256
POOL_TILE = 512
MOE_ROWS = 128
MOE_XBUFS = 3
MOE_OBUFS = 2
DMA_THREADS = 2
VMEM_LIMIT = 56 * 1024 * 1024


def _rms(x, gain):
    ms = jnp.mean(x * x, axis=-1, keepdims=True)
    return x * lax.rsqrt(ms + EPS) * gain


def _silu(x):
    return x * jax.nn.sigmoid(x)


def _dot(a, b):
    return jnp.dot(a, b, preferred_element_type=jnp.float32)


TILES_PER_ROW = D_MODEL // LANES


def _load_rows(ref, rows):
    return jnp.concatenate(
        [ref[pl.ds(s, rows, stride=TILES_PER_ROW), :] for s in range(TILES_PER_ROW)], axis=1)


def _store_rows(ref, val):
    rows = val.shape[0]
    for s in range(TILES_PER_ROW):
        ref[pl.ds(s, rows, stride=TILES_PER_ROW), :] = val[:, s * LANES:(s + 1) * LANES]


def _route(x1, gain2, wr, br):
    rows = x1.shape[0]
    h2 = _rms(x1, gain2)
    logits = jnp.dot(h2, wr, preferred_element_type=jnp.float32,
                     precision=lax.Precision.HIGHEST) + br
    lane = lax.broadcasted_iota(jnp.int32, (rows, LANES), 1)
    neg = jnp.float32(-jnp.inf)
    gmask = lane < MOE_GROUPS
    gl = jnp.where(gmask, logits, neg)
    gmax = jnp.max(gl, axis=1, keepdims=True)
    grp = jnp.min(jnp.where(gl == gmax, lane, LANES), axis=1, keepdims=True)
    gsum = jnp.sum(jnp.where(gmask, jnp.exp(gl - gmax), 0.0), axis=1, keepdims=True)
    g_gate = 1.0 / gsum
    first = MOE_GROUPS + MOE_EPG * grp
    emask = (lane >= first) & (lane < first + MOE_EPG)
    el = jnp.where(emask, logits, neg)
    m1 = jnp.max(el, axis=1, keepdims=True)
    i1 = jnp.min(jnp.where(el == m1, lane, LANES), axis=1, keepdims=True)
    el2 = jnp.where(lane == i1, neg, el)
    m2 = jnp.max(el2, axis=1, keepdims=True)
    i2 = jnp.min(jnp.where(el2 == m2, lane, LANES), axis=1, keepdims=True)
    t = jnp.exp(m2 - m1)
    w1 = g_gate / (1.0 + t)
    w2 = g_gate * t / (1.0 + t)
    first_lo = i1 < i2
    lo = jnp.where(first_lo, i1, i2) - MOE_GROUPS
    hi = jnp.where(first_lo, i2, i1) - MOE_GROUPS
    w_lo = jnp.where(first_lo, w1, w2)
    w_hi = jnp.where(first_lo, w2, w1)
    packed = jnp.where(lane == 0, lo.astype(jnp.float32),
             jnp.where(lane == 1, hi.astype(jnp.float32),
             jnp.where(lane == 2, w_lo,
             jnp.where(lane == 3, w_hi, 0.0))))
    return packed.T[:SUBLANES, :]


def _ret_kernel(x_ref, gain_ref, win_ref, wout_ref, cos_ref, sin_ref, dmat_ref,
                qdec_ref, kdec_ref, cdec_ref, gain2_ref, wr_ref, br_ref,
                x1_ref, route_ref, state_ref, gated_ref):
    @pl.when(pl.program_id(1) == 0)
    def _():
        state_ref[...] = jnp.zeros_like(state_ref)

    x = x_ref[...]
    h = _rms(x, gain_ref[...]).astype(jnp.bfloat16)
    cos = cos_ref[...]
    sin = sin_ref[...]
    half = RET_DK // 2

    def rope(t):
        t1, t2 = t[:, :half], t[:, half:]
        return jnp.concatenate([t1 * cos - t2 * sin, t1 * sin + t2 * cos], axis=1)

    for hd in range(RET_HEADS):
        q = _dot(h, win_ref[:, hd * RET_DK:(hd + 1) * RET_DK])
        k = _dot(h, win_ref[:, RET_QK_DIM + hd * RET_DK:RET_QK_DIM + (hd + 1) * RET_DK])
        v0 = 2 * RET_QK_DIM + hd * RET_DV
        v = _dot(h, win_ref[:, v0:v0 + RET_DV])
        g0 = 2 * RET_QK_DIM + RET_V_DIM + hd * RET_DV
        g = _dot(h, win_ref[:, g0:g0 + RET_DV])
        q = rope(q)
        k = rope(k) * (RET_DK ** -0.5)
        qb = q.astype(jnp.bfloat16)
        kb = k.astype(jnp.bfloat16)
        vb = v.astype(jnp.bfloat16)
        sc = lax.dot_general(qb, kb, (((1,), (1,)), ((), ())),
                             preferred_element_type=jnp.float32)
        sc = sc * dmat_ref[hd]
        intra = _dot(sc.astype(jnp.bfloat16), vb)
        st = state_ref[hd]
        cross = _dot(qb, st.astype(jnp.bfloat16)) * qdec_ref[hd]
        o = intra + cross
        kd = (k * kdec_ref[hd]).astype(jnp.bfloat16)
        upd = lax.dot_general(kd, vb, (((0,), (0,)), ((), ())),
                              preferred_element_type=jnp.float32)
        state_ref[hd] = st * cdec_ref[hd] + upd
        mu = jnp.mean(o, axis=-1, keepdims=True)
        oc = o - mu
        var = jnp.mean(oc * oc, axis=-1, keepdims=True)
        on = oc * lax.rsqrt(var + EPS)
        gated_ref[:, hd * RET_DV:(hd + 1) * RET_DV] = (_silu(g) * on).astype(jnp.bfloat16)

    x1 = x + _dot(gated_ref[...], wout_ref[...])
    _store_rows(x1_ref, x1)
    route_ref[...] = _route(x1, gain2_ref[...], wr_ref[...], br_ref[...])


def _const_spec(shape):
    nd = len(shape)
    return pl.BlockSpec(shape, lambda *_: (0,) * nd, pipeline_mode=pl.Buffered(1))


def _retention_layer(x2d, batch, seq, gain, w_in, w_out, gain2, wr, br):
    tb = RET_TILE
    n_s = seq // tb
    total = batch * seq
    log_g = jnp.log(1.0 - jnp.exp2(-5.0 - jnp.arange(RET_HEADS, dtype=jnp.float32)))
    n = jnp.arange(tb, dtype=jnp.float32)
    diff = n[:, None] - n[None, :]
    cn = jnp.arange(tb)[:, None] // CHUNK
    cm = jnp.arange(tb)[None, :] // CHUNK
    expo = jnp.where(cn == cm, jnp.abs(diff), diff)
    dmat = jnp.where((cm <= cn)[None], jnp.exp(log_g[:, None, None] * expo[None]), 0.0)
    qdec = jnp.exp(log_g[:, None] * (n + 1.0)[None])[:, :, None]
    kdec = jnp.exp(log_g[:, None] * (tb - 1.0 - n)[None])[:, :, None]
    cdec = jnp.broadcast_to(jnp.exp(log_g * tb)[:, None, None], (RET_HEADS, 1, RET_DV))
    half = RET_DK // 2
    inv_freq = 1.0 / (ROPE_BASE ** (jnp.arange(half, dtype=jnp.float32) / half))
    ang = jnp.arange(seq, dtype=jnp.float32)[:, None] * inv_freq[None, :]
    cos, sin = jnp.cos(ang), jnp.sin(ang)

    tile = lambda b, s: (b * n_s + s, 0)
    return pl.pallas_call(
        _ret_kernel,
        grid=(batch, n_s),
        in_specs=[
            pl.BlockSpec((tb, D_MODEL), tile),
            _const_spec((1, D_MODEL)),
            _const_spec(w_in.shape),
            _const_spec(w_out.shape),
            pl.BlockSpec((tb, half), lambda b, s: (s, 0)),
            pl.BlockSpec((tb, half), lambda b, s: (s, 0)),
            _const_spec(dmat.shape),
            _const_spec(qdec.shape),
            _const_spec(kdec.shape),
            _const_spec(cdec.shape),
            _const_spec((1, D_MODEL)),
            _const_spec(wr.shape),
            _const_spec(br.shape),
        ],
        out_specs=[
            pl.BlockSpec((tb * TILES_PER_ROW, LANES), tile),
            pl.BlockSpec((SUBLANES, tb), lambda b, s: (0, b * n_s + s)),
        ],
        out_shape=[
            jax.ShapeDtypeStruct((total * TILES_PER_ROW, LANES), jnp.float32),
            jax.ShapeDtypeStruct((SUBLANES, total), jnp.float32),
        ],
        scratch_shapes=[
            pltpu.VMEM((RET_HEADS, RET_DK, RET_DV), jnp.float32),
            pltpu.VMEM((tb, RET_V_DIM), jnp.bfloat16),
        ],
        compiler_params=pltpu.CompilerParams(
            dimension_semantics=("arbitrary", "arbitrary"),
            vmem_limit_bytes=VMEM_LIMIT),
        name="retention_layer",
    )(x2d, gain, w_in, w_out, cos, sin, dmat, qdec, kdec, cdec, gain2, wr, br)


def _pool_kernel(x_ref, gain_ref, pw_ref, scale_ref, gain2_ref, wr_ref, br_ref,
                 x1_ref, route_ref, ext_ref, y_ref):
    s = pl.program_id(1)
    tb = x_ref.shape[0] // TILES_PER_ROW

    @pl.when(s == 0)
    def _():
        ext_ref[0:POOL_HIST, :] = jnp.zeros((POOL_HIST, D_MODEL), jnp.float32)

    x = _load_rows(x_ref, tb)
    ext_ref[POOL_HIST:, :] = _rms(x, gain_ref[...])
    pos = s * tb + lax.broadcasted_iota(jnp.int32, (tb, 1), 0)
    for g, w in enumerate(POOL_WINDOWS):
        cs = slice(g * POOL_GROUP_DIM, (g + 1) * POOL_GROUP_DIM)
        e = ext_ref[:, cs]
        acc = e
        k = 1
        while k < w:
            acc = acc + pltpu.roll(acc, k, axis=0)
            k *= 2
        inv_cnt = 1.0 / jnp.minimum(pos + 1, w).astype(jnp.float32)
        pooled = acc[POOL_HIST:, :] * inv_cnt - e[POOL_HIST:, :]
        y_ref[:, cs] = _dot(pooled.astype(jnp.bfloat16), pw_ref[g])
    ext_ref[0:POOL_HIST, :] = ext_ref[tb:tb + POOL_HIST, :]
    x1 = x + y_ref[...] * scale_ref[...]
    _store_rows(x1_ref, x1)
    route_ref[...] = _route(x1, gain2_ref[...], wr_ref[...], br_ref[...])


def _pool_layer(x2d, batch, seq, gain, pw, scale, gain2, wr, br):
    tb = POOL_TILE
    n_s = seq // tb
    total = batch * seq
    tile = lambda b, s: (b * n_s + s, 0)
    return pl.pallas_call(
        _pool_kernel,
        grid=(batch, n_s),
        in_specs=[
            pl.BlockSpec((tb * TILES_PER_ROW, LANES), tile),
            _const_spec((1, D_MODEL)),
            _const_spec(pw.shape),
            _const_spec((1, D_MODEL)),
            _const_spec((1, D_MODEL)),
            _const_spec(wr.shape),
            _const_spec(br.shape),
        ],
        out_specs=[
            pl.BlockSpec((tb * TILES_PER_ROW, LANES), tile),
            pl.BlockSpec((SUBLANES, tb), lambda b, s: (0, b * n_s + s)),
        ],
        out_shape=[
            jax.ShapeDtypeStruct((total * TILES_PER_ROW, LANES), jnp.float32),
            jax.ShapeDtypeStruct((SUBLANES, total), jnp.float32),
        ],
        scratch_shapes=[
            pltpu.VMEM((POOL_HIST + tb, D_MODEL), jnp.float32),
            pltpu.VMEM((tb, D_MODEL), jnp.float32),
        ],
        compiler_params=pltpu.CompilerParams(
            dimension_semantics=("arbitrary", "arbitrary"),
            vmem_limit_bytes=VMEM_LIMIT),
        name="pool_layer",
    )(x2d, gain, pw, scale, gain2, wr, br)


def _moe_kernel(src_ref, dst_ref, nv_ref, ea_ref, eb_ref,
                x_hbm, wts_ref, gain_ref, fgain_ref, wgu_a_ref, wd_a_ref, wgu_b_ref, wd_b_ref,
                out_hbm, xbuf, obuf, sem_in, sem_out, *, final_norm, total):
    b = pl.program_id(0)
    n = nv_ref[b]
    n_prev = nv_ref[jnp.maximum(b - 1, 0)]
    xs = b % MOE_XBUFS
    osl = b % MOE_OBUFS

    def tile_of(row):
        return pl.ds(pl.multiple_of(row * TILES_PER_ROW, TILES_PER_ROW), TILES_PER_ROW)

    def start_gather(blk, sl):
        for r in range(MOE_ROWS):
            tok = src_ref[blk * MOE_ROWS + r]
            pltpu.make_async_copy(x_hbm.at[tile_of(tok)], xbuf.at[sl, tile_of(r)],
                                  sem_in.at[sl]).start(priority=r % DMA_THREADS)

    def wait_gather(sl):
        pltpu.make_async_copy(x_hbm.at[pl.ds(0, MOE_ROWS * TILES_PER_ROW)], xbuf.at[sl],
                              sem_in.at[sl]).wait()

    def out_copy(sl, r, tok):
        if final_norm:
            return pltpu.make_async_copy(obuf.at[sl, pl.ds(r, 1)], out_hbm.at[pl.ds(tok, 1)],
                                         sem_out.at[sl])
        return pltpu.make_async_copy(obuf.at[sl, tile_of(r)], out_hbm.at[tile_of(tok)],
                                     sem_out.at[sl])

    def start_scatter(blk, sl):
        for r in range(MOE_ROWS):
            out_copy(sl, r, dst_ref[blk * MOE_ROWS + r]).start(priority=r % DMA_THREADS)

    def wait_scatter(sl):
        pltpu.make_async_copy(obuf.at[sl], out_hbm.at[pl.ds(0, obuf.shape[1])],
                              sem_out.at[sl]).wait()

    @pl.when(b == 0)
    def _():
        obuf[...] = jnp.zeros(obuf.shape, jnp.float32)
        for sl in range(MOE_OBUFS):
            for r in range(MOE_ROWS):
                out_copy(sl, r, total + sl * MOE_ROWS + r).start()
        start_gather(0, 0)
        start_gather(1, 1)

    @pl.when(n > 0)
    def _():
        wait_gather(xs)
        start_gather(b + 2, (b + 2) % MOE_XBUFS)
        xb = _load_rows(xbuf.at[xs], MOE_ROWS)
        hb = _rms(xb, gain_ref[...]).astype(jnp.bfloat16)
        wt = wts_ref[0].T

        def expert(wgu_ref, wd_ref):
            gu = _dot(hb, wgu_ref[0])
            hid = _silu(gu[:, :MOE_D_FF]) * gu[:, MOE_D_FF:]
            return _dot(hid.astype(jnp.bfloat16), wd_ref[0])

        o = (xb + wt[:, 0:1] * expert(wgu_a_ref, wd_a_ref)
             + wt[:, 1:2] * expert(wgu_b_ref, wd_b_ref))
        wait_scatter(osl)
        if final_norm:
            obuf[osl] = _rms(o, fgain_ref[...])
        else:
            _store_rows(obuf.at[osl], o)
        start_scatter(b, osl)

    @pl.when(jnp.logical_and(n == 0, n_prev > 0))
    def _():
        wait_gather(xs)
        wait_gather((b + 1) % MOE_XBUFS)
        for sl in range(MOE_OBUFS):
            wait_scatter(sl)


def _class_tables():
    ea = np.zeros((N_CLASSES,), np.int32)
    eb = np.zeros((N_CLASSES,), np.int32)
    for g in range(MOE_GROUPS):
        c = g * PAIRS_PER_GROUP
        for a in range(MOE_EPG):
            for b in range(a + 1, MOE_EPG):
                ea[c], eb[c] = g * MOE_EPG + a, g * MOE_EPG + b
                c += 1
    return ea, eb


def _moe_layer(x_tiled, route, gain, fgain, wgu, wd, final_norm):
    total = route.shape[1]
    nb = total // MOE_ROWS + N_CLASSES + 1
    lo = route[0].astype(jnp.int32)
    hi = route[1].astype(jnp.int32)
    w_lo, w_hi = route[2], route[3]
    a = lo % MOE_EPG
    bb = hi % MOE_EPG
    cls = (lo // MOE_EPG) * PAIRS_PER_GROUP + (a * (2 * MOE_EPG - 1 - a)) // 2 + (bb - a - 1)
    cls_ids = jnp.arange(N_CLASSES, dtype=jnp.int32)
    counts = jnp.sum((cls[:, None] == cls_ids[None, :]).astype(jnp.int32), axis=0)
    nblk = (counts + MOE_ROWS - 1) // MOE_ROWS
    npad = nblk * MOE_ROWS - counts
    max_pad = MOE_ROWS - 1
    cand_cls = jnp.repeat(cls_ids, max_pad)
    cand_on = jnp.tile(jnp.arange(max_pad, dtype=jnp.int32), N_CLASSES) < jnp.repeat(npad, max_pad)
    n_slots = nb * MOE_ROWS
    n_fill = n_slots - total - N_CLASSES * max_pad
    keys = jnp.concatenate([2 * cls, jnp.where(cand_on, 2 * cand_cls + 1, 2 * N_CLASSES),
                            jnp.full((n_fill,), 2 * N_CLASSES, jnp.int32)])
    n_extra = n_slots - total
    toks = jnp.concatenate([jnp.arange(total, dtype=jnp.int32), jnp.full((n_extra,), -1, jnp.int32)])
    zeros = jnp.zeros((n_extra,), jnp.float32)
    _, tok_s, wlo_s, whi_s = lax.sort(
        (keys, toks, jnp.concatenate([w_lo, zeros]), jnp.concatenate([w_hi, zeros])),
        num_keys=1, is_stable=True)
    is_pad = tok_s < 0
    slot = jnp.arange(n_slots, dtype=jnp.int32)
    dump = total + ((slot // MOE_ROWS) % MOE_OBUFS) * MOE_ROWS + slot % MOE_ROWS
    src_tab = jnp.concatenate([jnp.where(is_pad, 0, tok_s),
                               jnp.zeros((MOE_ROWS,), jnp.int32)])
    dst = jnp.where(is_pad, dump, tok_s)
    nvalid = jnp.sum((~is_pad).reshape(nb, MOE_ROWS).astype(jnp.int32), axis=1)
    wts = jnp.concatenate(
        [wlo_s.reshape(nb, 1, MOE_ROWS), whi_s.reshape(nb, 1, MOE_ROWS),
         jnp.zeros((nb, SUBLANES - 2, MOE_ROWS), jnp.float32)], axis=1)
    blk_end = jnp.cumsum(nblk)
    bidx = jnp.arange(nb, dtype=jnp.int32)
    c_blk = jnp.sum((blk_end[None, :] <= jnp.minimum(bidx, blk_end[-1] - 1)[:, None]).astype(jnp.int32),
                    axis=1)
    ea_tab, eb_tab = _class_tables()
    onehot = (c_blk[:, None] == cls_ids[None, :]).astype(jnp.int32)
    ea = jnp.sum(onehot * jnp.asarray(ea_tab)[None, :], axis=1)
    eb = jnp.sum(onehot * jnp.asarray(eb_tab)[None, :], axis=1)

    tiled_block = (MOE_ROWS * TILES_PER_ROW, LANES)
    out_rows = total + MOE_OBUFS * MOE_ROWS
    if final_norm:
        out_block = (MOE_ROWS, D_MODEL)
        out_shape = (out_rows, D_MODEL)
    else:
        out_block = tiled_block
        out_shape = (out_rows * TILES_PER_ROW, LANES)

    grid_spec = pltpu.PrefetchScalarGridSpec(
        num_scalar_prefetch=5,
        grid=(nb,),
        in_specs=[
            pl.BlockSpec(memory_space=pl.ANY),
            pl.BlockSpec((1, SUBLANES, MOE_ROWS), lambda b, *_: (b, 0, 0)),
            pl.BlockSpec((1, D_MODEL), lambda b, *_: (0, 0)),
            pl.BlockSpec((1, D_MODEL), lambda b, *_: (0, 0)),
            pl.BlockSpec((1, D_MODEL, 2 * MOE_D_FF), lambda b, src, dst, nv, ea, eb: (ea[b], 0, 0)),
            pl.BlockSpec((1, MOE_D_FF, D_MODEL), lambda b, src, dst, nv, ea, eb: (ea[b], 0, 0)),
            pl.BlockSpec((1, D_MODEL, 2 * MOE_D_FF), lambda b, src, dst, nv, ea, eb: (eb[b], 0, 0)),
            pl.BlockSpec((1, MOE_D_FF, D_MODEL), lambda b, src, dst, nv, ea, eb: (eb[b], 0, 0)),
        ],
        out_specs=pl.BlockSpec(memory_space=pl.ANY),
        scratch_shapes=[
            pltpu.VMEM((MOE_XBUFS,) + tiled_block, jnp.float32),
            pltpu.VMEM((MOE_OBUFS,) + out_block, jnp.float32),
            pltpu.SemaphoreType.DMA((MOE_XBUFS,)),
            pltpu.SemaphoreType.DMA((MOE_OBUFS,)),
        ],
    )
    return pl.pallas_call(
        functools.partial(_moe_kernel, final_norm=final_norm, total=total),
        grid_spec=grid_spec,
        out_shape=jax.ShapeDtypeStruct(out_shape, jnp.float32),
        compiler_params=pltpu.CompilerParams(
            dimension_semantics=("arbitrary",),
            vmem_limit_bytes=VMEM_LIMIT),
        name="moe_layer",
    )(src_tab, dst.reshape(-1), nvalid, ea, eb, x_tiled, wts, gain, fgain, wgu, wd, wgu, wd)


def _router_params(w_group, b_group, w_expert, b_expert):
    pad = LANES - MOE_GROUPS - MOE_EXPERTS
    wr = jnp.concatenate([w_group, w_expert, jnp.zeros((D_MODEL, pad), jnp.float32)], axis=1)
    br = jnp.concatenate([b_group, b_expert.reshape(-1), jnp.zeros((pad,), jnp.float32)])[None, :]
    return wr, br


def kernel(x, norm_mix, norm_ffn, ret_w_in, ret_w_out, pool_w, pool_scale, moe_w_group, moe_b_group, moe_w_expert, moe_b_expert, moe_w_gate, moe_w_up, moe_w_down, final_norm):
    batch, seq, _ = x.shape
    x2d = x.reshape(batch * seq, D_MODEL)
    bf = jnp.bfloat16
    wgu = jnp.concatenate([moe_w_gate, moe_w_up], axis=-1).astype(bf)
    wd = moe_w_down.astype(bf)
    fgain = final_norm[None, :]

    wr0, br0 = _router_params(moe_w_group[0], moe_b_group[0], moe_w_expert[0], moe_b_expert[0])
    x1, route0 = _retention_layer(x2d, batch, seq, norm_mix[0][None, :], ret_w_in[0].astype(bf),
                                  ret_w_out[0].astype(bf), norm_ffn[0][None, :], wr0, br0)
    x2 = _moe_layer(x1, route0, norm_ffn[0][None, :], fgain, wgu[0], wd[0], final_norm=False)

    wr1, br1 = _router_params(moe_w_group[1], moe_b_group[1], moe_w_expert[1], moe_b_expert[1])
    x3, route1 = _pool_layer(x2, batch, seq, norm_mix[1][None, :], pool_w[0].astype(bf),
                             pool_scale[0][None, :], norm_ffn[1][None, :], wr1, br1)
    out = _moe_layer(x3, route1, norm_ffn[1][None, :], fgain, wgu[1], wd[1], final_norm=True)
    return out[:batch * seq].reshape(batch, seq, D_MODEL)
```

```python
import functools

import numpy as np
import jax
import jax.numpy as jnp
from jax import lax
from jax.experimental import pallas as pl
from jax.experimental.pallas import tpu as pltpu

D_MODEL = 1024
EPS = 1e-6
CHUNK = 64

RET_HEADS = 4
RET_DK = 256
RET_DV = 512
RET_QK_DIM = RET_HEADS * RET_DK
RET_V_DIM = RET_HEADS * RET_DV
ROPE_BASE = 10000.0

POOL_WINDOWS = (2, 4, 8, 16)
POOL_GROUP_DIM = 256
POOL_HIST = 16

MOE_GROUPS = 4
MOE_EPG = 8
MOE_EXPERTS = MOE_GROUPS * MOE_EPG
MOE_D_FF = 256
PAIRS_PER_GROUP = MOE_EPG * (MOE_EPG - 1) // 2
N_CLASSES = MOE_GROUPS * PAIRS_PER_GROUP

LANES = 128
SUBLANES = 8

RET_TILE = 256
POOL_TILE = 512
MOE_ROWS = 128
MOE_XBUFS = 3
MOE_OBUFS = 2
DMA_THREADS = 2
VMEM_LIMIT = 56 * 1024 * 1024


def _rms(x, gain):
    ms = jnp.mean(x * x, axis=-1, keepdims=True)
    return x * lax.rsqrt(ms + EPS) * gain


def _silu(x):
    return x * jax.nn.sigmoid(x)


def _dot(a, b):
    return jnp.dot(a, b, preferred_element_type=jnp.float32)


ROUTE_EXPERT_COL = 8
TILES_PER_ROW = D_MODEL // LANES


def _load_rows(ref, rows):
    return jnp.concatenate(
        [ref[pl.ds(s, rows, stride=TILES_PER_ROW), :] for s in range(TILES_PER_ROW)], axis=1)


def _store_rows(ref, val):
    rows = val.shape[0]
    for s in range(TILES_PER_ROW):
        ref[pl.ds(s, rows, stride=TILES_PER_ROW), :] = val[:, s * LANES:(s + 1) * LANES]


def _route(x1, gain2, wr3, br):
    rows = x1.shape[0]
    h2 = _rms(x1, gain2)
    h_hi = h2.astype(jnp.bfloat16)
    h_lo = (h2 - h_hi.astype(jnp.float32)).astype(jnp.bfloat16)
    logits = _dot(jnp.concatenate([h_hi, h_lo, h_hi], axis=1), wr3) + br
    lt = logits.T
    sub = lax.broadcasted_iota(jnp.int32, (SUBLANES, rows), 0)
    neg = jnp.float32(-jnp.inf)
    gl = jnp.where(sub < MOE_GROUPS, lt[0:SUBLANES], neg)
    gmax = jnp.max(gl, axis=0, keepdims=True)
    grp = jnp.min(jnp.where(gl == gmax, sub, SUBLANES), axis=0, keepdims=True)
    g_gate = 1.0 / jnp.sum(jnp.exp(gl - gmax), axis=0, keepdims=True)
    el = lt[ROUTE_EXPERT_COL:ROUTE_EXPERT_COL + MOE_EPG]
    for g in range(1, MOE_GROUPS):
        first = ROUTE_EXPERT_COL + g * MOE_EPG
        el = jnp.where(grp == g, lt[first:first + MOE_EPG], el)
    m1 = jnp.max(el, axis=0, keepdims=True)
    i1 = jnp.min(jnp.where(el == m1, sub, SUBLANES), axis=0, keepdims=True)
    el2 = jnp.where(sub == i1, neg, el)
    m2 = jnp.max(el2, axis=0, keepdims=True)
    i2 = jnp.min(jnp.where(el2 == m2, sub, SUBLANES), axis=0, keepdims=True)
    t = jnp.exp(m2 - m1)
    w1 = g_gate / (1.0 + t)
    w2 = g_gate * t / (1.0 + t)
    first_lo = i1 < i2
    base = grp * MOE_EPG
    lo = (base + jnp.where(first_lo, i1, i2)).astype(jnp.float32)
    hi = (base + jnp.where(first_lo, i2, i1)).astype(jnp.float32)
    w_lo = jnp.where(first_lo, w1, w2)
    w_hi = jnp.where(first_lo, w2, w1)
    return jnp.where(sub == 0, lo, jnp.where(sub == 1, hi, jnp.where(sub == 2, w_lo,
                     jnp.where(sub == 3, w_hi, 0.0))))


def _ret_kernel(x_ref, gain_ref, win_ref, wout_ref, cos_ref, sin_ref, dmat_ref,
                qdec_ref, kdec_ref, cdec_ref, gain2_ref, wr_ref, br_ref,
                x1_ref, route_ref, state_ref, gated_ref):
    @pl.when(pl.program_id(1) == 0)
    def _():
        state_ref[...] = jnp.zeros_like(state_ref)

    x = x_ref[...]
    h = _rms(x, gain_ref[...]).astype(jnp.bfloat16)
    cos = cos_ref[...]
    sin = sin_ref[...]
    half = RET_DK // 2

    def rope(t):
        t1, t2 = t[:, :half], t[:, half:]
        return jnp.concatenate([t1 * cos - t2 * sin, t1 * sin + t2 * cos], axis=1)

    for hd in range(RET_HEADS):
        q = _dot(h, win_ref[:, hd * RET_DK:(hd + 1) * RET_DK])
        k = _dot(h, win_ref[:, RET_QK_DIM + hd * RET_DK:RET_QK_DIM + (hd + 1) * RET_DK])
        v0 = 2 * RET_QK_DIM + hd * RET_DV
        v = _dot(h, win_ref[:, v0:v0 + RET_DV])
        g0 = 2 * RET_QK_DIM + RET_V_DIM + hd * RET_DV
        g = _dot(h, win_ref[:, g0:g0 + RET_DV])
        q = rope(q)
        k = rope(k) * (RET_DK ** -0.5)
        qb = q.astype(jnp.bfloat16)
        kb = k.astype(jnp.bfloat16)
        vb = v.astype(jnp.bfloat16)
        sc = lax.dot_general(qb, kb, (((1,), (1,)), ((), ())),
                             preferred_element_type=jnp.float32)
        sc = sc * dmat_ref[hd]
        intra = _dot(sc.astype(jnp.bfloat16), vb)
        st = state_ref[hd]
        cross = _dot(qb, st.astype(jnp.bfloat16)) * qdec_ref[hd]
        o = intra + cross
        kd = (k * kdec_ref[hd]).astype(jnp.bfloat16)
        upd = lax.dot_general(kd, vb, (((0,), (0,)), ((), ())),
                              preferred_element_type=jnp.float32)
        state_ref[hd] = st * cdec_ref[hd] + upd
        mu = jnp.mean(o, axis=-1, keepdims=True)
        oc = o - mu
        var = jnp.mean(oc * oc, axis=-1, keepdims=True)
        on = oc * lax.rsqrt(var + EPS)
        gated_ref[:, hd * RET_DV:(hd + 1) * RET_DV] = (_silu(g) * on).astype(jnp.bfloat16)

    x1 = x + _dot(gated_ref[...], wout_ref[...])
    _store_rows(x1_ref, x1)
    route_ref[...] = _route(x1, gain2_ref[...], wr_ref[...], br_ref[...])


def _const_spec(shape):
    nd = len(shape)
    return pl.BlockSpec(shape, lambda *_: (0,) * nd, pipeline_mode=pl.Buffered(1))


def _retention_layer(x2d, batch, seq, gain, w_in, w_out, gain2, wr, br):
    tb = RET_TILE
    n_s = seq // tb
    total = batch * seq
    log_g = jnp.log(1.0 - jnp.exp2(-5.0 - jnp.arange(RET_HEADS, dtype=jnp.float32)))
    n = jnp.arange(tb, dtype=jnp.float32)
    diff = n[:, None] - n[None, :]
    cn = jnp.arange(tb)[:, None] // CHUNK
    cm = jnp.arange(tb)[None, :] // CHUNK
    expo = jnp.where(cn == cm, jnp.abs(diff), diff)
    dmat = jnp.where((cm <= cn)[None], jnp.exp(log_g[:, None, None] * expo[None]), 0.0)
    qdec = jnp.exp(log_g[:, None] * (n + 1.0)[None])[:, :, None]
    kdec = jnp.exp(log_g[:, None] * (tb - 1.0 - n)[None])[:, :, None]
    cdec = jnp.broadcast_to(jnp.exp(log_g * tb)[:, None, None], (RET_HEADS, 1, RET_DV))
    half = RET_DK // 2
    inv_freq = 1.0 / (ROPE_BASE ** (jnp.arange(half, dtype=jnp.float32) / half))
    ang = jnp.arange(seq, dtype=jnp.float32)[:, None] * inv_freq[None, :]
    cos, sin = jnp.cos(ang), jnp.sin(ang)

    tile = lambda b, s: (b * n_s + s, 0)
    return pl.pallas_call(
        _ret_kernel,
        grid=(batch, n_s),
        in_specs=[
            pl.BlockSpec((tb, D_MODEL), tile),
            _const_spec((1, D_MODEL)),
            _const_spec(w_in.shape),
            _const_spec(w_out.shape),
            pl.BlockSpec((tb, half), lambda b, s: (s, 0)),
            pl.BlockSpec((tb, half), lambda b, s: (s, 0)),
            _const_spec(dmat.shape),
            _const_spec(qdec.shape),
            _const_spec(kdec.shape),
            _const_spec(cdec.shape),
            _const_spec((1, D_MODEL)),
            _const_spec(wr.shape),
            _const_spec(br.shape),
        ],
        out_specs=[
            pl.BlockSpec((tb * TILES_PER_ROW, LANES), tile),
            pl.BlockSpec((SUBLANES, tb), lambda b, s: (0, b * n_s + s)),
        ],
        out_shape=[
            jax.ShapeDtypeStruct((total * TILES_PER_ROW, LANES), jnp.float32),
            jax.ShapeDtypeStruct((SUBLANES, total), jnp.float32),
        ],
        scratch_shapes=[
            pltpu.VMEM((RET_HEADS, RET_DK, RET_DV), jnp.float32),
            pltpu.VMEM((tb, RET_V_DIM), jnp.bfloat16),
        ],
        compiler_params=pltpu.CompilerParams(
            dimension_semantics=("arbitrary", "arbitrary"),
            vmem_limit_bytes=VMEM_LIMIT),
        name="retention_layer",
    )(x2d, gain, w_in, w_out, cos, sin, dmat, qdec, kdec, cdec, gain2, wr, br)


def _pool_kernel(x_ref, gain_ref, pw_ref, scale_ref, gain2_ref, wr_ref, br_ref,
                 x1_ref, route_ref, ext_ref, y_ref):
    s = pl.program_id(1)
    tb = x_ref.shape[0] // TILES_PER_ROW

    @pl.when(s == 0)
    def _():
        ext_ref[0:POOL_HIST, :] = jnp.zeros((POOL_HIST, D_MODEL), jnp.float32)

    x = _load_rows(x_ref, tb)
    ext_ref[POOL_HIST:, :] = _rms(x, gain_ref[...])
    pos = s * tb + lax.broadcasted_iota(jnp.int32, (tb, 1), 0)
    for g, w in enumerate(POOL_WINDOWS):
        cs = slice(g * POOL_GROUP_DIM, (g + 1) * POOL_GROUP_DIM)
        e = ext_ref[:, cs]
        acc = e
        k = 1
        while k < w:
            acc = acc + pltpu.roll(acc, k, axis=0)
            k *= 2
        inv_cnt = 1.0 / jnp.minimum(pos + 1, w).astype(jnp.float32)
        pooled = acc[POOL_HIST:, :] * inv_cnt - e[POOL_HIST:, :]
        y_ref[:, cs] = _dot(pooled.astype(jnp.bfloat16), pw_ref[g])
    ext_ref[0:POOL_HIST, :] = ext_ref[tb:tb + POOL_HIST, :]
    x1 = x + y_ref[...] * scale_ref[...]
    _store_rows(x1_ref, x1)
    route_ref[...] = _route(x1, gain2_ref[...], wr_ref[...], br_ref[...])


def _pool_layer(x2d, batch, seq, gain, pw, scale, gain2, wr, br):
    tb = POOL_TILE
    n_s = seq // tb
    total = batch * seq
    tile = lambda b, s: (b * n_s + s, 0)
    return pl.pallas_call(
        _pool_kernel,
        grid=(batch, n_s),
        in_specs=[
            pl.BlockSpec((tb * TILES_PER_ROW, LANES), tile),
            _const_spec((1, D_MODEL)),
            _const_spec(pw.shape),
            _const_spec((1, D_MODEL)),
            _const_spec((1, D_MODEL)),
            _const_spec(wr.shape),
            _const_spec(br.shape),
        ],
        out_specs=[
            pl.BlockSpec((tb * TILES_PER_ROW, LANES), tile),
            pl.BlockSpec((SUBLANES, tb), lambda b, s: (0, b * n_s + s)),
        ],
        out_shape=[
            jax.ShapeDtypeStruct((total * TILES_PER_ROW, LANES), jnp.float32),
            jax.ShapeDtypeStruct((SUBLANES, total), jnp.float32),
        ],
        scratch_shapes=[
            pltpu.VMEM((POOL_HIST + tb, D_MODEL), jnp.float32),
            pltpu.VMEM((tb, D_MODEL), jnp.float32),
        ],
        compiler_params=pltpu.CompilerParams(
            dimension_semantics=("arbitrary", "arbitrary"),
            vmem_limit_bytes=VMEM_LIMIT),
        name="pool_layer",
    )(x2d, gain, pw, scale, gain2, wr, br)


def _moe_kernel(src_ref, dst_ref, nv_ref, ea_ref, eb_ref,
                x_hbm, wts_ref, gain_ref, fgain_ref, wgu_a_ref, wd_a_ref, wgu_b_ref, wd_b_ref,
                out_hbm, xbuf, obuf, sem_in, sem_out, *, final_norm, total):
    b = pl.program_id(0)
    n = nv_ref[b]
    n_prev = nv_ref[jnp.maximum(b - 1, 0)]
    xs = b % MOE_XBUFS
    osl = b % MOE_OBUFS

    def tile_of(row):
        return pl.ds(pl.multiple_of(row * TILES_PER_ROW, TILES_PER_ROW), TILES_PER_ROW)

    def start_gather(blk, sl):
        for r in range(MOE_ROWS):
            tok = src_ref[blk * MOE_ROWS + r]
            pltpu.make_async_copy(x_hbm.at[tile_of(tok)], xbuf.at[sl, tile_of(r)],
                                  sem_in.at[sl]).start(priority=r % DMA_THREADS)

    def wait_gather(sl):
        pltpu.make_async_copy(x_hbm.at[pl.ds(0, MOE_ROWS * TILES_PER_ROW)], xbuf.at[sl],
                              sem_in.at[sl]).wait()

    def out_copy(sl, r, tok):
        if final_norm:
            return pltpu.make_async_copy(obuf.at[sl, pl.ds(r, 1)], out_hbm.at[pl.ds(tok, 1)],
                                         sem_out.at[sl])
        return pltpu.make_async_copy(obuf.at[sl, tile_of(r)], out_hbm.at[tile_of(tok)],
                                     sem_out.at[sl])

    def start_scatter(blk, sl):
        for r in range(MOE_ROWS):
            out_copy(sl, r, dst_ref[blk * MOE_ROWS + r]).start(priority=r % DMA_THREADS)

    def wait_scatter(sl):
        pltpu.make_async_copy(obuf.at[sl], out_hbm.at[pl.ds(0, obuf.shape[1])],
                              sem_out.at[sl]).wait()

    @pl.when(b == 0)
    def _():
        obuf[...] = jnp.zeros(obuf.shape, jnp.float32)
        for sl in range(MOE_OBUFS):
            for r in range(MOE_ROWS):
                out_copy(sl, r, total + sl * MOE_ROWS + r).start()
        start_gather(0, 0)
        start_gather(1, 1)

    @pl.when(n > 0)
    def _():
        wait_gather(xs)
        start_gather(b + 2, (b + 2) % MOE_XBUFS)
        xb = _load_rows(xbuf.at[xs], MOE_ROWS)
        hb = _rms(xb, gain_ref[...]).astype(jnp.bfloat16)
        wt = wts_ref[0].T

        def expert(wgu_ref, wd_ref):
            gu = _dot(hb, wgu_ref[0])
            hid = _silu(gu[:, :MOE_D_FF]) * gu[:, MOE_D_FF:]
            return _dot(hid.astype(jnp.bfloat16), wd_ref[0])

        o = (xb + wt[:, 0:1] * expert(wgu_a_ref, wd_a_ref)
             + wt[:, 1:2] * expert(wgu_b_ref, wd_b_ref))
        wait_scatter(osl)
        if final_norm:
            obuf[osl] = _rms(o, fgain_ref[...])
        else:
            _store_rows(obuf.at[osl], o)
        start_scatter(b, osl)

    @pl.when(jnp.logical_and(n == 0, n_prev > 0))
    def _():
        wait_gather(xs)
        wait_gather((b + 1) % MOE_XBUFS)
        for sl in range(MOE_OBUFS):
            wait_scatter(sl)


def _class_tables():
    ea = np.zeros((N_CLASSES,), np.int32)
    eb = np.zeros((N_CLASSES,), np.int32)
    for g in range(MOE_GROUPS):
        c = g * PAIRS_PER_GROUP
        for a in range(MOE_EPG):
            for b in range(a + 1, MOE_EPG):
                ea[c], eb[c] = g * MOE_EPG + a, g * MOE_EPG + b
                c += 1
    return ea, eb


def _moe_layer(x_tiled, route, gain, fgain, wgu, wd, final_norm):
    total = route.shape[1]
    nb = total // MOE_ROWS + N_CLASSES + 1
    lo = route[0].astype(jnp.int32)
    hi = route[1].astype(jnp.int32)
    w_lo, w_hi = route[2], route[3]
    a = lo % MOE_EPG
    bb = hi % MOE_EPG
    cls = (lo // MOE_EPG) * PAIRS_PER_GROUP + (a * (2 * MOE_EPG - 1 - a)) // 2 + (bb - a - 1)
    cls_ids = jnp.arange(N_CLASSES, dtype=jnp.int32)
    counts = jnp.sum((cls[:, None] == cls_ids[None, :]).astype(jnp.int32), axis=0)
    nblk = (counts + MOE_ROWS - 1) // MOE_ROWS
    npad = nblk * MOE_ROWS - counts
    max_pad = MOE_ROWS - 1
    cand_cls = jnp.repeat(cls_ids, max_pad)
    cand_on = jnp.tile(jnp.arange(max_pad, dtype=jnp.int32), N_CLASSES) < jnp.repeat(npad, max_pad)
    n_slots = nb * MOE_ROWS
    n_fill = n_slots - total - N_CLASSES * max_pad
    keys = jnp.concatenate([2 * cls, jnp.where(cand_on, 2 * cand_cls + 1, 2 * N_CLASSES),
                            jnp.full((n_fill,), 2 * N_CLASSES, jnp.int32)])
    n_extra = n_slots - total
    toks = jnp.concatenate([jnp.arange(total, dtype=jnp.int32), jnp.full((n_extra,), -1, jnp.int32)])
    zeros = jnp.zeros((n_extra,), jnp.float32)
    _, tok_s, wlo_s, whi_s = lax.sort(
        (keys, toks, jnp.concatenate([w_lo, zeros]), jnp.concatenate([w_hi, zeros])),
        num_keys=1, is_stable=True)
    is_pad = tok_s < 0
    slot = jnp.arange(n_slots, dtype=jnp.int32)
    dump = total + ((slot // MOE_ROWS) % MOE_OBUFS) * MOE_ROWS + slot % MOE_ROWS
    src_tab = jnp.concatenate([jnp.where(is_pad, 0, tok_s),
                               jnp.zeros((MOE_ROWS,), jnp.int32)])
    dst = jnp.where(is_pad, dump, tok_s)
    nvalid = jnp.sum((~is_pad).reshape(nb, MOE_ROWS).astype(jnp.int32), axis=1)
    wts = jnp.concatenate(
        [wlo_s.reshape(nb, 1, MOE_ROWS), whi_s.reshape(nb, 1, MOE_ROWS),
         jnp.zeros((nb, SUBLANES - 2, MOE_ROWS), jnp.float32)], axis=1)
    blk_end = jnp.cumsum(nblk)
    bidx = jnp.arange(nb, dtype=jnp.int32)
    c_blk = jnp.sum((blk_end[None, :] <= jnp.minimum(bidx, blk_end[-1] - 1)[:, None]).astype(jnp.int32),
                    axis=1)
    ea_tab, eb_tab = _class_tables()
    onehot = (c_blk[:, None] == cls_ids[None, :]).astype(jnp.int32)
    ea = jnp.sum(onehot * jnp.asarray(ea_tab)[None, :], axis=1)
    eb = jnp.sum(onehot * jnp.asarray(eb_tab)[None, :], axis=1)

    tiled_block = (MOE_ROWS * TILES_PER_ROW, LANES)
    out_rows = total + MOE_OBUFS * MOE_ROWS
    if final_norm:
        out_block = (MOE_ROWS, D_MODEL)
        out_shape = (out_rows, D_MODEL)
    else:
        out_block = tiled_block
        out_shape = (out_rows * TILES_PER_ROW, LANES)

    grid_spec = pltpu.PrefetchScalarGridSpec(
        num_scalar_prefetch=5,
        grid=(nb,),
        in_specs=[
            pl.BlockSpec(memory_space=pl.ANY),
            pl.BlockSpec((1, SUBLANES, MOE_ROWS), lambda b, *_: (b, 0, 0)),
            pl.BlockSpec((1, D_MODEL), lambda b, *_: (0, 0)),
            pl.BlockSpec((1, D_MODEL), lambda b, *_: (0, 0)),
            pl.BlockSpec((1, D_MODEL, 2 * MOE_D_FF), lambda b, src, dst, nv, ea, eb: (ea[b], 0, 0)),
            pl.BlockSpec((1, MOE_D_FF, D_MODEL), lambda b, src, dst, nv, ea, eb: (ea[b], 0, 0)),
            pl.BlockSpec((1, D_MODEL, 2 * MOE_D_FF), lambda b, src, dst, nv, ea, eb: (eb[b], 0, 0)),
            pl.BlockSpec((1, MOE_D_FF, D_MODEL), lambda b, src, dst, nv, ea, eb: (eb[b], 0, 0)),
        ],
        out_specs=pl.BlockSpec(memory_space=pl.ANY),
        scratch_shapes=[
            pltpu.VMEM((MOE_XBUFS,) + tiled_block, jnp.float32),
            pltpu.VMEM((MOE_OBUFS,) + out_block, jnp.float32),
            pltpu.SemaphoreType.DMA((MOE_XBUFS,)),
            pltpu.SemaphoreType.DMA((MOE_OBUFS,)),
        ],
    )
    return pl.pallas_call(
        functools.partial(_moe_kernel, final_norm=final_norm, total=total),
        grid_spec=grid_spec,
        out_shape=jax.ShapeDtypeStruct(out_shape, jnp.float32),
        compiler_params=pltpu.CompilerParams(
            dimension_semantics=("arbitrary",),
            vmem_limit_bytes=VMEM_LIMIT),
        name="moe_layer",
    )(src_tab, dst.reshape(-1), nvalid, ea, eb, x_tiled, wts, gain, fgain, wgu, wd, wgu, wd)


def _router_params(w_group, b_group, w_expert, b_expert):
    gap = ROUTE_EXPERT_COL - MOE_GROUPS
    pad = LANES - ROUTE_EXPERT_COL - MOE_EXPERTS
    wr = jnp.concatenate([w_group, jnp.zeros((D_MODEL, gap), jnp.float32), w_expert,
                          jnp.zeros((D_MODEL, pad), jnp.float32)], axis=1)
    br = jnp.concatenate([b_group, jnp.zeros((gap,), jnp.float32), b_expert.reshape(-1),
                          jnp.zeros((pad,), jnp.float32)])[None, :]
    wr_hi = wr.astype(jnp.bfloat16)
    wr_lo = (wr - wr_hi.astype(jnp.float32)).astype(jnp.bfloat16)
    return jnp.concatenate([wr_hi, wr_hi, wr_lo], axis=0), br


def kernel(x, norm_mix, norm_ffn, ret_w_in, ret_w_out, pool_w, pool_scale, moe_w_group, moe_b_group, moe_w_expert, moe_b_expert, moe_w_gate, moe_w_up, moe_w_down, final_norm):
    batch, seq, _ = x.shape
    x2d = x.reshape(batch * seq, D_MODEL)
    bf = jnp.bfloat16
    wgu = jnp.concatenate([moe_w_gate, moe_w_up], axis=-1).astype(bf)
    wd = moe_w_down.astype(bf)
    fgain = final_norm[None, :]

    wr0, br0 = _router_params(moe_w_group[0], moe_b_group[0], moe_w_expert[0], moe_b_expert[0])
    x1, route0 = _retention_layer(x2d, batch, seq, norm_mix[0][None, :], ret_w_in[0].astype(bf),
                                  ret_w_out[0].astype(bf), norm_ffn[0][None, :], wr0, br0)
    x2 = _moe_layer(x1, route0, norm_ffn[0][None, :], fgain, wgu[0], wd[0], final_norm=False)

    wr1, br1 = _router_params(moe_w_group[1], moe_b_group[1], moe_w_expert[1], moe_b_expert[1])
    x3, route1 = _pool_layer(x2, batch, seq, norm_mix[1][None, :], pool_w[0].astype(bf),
                             pool_scale[0][None, :], norm_ffn[1][None, :], wr1, br1)
    out = _moe_layer(x3, route1, norm_ffn[1][None, :], fgain, wgu[1], wd[1], final_norm=True)
    return out[:batch * seq].reshape(batch, seq, D_MODEL)
```

```python
import functools

import numpy as np
import jax
import jax.numpy as jnp
from jax import lax
from jax.experimental import pallas as pl
from jax.experimental.pallas import tpu as pltpu

D_MODEL = 1024
EPS = 1e-6
CHUNK = 64

RET_HEADS = 4
RET_DK = 256
RET_DV = 512
RET_QK_DIM = RET_HEADS * RET_DK
RET_V_DIM = RET_HEADS * RET_DV
ROPE_BASE = 10000.0

POOL_WINDOWS = (2, 4, 8, 16)
POOL_GROUP_DIM = 256
POOL_HIST = 16

MOE_GROUPS = 4
MOE_EPG = 8
MOE_EXPERTS = MOE_GROUPS * MOE_EPG
MOE_D_FF = 256
PAIRS_PER_GROUP = MOE_EPG * (MOE_EPG - 1) // 2
N_CLASSES = MOE_GROUPS * PAIRS_PER_GROUP

LANES = 128
SUBLANES = 8

RET_TILE = 256
POOL_TILE = 512
MOE_ROWS = 128
DISPATCH_TILE = 1024
MOE_OBUFS = 2
DMA_THREADS = 2
VMEM_LIMIT = 56 * 1024 * 1024


def _rms(x, gain):
    ms = jnp.mean(x * x, axis=-1, keepdims=True)
    return x * lax.rsqrt(ms + EPS) * gain


def _silu(x):
    return x * jax.nn.sigmoid(x)


def _dot(a, b):
    return jnp.dot(a, b, preferred_element_type=jnp.float32)


ROUTE_EXPERT_COL = 8
TILES_PER_ROW = D_MODEL // LANES


def _load_rows(ref, rows):
    return jnp.concatenate(
        [ref[pl.ds(s, rows, stride=TILES_PER_ROW), :] for s in range(TILES_PER_ROW)], axis=1)


def _store_rows(ref, val):
    rows = val.shape[0]
    for s in range(TILES_PER_ROW):
        ref[pl.ds(s, rows, stride=TILES_PER_ROW), :] = val[:, s * LANES:(s + 1) * LANES]


def _route(x1, gain2, wr3, br, tri, cnt_ref):
    rows = x1.shape[0]
    h2 = _rms(x1, gain2)
    h_hi = h2.astype(jnp.bfloat16)
    h_lo = (h2 - h_hi.astype(jnp.float32)).astype(jnp.bfloat16)
    logits = _dot(jnp.concatenate([h_hi, h_lo, h_hi], axis=1), wr3) + br
    lt = logits.T
    sub = lax.broadcasted_iota(jnp.int32, (SUBLANES, rows), 0)
    neg = jnp.float32(-jnp.inf)
    gl = jnp.where(sub < MOE_GROUPS, lt[0:SUBLANES], neg)
    gmax = jnp.max(gl, axis=0, keepdims=True)
    grp = jnp.min(jnp.where(gl == gmax, sub, SUBLANES), axis=0, keepdims=True)
    g_gate = 1.0 / jnp.sum(jnp.exp(gl - gmax), axis=0, keepdims=True)
    el = lt[ROUTE_EXPERT_COL:ROUTE_EXPERT_COL + MOE_EPG]
    for g in range(1, MOE_GROUPS):
        first = ROUTE_EXPERT_COL + g * MOE_EPG
        el = jnp.where(grp == g, lt[first:first + MOE_EPG], el)
    m1 = jnp.max(el, axis=0, keepdims=True)
    i1 = jnp.min(jnp.where(el == m1, sub, SUBLANES), axis=0, keepdims=True)
    el2 = jnp.where(sub == i1, neg, el)
    m2 = jnp.max(el2, axis=0, keepdims=True)
    i2 = jnp.min(jnp.where(el2 == m2, sub, SUBLANES), axis=0, keepdims=True)
    t = jnp.exp(m2 - m1)
    w1 = g_gate / (1.0 + t)
    w2 = g_gate * t / (1.0 + t)
    first_lo = i1 < i2
    a = jnp.where(first_lo, i1, i2)
    bb = jnp.where(first_lo, i2, i1)
    base = grp * MOE_EPG
    lo = (base + a).astype(jnp.float32)
    hi = (base + bb).astype(jnp.float32)
    w_lo = jnp.where(first_lo, w1, w2)
    w_hi = jnp.where(first_lo, w2, w1)
    cls = grp * PAIRS_PER_GROUP + ((a * (2 * MOE_EPG - 1 - a)) >> 1) + (bb - a - 1)
    onehot = (lax.broadcasted_iota(jnp.int32, (LANES, rows), 0) == cls).astype(jnp.float32)
    earlier = _dot(onehot.astype(jnp.bfloat16), tri) + cnt_ref[...]
    rank = jnp.sum(onehot * earlier, axis=0, keepdims=True)
    cnt_ref[...] += jnp.sum(onehot, axis=1, keepdims=True)
    out = jnp.where(sub == 0, lo, jnp.where(sub == 1, hi, jnp.where(sub == 2, w_lo,
                    jnp.where(sub == 3, w_hi, 0.0))))
    return jnp.where(sub == 4, rank, jnp.where(sub == 5, cls.astype(jnp.float32), out))


def _ret_kernel(x_ref, gain_ref, win_ref, wout_ref, cos_ref, sin_ref, dmat_ref,
                qdec_ref, kdec_ref, cdec_ref, gain2_ref, wr_ref, br_ref, tri_ref,
                x1_ref, route_ref, state_ref, gated_ref, cnt_ref):
    @pl.when(pl.program_id(1) == 0)
    def _():
        state_ref[...] = jnp.zeros_like(state_ref)

    @pl.when(jnp.logical_and(pl.program_id(0) == 0, pl.program_id(1) == 0))
    def _():
        cnt_ref[...] = jnp.zeros_like(cnt_ref)

    x = x_ref[...]
    h = _rms(x, gain_ref[...]).astype(jnp.bfloat16)
    cos = cos_ref[...]
    sin = sin_ref[...]
    half = RET_DK // 2

    def rope(t):
        t1, t2 = t[:, :half], t[:, half:]
        return jnp.concatenate([t1 * cos - t2 * sin, t1 * sin + t2 * cos], axis=1)

    for hd in range(RET_HEADS):
        q = _dot(h, win_ref[:, hd * RET_DK:(hd + 1) * RET_DK])
        k = _dot(h, win_ref[:, RET_QK_DIM + hd * RET_DK:RET_QK_DIM + (hd + 1) * RET_DK])
        v0 = 2 * RET_QK_DIM + hd * RET_DV
        v = _dot(h, win_ref[:, v0:v0 + RET_DV])
        g0 = 2 * RET_QK_DIM + RET_V_DIM + hd * RET_DV
        g = _dot(h, win_ref[:, g0:g0 + RET_DV])
        q = rope(q)
        k = rope(k) * (RET_DK ** -0.5)
        qb = q.astype(jnp.bfloat16)
        kb = k.astype(jnp.bfloat16)
        vb = v.astype(jnp.bfloat16)
        sc = lax.dot_general(qb, kb, (((1,), (1,)), ((), ())),
                             preferred_element_type=jnp.float32)
        sc = sc * dmat_ref[hd]
        intra = _dot(sc.astype(jnp.bfloat16), vb)
        st = state_ref[hd]
        cross = _dot(qb, st.astype(jnp.bfloat16)) * qdec_ref[hd]
        o = intra + cross
        kd = (k * kdec_ref[hd]).astype(jnp.bfloat16)
        upd = lax.dot_general(kd, vb, (((0,), (0,)), ((), ())),
                              preferred_element_type=jnp.float32)
        state_ref[hd] = st * cdec_ref[hd] + upd
        mu = jnp.mean(o, axis=-1, keepdims=True)
        oc = o - mu
        var = jnp.mean(oc * oc, axis=-1, keepdims=True)
        on = oc * lax.rsqrt(var + EPS)
        gated_ref[:, hd * RET_DV:(hd + 1) * RET_DV] = (_silu(g) * on).astype(jnp.bfloat16)

    x1 = x + _dot(gated_ref[...], wout_ref[...])
    _store_rows(x1_ref, x1)
    route_ref[...] = _route(x1, gain2_ref[...], wr_ref[...], br_ref[...], tri_ref[...], cnt_ref)


def _tri(n):
    i = jnp.arange(n)
    return (i[:, None] < i[None, :]).astype(jnp.bfloat16)


def _const_spec(shape):
    nd = len(shape)
    return pl.BlockSpec(shape, lambda *_: (0,) * nd, pipeline_mode=pl.Buffered(1))


def _retention_layer(x2d, batch, seq, gain, w_in, w_out, gain2, wr, br):
    tb = RET_TILE
    n_s = seq // tb
    total = batch * seq
    log_g = jnp.log(1.0 - jnp.exp2(-5.0 - jnp.arange(RET_HEADS, dtype=jnp.float32)))
    n = jnp.arange(tb, dtype=jnp.float32)
    diff = n[:, None] - n[None, :]
    cn = jnp.arange(tb)[:, None] // CHUNK
    cm = jnp.arange(tb)[None, :] // CHUNK
    expo = jnp.where(cn == cm, jnp.abs(diff), diff)
    dmat = jnp.where((cm <= cn)[None], jnp.exp(log_g[:, None, None] * expo[None]), 0.0)
    qdec = jnp.exp(log_g[:, None] * (n + 1.0)[None])[:, :, None]
    kdec = jnp.exp(log_g[:, None] * (tb - 1.0 - n)[None])[:, :, None]
    cdec = jnp.broadcast_to(jnp.exp(log_g * tb)[:, None, None], (RET_HEADS, 1, RET_DV))
    half = RET_DK // 2
    inv_freq = 1.0 / (ROPE_BASE ** (jnp.arange(half, dtype=jnp.float32) / half))
    ang = jnp.arange(seq, dtype=jnp.float32)[:, None] * inv_freq[None, :]
    cos, sin = jnp.cos(ang), jnp.sin(ang)

    tile = lambda b, s: (b * n_s + s, 0)
    return pl.pallas_call(
        _ret_kernel,
        grid=(batch, n_s),
        in_specs=[
            pl.BlockSpec((tb, D_MODEL), tile),
            _const_spec((1, D_MODEL)),
            _const_spec(w_in.shape),
            _const_spec(w_out.shape),
            pl.BlockSpec((tb, half), lambda b, s: (s, 0)),
            pl.BlockSpec((tb, half), lambda b, s: (s, 0)),
            _const_spec(dmat.shape),
            _const_spec(qdec.shape),
            _const_spec(kdec.shape),
            _const_spec(cdec.shape),
            _const_spec((1, D_MODEL)),
            _const_spec(wr.shape),
            _const_spec(br.shape),
            _const_spec((tb, tb)),
        ],
        out_specs=[
            pl.BlockSpec((tb * TILES_PER_ROW, LANES), tile),
            pl.BlockSpec((SUBLANES, tb), lambda b, s: (0, b * n_s + s)),
        ],
        out_shape=[
            jax.ShapeDtypeStruct((total * TILES_PER_ROW, LANES), jnp.float32),
            jax.ShapeDtypeStruct((SUBLANES, total), jnp.float32),
        ],
        scratch_shapes=[
            pltpu.VMEM((RET_HEADS, RET_DK, RET_DV), jnp.float32),
            pltpu.VMEM((tb, RET_V_DIM), jnp.bfloat16),
            pltpu.VMEM((LANES, 1), jnp.float32),
        ],
        compiler_params=pltpu.CompilerParams(
            dimension_semantics=("arbitrary", "arbitrary"),
            vmem_limit_bytes=VMEM_LIMIT),
        name="retention_layer",
    )(x2d, gain, w_in, w_out, cos, sin, dmat, qdec, kdec, cdec, gain2, wr, br, _tri(tb))


def _pool_kernel(x_ref, gain_ref, pw_ref, scale_ref, gain2_ref, wr_ref, br_ref, tri_ref,
                 x1_ref, route_ref, ext_ref, y_ref, cnt_ref):
    s = pl.program_id(1)
    tb = x_ref.shape[0] // TILES_PER_ROW

    @pl.when(s == 0)
    def _():
        ext_ref[0:POOL_HIST, :] = jnp.zeros((POOL_HIST, D_MODEL), jnp.float32)

    @pl.when(jnp.logical_and(pl.program_id(0) == 0, s == 0))
    def _():
        cnt_ref[...] = jnp.zeros_like(cnt_ref)

    x = _load_rows(x_ref, tb)
    ext_ref[POOL_HIST:, :] = _rms(x, gain_ref[...])
    pos = s * tb + lax.broadcasted_iota(jnp.int32, (tb, 1), 0)
    for g, w in enumerate(POOL_WINDOWS):
        cs = slice(g * POOL_GROUP_DIM, (g + 1) * POOL_GROUP_DIM)
        e = ext_ref[:, cs]
        acc = e
        k = 1
        while k < w:
            acc = acc + pltpu.roll(acc, k, axis=0)
            k *= 2
        inv_cnt = 1.0 / jnp.minimum(pos + 1, w).astype(jnp.float32)
        pooled = acc[POOL_HIST:, :] * inv_cnt - e[POOL_HIST:, :]
        y_ref[:, cs] = _dot(pooled.astype(jnp.bfloat16), pw_ref[g])
    ext_ref[0:POOL_HIST, :] = ext_ref[tb:tb + POOL_HIST, :]
    x1 = x + y_ref[...] * scale_ref[...]
    _store_rows(x1_ref, x1)
    route_ref[...] = _route(x1, gain2_ref[...], wr_ref[...], br_ref[...], tri_ref[...], cnt_ref)


def _pool_layer(x2d, batch, seq, gain, pw, scale, gain2, wr, br):
    tb = POOL_TILE
    n_s = seq // tb
    total = batch * seq
    tile = lambda b, s: (b * n_s + s, 0)
    return pl.pallas_call(
        _pool_kernel,
        grid=(batch, n_s),
        in_specs=[
            pl.BlockSpec((tb * TILES_PER_ROW, LANES), tile),
            _const_spec((1, D_MODEL)),
            _const_spec(pw.shape),
            _const_spec((1, D_MODEL)),
            _const_spec((1, D_MODEL)),
            _const_spec(wr.shape),
            _const_spec(br.shape),
            _const_spec((tb, tb)),
        ],
        out_specs=[
            pl.BlockSpec((tb * TILES_PER_ROW, LANES), tile),
            pl.BlockSpec((SUBLANES, tb), lambda b, s: (0, b * n_s + s)),
        ],
        out_shape=[
            jax.ShapeDtypeStruct((total * TILES_PER_ROW, LANES), jnp.float32),
            jax.ShapeDtypeStruct((SUBLANES, total), jnp.float32),
        ],
        scratch_shapes=[
            pltpu.VMEM((POOL_HIST + tb, D_MODEL), jnp.float32),
            pltpu.VMEM((tb, D_MODEL), jnp.float32),
            pltpu.VMEM((LANES, 1), jnp.float32),
        ],
        compiler_params=pltpu.CompilerParams(
            dimension_semantics=("arbitrary", "arbitrary"),
            vmem_limit_bytes=VMEM_LIMIT),
        name="pool_layer",
    )(x2d, gain, pw, scale, gain2, wr, br, _tri(tb))


def _tile_of(row):
    return pl.ds(pl.multiple_of(row * TILES_PER_ROW, TILES_PER_ROW), TILES_PER_ROW)


def _dispatch_kernel(slot_ref, run_start_ref, run_len_ref, n_used_ref,
                     x_ref, out_hbm, zero_ref, sem, sem_pad):
    i = pl.program_id(0)
    tile = x_ref.shape[0] // TILES_PER_ROW

    @pl.when(i == 0)
    def _():
        zero_ref[...] = jnp.zeros_like(zero_ref)

        def pad_copy(c):
            rows = pl.multiple_of(run_len_ref[c] * TILES_PER_ROW, TILES_PER_ROW)
            first = pl.multiple_of(run_start_ref[c] * TILES_PER_ROW, TILES_PER_ROW)
            return pltpu.make_async_copy(zero_ref.at[pl.ds(0, rows)], out_hbm.at[pl.ds(first, rows)],
                                         sem_pad.at[0])

        def block_copy(blk):
            first = pl.multiple_of(blk * zero_ref.shape[0], zero_ref.shape[0])
            return pltpu.make_async_copy(zero_ref, out_hbm.at[pl.ds(first, zero_ref.shape[0])],
                                         sem_pad.at[0])

        n_blocks = out_hbm.shape[0] // zero_ref.shape[0]
        for c in range(N_CLASSES):
            @pl.when(run_len_ref[c] > 0)
            def _():
                pad_copy(c).start()
        lax.fori_loop(n_used_ref[0], n_blocks, lambda blk, c: (block_copy(blk).start(), c)[1], 0)
        for c in range(N_CLASSES):
            @pl.when(run_len_ref[c] > 0)
            def _():
                pad_copy(c).wait()
        lax.fori_loop(n_used_ref[0], n_blocks, lambda blk, c: (block_copy(blk).wait(), c)[1], 0)

    for r in range(tile):
        slot = slot_ref[i * tile + r]
        pltpu.make_async_copy(x_ref.at[_tile_of(r)], out_hbm.at[_tile_of(slot)],
                              sem.at[0]).start(priority=r % DMA_THREADS)
    pltpu.make_async_copy(x_ref, out_hbm.at[pl.ds(0, x_ref.shape[0])], sem.at[0]).wait()


def _dispatch(x_tiled, slot_of_tok, run_start, run_len, n_used, n_slots):
    total = slot_of_tok.shape[0]
    tile = DISPATCH_TILE
    grid_spec = pltpu.PrefetchScalarGridSpec(
        num_scalar_prefetch=4,
        grid=(total // tile,),
        in_specs=[pl.BlockSpec((tile * TILES_PER_ROW, LANES), lambda i, *_: (i, 0))],
        out_specs=pl.BlockSpec(memory_space=pl.ANY),
        scratch_shapes=[
            pltpu.VMEM((MOE_ROWS * TILES_PER_ROW, LANES), jnp.float32),
            pltpu.SemaphoreType.DMA((1,)),
            pltpu.SemaphoreType.DMA((1,)),
        ],
    )
    return pl.pallas_call(
        _dispatch_kernel,
        grid_spec=grid_spec,
        out_shape=jax.ShapeDtypeStruct((n_slots * TILES_PER_ROW, LANES), jnp.float32),
        compiler_params=pltpu.CompilerParams(
            dimension_semantics=("arbitrary",),
            vmem_limit_bytes=VMEM_LIMIT),
        name="moe_dispatch",
    )(slot_of_tok, run_start, run_len, n_used, x_tiled)


def _moe_kernel(dst_ref, nv_ref, ea_ref, eb_ref, xblk_ref,
                x_ref, wts_ref, gain_ref, fgain_ref, wgu_a_ref, wd_a_ref, wgu_b_ref, wd_b_ref,
                out_hbm, obuf, sem_out, *, final_norm, total):
    b = pl.program_id(0)
    n = nv_ref[b]
    n_prev = nv_ref[jnp.maximum(b - 1, 0)]
    osl = b % MOE_OBUFS

    def out_copy(sl, r, tok):
        if final_norm:
            return pltpu.make_async_copy(obuf.at[sl, pl.ds(r, 1)], out_hbm.at[pl.ds(tok, 1)],
                                         sem_out.at[sl])
        return pltpu.make_async_copy(obuf.at[sl, _tile_of(r)], out_hbm.at[_tile_of(tok)],
                                     sem_out.at[sl])

    def start_scatter(blk, sl):
        for r in range(MOE_ROWS):
            out_copy(sl, r, dst_ref[blk * MOE_ROWS + r]).start(priority=r % DMA_THREADS)

    def wait_scatter(sl):
        pltpu.make_async_copy(obuf.at[sl], out_hbm.at[pl.ds(0, obuf.shape[1])],
                              sem_out.at[sl]).wait()

    @pl.when(b == 0)
    def _():
        obuf[...] = jnp.zeros(obuf.shape, jnp.float32)
        for sl in range(MOE_OBUFS):
            for r in range(MOE_ROWS):
                out_copy(sl, r, total + sl * MOE_ROWS + r).start()

    @pl.when(n > 0)
    def _():
        xb = _load_rows(x_ref, MOE_ROWS)
        hb = _rms(xb, gain_ref[...]).astype(jnp.bfloat16)
        wt = wts_ref[0].T

        def expert(wgu_ref, wd_ref):
            gu = _dot(hb, wgu_ref[0])
            hid = _silu(gu[:, :MOE_D_FF]) * gu[:, MOE_D_FF:]
            return _dot(hid.astype(jnp.bfloat16), wd_ref[0])

        o = (xb + wt[:, 0:1] * expert(wgu_a_ref, wd_a_ref)
             + wt[:, 1:2] * expert(wgu_b_ref, wd_b_ref))
        wait_scatter(osl)
        if final_norm:
            obuf[osl] = _rms(o, fgain_ref[...])
        else:
            _store_rows(obuf.at[osl], o)
        start_scatter(b, osl)

    @pl.when(jnp.logical_and(n == 0, n_prev > 0))
    def _():
        for sl in range(MOE_OBUFS):
            wait_scatter(sl)


def _class_tables():
    ea = np.zeros((N_CLASSES,), np.int32)
    eb = np.zeros((N_CLASSES,), np.int32)
    for g in range(MOE_GROUPS):
        c = g * PAIRS_PER_GROUP
        for a in range(MOE_EPG):
            for b in range(a + 1, MOE_EPG):
                ea[c], eb[c] = g * MOE_EPG + a, g * MOE_EPG + b
                c += 1
    return ea, eb


def _moe_layer(x_tiled, route, gain, fgain, wgu, wd, final_norm):
    total = route.shape[1]
    nb = total // MOE_ROWS + N_CLASSES + 1
    n_slots = nb * MOE_ROWS
    w_lo, w_hi = route[2], route[3]
    rank = route[4].astype(jnp.int32)
    cls = route[5].astype(jnp.int32)
    cls_ids = jnp.arange(N_CLASSES, dtype=jnp.int32)
    onehot_tok = (cls[:, None] == cls_ids[None, :]).astype(jnp.int32)
    counts = jnp.sum(onehot_tok, axis=0)
    nblk = (counts + MOE_ROWS - 1) // MOE_ROWS
    npad = nblk * MOE_ROWS - counts
    blk_end = jnp.cumsum(nblk)
    cls_start = (blk_end - nblk) * MOE_ROWS
    slot_of_tok = jnp.sum(onehot_tok * cls_start[None, :], axis=1) + rank
    x_sorted = _dispatch(x_tiled, slot_of_tok, cls_start + counts, npad, blk_end[-1:], n_slots)

    max_pad = MOE_ROWS - 1
    cand_cls = jnp.repeat(cls_ids, max_pad)
    cand_on = jnp.tile(jnp.arange(max_pad, dtype=jnp.int32), N_CLASSES) < jnp.repeat(npad, max_pad)
    n_fill = n_slots - total - N_CLASSES * max_pad
    keys = jnp.concatenate([2 * cls, jnp.where(cand_on, 2 * cand_cls + 1, 2 * N_CLASSES),
                            jnp.full((n_fill,), 2 * N_CLASSES, jnp.int32)])
    n_extra = n_slots - total
    toks = jnp.concatenate([jnp.arange(total, dtype=jnp.int32), jnp.full((n_extra,), -1, jnp.int32)])
    zeros = jnp.zeros((n_extra,), jnp.float32)
    _, tok_s, wlo_s, whi_s = lax.sort(
        (keys, toks, jnp.concatenate([w_lo, zeros]), jnp.concatenate([w_hi, zeros])),
        num_keys=1, is_stable=True)
    is_pad = tok_s < 0
    slot = jnp.arange(n_slots, dtype=jnp.int32)
    dump = total + ((slot // MOE_ROWS) % MOE_OBUFS) * MOE_ROWS + slot % MOE_ROWS
    dst = jnp.where(is_pad, dump, tok_s)
    nvalid = jnp.sum((~is_pad).reshape(nb, MOE_ROWS).astype(jnp.int32), axis=1)
    wts = jnp.concatenate(
        [wlo_s.reshape(nb, 1, MOE_ROWS), whi_s.reshape(nb, 1, MOE_ROWS),
         jnp.zeros((nb, SUBLANES - 2, MOE_ROWS), jnp.float32)], axis=1)
    bidx = jnp.arange(nb, dtype=jnp.int32)
    xblk = jnp.minimum(bidx, blk_end[-1] - 1)
    c_blk = jnp.sum((blk_end[None, :] <= xblk[:, None]).astype(jnp.int32), axis=1)
    ea_tab, eb_tab = _class_tables()
    onehot_blk = (c_blk[:, None] == cls_ids[None, :]).astype(jnp.int32)
    ea = jnp.sum(onehot_blk * jnp.asarray(ea_tab)[None, :], axis=1)
    eb = jnp.sum(onehot_blk * jnp.asarray(eb_tab)[None, :], axis=1)

    tiled_block = (MOE_ROWS * TILES_PER_ROW, LANES)
    out_rows = total + MOE_OBUFS * MOE_ROWS
    if final_norm:
        out_block = (MOE_ROWS, D_MODEL)
        out_shape = (out_rows, D_MODEL)
    else:
        out_block = tiled_block
        out_shape = (out_rows * TILES_PER_ROW, LANES)

    grid_spec = pltpu.PrefetchScalarGridSpec(
        num_scalar_prefetch=5,
        grid=(nb,),
        in_specs=[
            pl.BlockSpec(tiled_block, lambda b, dst, nv, ea, eb, xblk: (xblk[b], 0)),
            pl.BlockSpec((1, SUBLANES, MOE_ROWS), lambda b, *_: (b, 0, 0)),
            pl.BlockSpec((1, D_MODEL), lambda b, *_: (0, 0)),
            pl.BlockSpec((1, D_MODEL), lambda b, *_: (0, 0)),
            pl.BlockSpec((1, D_MODEL, 2 * MOE_D_FF), lambda b, dst, nv, ea, eb, xblk: (ea[b], 0, 0)),
            pl.BlockSpec((1, MOE_D_FF, D_MODEL), lambda b, dst, nv, ea, eb, xblk: (ea[b], 0, 0)),
            pl.BlockSpec((1, D_MODEL, 2 * MOE_D_FF), lambda b, dst, nv, ea, eb, xblk: (eb[b], 0, 0)),
            pl.BlockSpec((1, MOE_D_FF, D_MODEL), lambda b, dst, nv, ea, eb, xblk: (eb[b], 0, 0)),
        ],
        out_specs=pl.BlockSpec(memory_space=pl.ANY),
        scratch_shapes=[
            pltpu.VMEM((MOE_OBUFS,) + out_block, jnp.float32),
            pltpu.SemaphoreType.DMA((MOE_OBUFS,)),
        ],
    )
    return pl.pallas_call(
        functools.partial(_moe_kernel, final_norm=final_norm, total=total),
        grid_spec=grid_spec,
        out_shape=jax.ShapeDtypeStruct(out_shape, jnp.float32),
        compiler_params=pltpu.CompilerParams(
            dimension_semantics=("arbitrary",),
            vmem_limit_bytes=VMEM_LIMIT),
        name="moe_layer",
    )(dst, nvalid, ea, eb, xblk, x_sorted, wts, gain, fgain, wgu, wd, wgu, wd)


def _router_params(w_group, b_group, w_expert, b_expert):
    gap = ROUTE_EXPERT_COL - MOE_GROUPS
    pad = LANES - ROUTE_EXPERT_COL - MOE_EXPERTS
    wr = jnp.concatenate([w_group, jnp.zeros((D_MODEL, gap), jnp.float32), w_expert,
                          jnp.zeros((D_MODEL, pad), jnp.float32)], axis=1)
    br = jnp.concatenate([b_group, jnp.zeros((gap,), jnp.float32), b_expert.reshape(-1),
                          jnp.zeros((pad,), jnp.float32)])[None, :]
    wr_hi = wr.astype(jnp.bfloat16)
    wr_lo = (wr - wr_hi.astype(jnp.float32)).astype(jnp.bfloat16)
    return jnp.concatenate([wr_hi, wr_hi, wr_lo], axis=0), br


def kernel(x, norm_mix, norm_ffn, ret_w_in, ret_w_out, pool_w, pool_scale, moe_w_group, moe_b_group, moe_w_expert, moe_b_expert, moe_w_gate, moe_w_up, moe_w_down, final_norm):
    batch, seq, _ = x.shape
    x2d = x.reshape(batch * seq, D_MODEL)
    bf = jnp.bfloat16
    wgu = jnp.concatenate([moe_w_gate, moe_w_up], axis=-1).astype(bf)
    wd = moe_w_down.astype(bf)
    fgain = final_norm[None, :]

    wr0, br0 = _router_params(moe_w_group[0], moe_b_group[0], moe_w_expert[0], moe_b_expert[0])
    x1, route0 = _retention_layer(x2d, batch, seq, norm_mix[0][None, :], ret_w_in[0].astype(bf),
                                  ret_w_out[0].astype(bf), norm_ffn[0][None, :], wr0, br0)
    x2 = _moe_layer(x1, route0, norm_ffn[0][None, :], fgain, wgu[0], wd[0], final_norm=False)

    wr1, br1 = _router_params(moe_w_group[1], moe_b_group[1], moe_w_expert[1], moe_b_expert[1])
    x3, route1 = _pool_layer(x2, batch, seq, norm_mix[1][None, :], pool_w[0].astype(bf),
                             pool_scale[0][None, :], norm_ffn[1][None, :], wr1, br1)
    out = _moe_layer(x3, route1, norm_ffn[1][None, :], fgain, wgu[1], wd[1], final_norm=True)
    return out[:batch * seq].reshape(batch, seq, D_MODEL)
```

```python
import functools

import numpy as np
import jax
import jax.numpy as jnp
from jax import lax
from jax.experimental import pallas as pl
from jax.experimental.pallas import tpu as pltpu

D_MODEL = 1024
EPS = 1e-6
CHUNK = 64

RET_HEADS = 4
RET_DK = 256
RET_DV = 512
RET_QK_DIM = RET_HEADS * RET_DK
RET_V_DIM = RET_HEADS * RET_DV
ROPE_BASE = 10000.0

POOL_WINDOWS = (2, 4, 8, 16)
POOL_GROUP_DIM = 256
POOL_HIST = 16

MOE_GROUPS = 4
MOE_EPG = 8
MOE_EXPERTS = MOE_GROUPS * MOE_EPG
MOE_D_FF = 256
PAIRS_PER_GROUP = MOE_EPG * (MOE_EPG - 1) // 2
N_CLASSES = MOE_GROUPS * PAIRS_PER_GROUP

LANES = 128
SUBLANES = 8

RET_TILE = 256
RET_BLOCK = 256
POOL_TILE = 512
MOE_ROWS = 128
DISPATCH_TILE = 1024
MOE_OBUFS = 2
DMA_THREADS = 2
VMEM_LIMIT = 56 * 1024 * 1024


def _rms(x, gain):
    ms = jnp.mean(x * x, axis=-1, keepdims=True)
    return x * lax.rsqrt(ms + EPS) * gain


def _silu(x):
    return x * jax.nn.sigmoid(x)


def _dot(a, b):
    return jnp.dot(a, b, preferred_element_type=jnp.float32)


ROUTE_EXPERT_COL = 8
TILES_PER_ROW = D_MODEL // LANES


def _load_rows(ref, rows):
    return jnp.concatenate(
        [ref[pl.ds(s, rows, stride=TILES_PER_ROW), :] for s in range(TILES_PER_ROW)], axis=1)


def _store_rows(ref, val):
    rows = val.shape[0]
    for s in range(TILES_PER_ROW):
        ref[pl.ds(s, rows, stride=TILES_PER_ROW), :] = val[:, s * LANES:(s + 1) * LANES]


def _route(x1, gain2, wr3, br, tri, cnt_ref):
    rows = x1.shape[0]
    h2 = _rms(x1, gain2)
    h_hi = h2.astype(jnp.bfloat16)
    h_lo = (h2 - h_hi.astype(jnp.float32)).astype(jnp.bfloat16)
    logits = _dot(jnp.concatenate([h_hi, h_lo, h_hi], axis=1), wr3) + br
    lt = logits.T
    sub = lax.broadcasted_iota(jnp.int32, (SUBLANES, rows), 0)
    neg = jnp.float32(-jnp.inf)
    gl = jnp.where(sub < MOE_GROUPS, lt[0:SUBLANES], neg)
    gmax = jnp.max(gl, axis=0, keepdims=True)
    grp = jnp.min(jnp.where(gl == gmax, sub, SUBLANES), axis=0, keepdims=True)
    g_gate = 1.0 / jnp.sum(jnp.exp(gl - gmax), axis=0, keepdims=True)
    el = lt[ROUTE_EXPERT_COL:ROUTE_EXPERT_COL + MOE_EPG]
    for g in range(1, MOE_GROUPS):
        first = ROUTE_EXPERT_COL + g * MOE_EPG
        el = jnp.where(grp == g, lt[first:first + MOE_EPG], el)
    m1 = jnp.max(el, axis=0, keepdims=True)
    i1 = jnp.min(jnp.where(el == m1, sub, SUBLANES), axis=0, keepdims=True)
    el2 = jnp.where(sub == i1, neg, el)
    m2 = jnp.max(el2, axis=0, keepdims=True)
    i2 = jnp.min(jnp.where(el2 == m2, sub, SUBLANES), axis=0, keepdims=True)
    t = jnp.exp(m2 - m1)
    w1 = g_gate / (1.0 + t)
    w2 = g_gate * t / (1.0 + t)
    first_lo = i1 < i2
    a = jnp.where(first_lo, i1, i2)
    bb = jnp.where(first_lo, i2, i1)
    base = grp * MOE_EPG
    lo = (base + a).astype(jnp.float32)
    hi = (base + bb).astype(jnp.float32)
    w_lo = jnp.where(first_lo, w1, w2)
    w_hi = jnp.where(first_lo, w2, w1)
    cls = grp * PAIRS_PER_GROUP + ((a * (2 * MOE_EPG - 1 - a)) >> 1) + (bb - a - 1)
    onehot = (lax.broadcasted_iota(jnp.int32, (LANES, rows), 0) == cls).astype(jnp.float32)
    earlier = _dot(onehot.astype(jnp.bfloat16), tri) + cnt_ref[...]
    rank = jnp.sum(onehot * earlier, axis=0, keepdims=True)
    cnt_ref[...] += jnp.sum(onehot, axis=1, keepdims=True)
    out = jnp.where(sub == 0, lo, jnp.where(sub == 1, hi, jnp.where(sub == 2, w_lo,
                    jnp.where(sub == 3, w_hi, 0.0))))
    return jnp.where(sub == 4, rank, jnp.where(sub == 5, cls.astype(jnp.float32), out))


def _ret_kernel(x_ref, gain_ref, win_ref, wout_ref, cos_ref, sin_ref, dmat_ref,
                qdec_ref, kdec_ref, cdec_ref, gain2_ref, wr_ref, br_ref, tri_ref,
                x1_ref, route_ref, state_ref, gated_ref, cnt_ref):
    @pl.when(pl.program_id(1) == 0)
    def _():
        state_ref[...] = jnp.zeros_like(state_ref)

    @pl.when(jnp.logical_and(pl.program_id(0) == 0, pl.program_id(1) == 0))
    def _():
        cnt_ref[...] = jnp.zeros_like(cnt_ref)

    x = x_ref[...]
    h = _rms(x, gain_ref[...]).astype(jnp.bfloat16)
    cos = cos_ref[...]
    sin = sin_ref[...]
    half = RET_DK // 2

    def rope(t):
        t1, t2 = t[:, :half], t[:, half:]
        return jnp.concatenate([t1 * cos - t2 * sin, t1 * sin + t2 * cos], axis=1)

    for hd in range(RET_HEADS):
        q = _dot(h, win_ref[:, hd * RET_DK:(hd + 1) * RET_DK])
        k = _dot(h, win_ref[:, RET_QK_DIM + hd * RET_DK:RET_QK_DIM + (hd + 1) * RET_DK])
        v0 = 2 * RET_QK_DIM + hd * RET_DV
        v = _dot(h, win_ref[:, v0:v0 + RET_DV])
        g0 = 2 * RET_QK_DIM + RET_V_DIM + hd * RET_DV
        g = _dot(h, win_ref[:, g0:g0 + RET_DV])
        q = rope(q)
        k = rope(k) * (RET_DK ** -0.5)
        for blk in range(x.shape[0] // RET_BLOCK):
            rs = slice(blk * RET_BLOCK, (blk + 1) * RET_BLOCK)
            qb = q[rs].astype(jnp.bfloat16)
            kb = k[rs].astype(jnp.bfloat16)
            vb = v[rs].astype(jnp.bfloat16)
            sc = lax.dot_general(qb, kb, (((1,), (1,)), ((), ())),
                                 preferred_element_type=jnp.float32)
            sc = sc * dmat_ref[hd]
            intra = _dot(sc.astype(jnp.bfloat16), vb)
            st = state_ref[hd]
            cross = _dot(qb, st.astype(jnp.bfloat16)) * qdec_ref[hd]
            o = intra + cross
            kd = (k[rs] * kdec_ref[hd]).astype(jnp.bfloat16)
            upd = lax.dot_general(kd, vb, (((0,), (0,)), ((), ())),
                                  preferred_element_type=jnp.float32)
            state_ref[hd] = st * cdec_ref[hd] + upd
            mu = jnp.mean(o, axis=-1, keepdims=True)
            oc = o - mu
            var = jnp.mean(oc * oc, axis=-1, keepdims=True)
            on = oc * lax.rsqrt(var + EPS)
            gated_ref[rs, hd * RET_DV:(hd + 1) * RET_DV] = (_silu(g[rs]) * on).astype(jnp.bfloat16)

    x1 = x + _dot(gated_ref[...], wout_ref[...])
    _store_rows(x1_ref, x1)
    route_ref[...] = _route(x1, gain2_ref[...], wr_ref[...], br_ref[...], tri_ref[...], cnt_ref)


def _tri(n):
    i = jnp.arange(n)
    return (i[:, None] < i[None, :]).astype(jnp.bfloat16)


def _const_spec(shape):
    nd = len(shape)
    return pl.BlockSpec(shape, lambda *_: (0,) * nd, pipeline_mode=pl.Buffered(1))


def _retention_layer(x2d, batch, seq, gain, w_in, w_out, gain2, wr, br):
    tb = RET_TILE
    n_s = seq // tb
    total = batch * seq
    rb = RET_BLOCK
    log_g = jnp.log(1.0 - jnp.exp2(-5.0 - jnp.arange(RET_HEADS, dtype=jnp.float32)))
    n = jnp.arange(rb, dtype=jnp.float32)
    diff = n[:, None] - n[None, :]
    cn = jnp.arange(rb)[:, None] // CHUNK
    cm = jnp.arange(rb)[None, :] // CHUNK
    expo = jnp.where(cn == cm, jnp.abs(diff), diff)
    dmat = jnp.where((cm <= cn)[None], jnp.exp(log_g[:, None, None] * expo[None]), 0.0)
    qdec = jnp.exp(log_g[:, None] * (n + 1.0)[None])[:, :, None]
    kdec = jnp.exp(log_g[:, None] * (rb - 1.0 - n)[None])[:, :, None]
    cdec = jnp.broadcast_to(jnp.exp(log_g * rb)[:, None, None], (RET_HEADS, 1, RET_DV))
    half = RET_DK // 2
    inv_freq = 1.0 / (ROPE_BASE ** (jnp.arange(half, dtype=jnp.float32) / half))
    ang = jnp.arange(seq, dtype=jnp.float32)[:, None] * inv_freq[None, :]
    cos, sin = jnp.cos(ang), jnp.sin(ang)

    tile = lambda b, s: (b * n_s + s, 0)
    return pl.pallas_call(
        _ret_kernel,
        grid=(batch, n_s),
        in_specs=[
            pl.BlockSpec((tb, D_MODEL), tile),
            _const_spec((1, D_MODEL)),
            _const_spec(w_in.shape),
            _const_spec(w_out.shape),
            pl.BlockSpec((tb, half), lambda b, s: (s, 0)),
            pl.BlockSpec((tb, half), lambda b, s: (s, 0)),
            _const_spec(dmat.shape),
            _const_spec(qdec.shape),
            _const_spec(kdec.shape),
            _const_spec(cdec.shape),
            _const_spec((1, D_MODEL)),
            _const_spec(wr.shape),
            _const_spec(br.shape),
            _const_spec((tb, tb)),
        ],
        out_specs=[
            pl.BlockSpec((tb * TILES_PER_ROW, LANES), tile),
            pl.BlockSpec((SUBLANES, tb), lambda b, s: (0, b * n_s + s)),
        ],
        out_shape=[
            jax.ShapeDtypeStruct((total * TILES_PER_ROW, LANES), jnp.float32),
            jax.ShapeDtypeStruct((SUBLANES, total), jnp.float32),
        ],
        scratch_shapes=[
            pltpu.VMEM((RET_HEADS, RET_DK, RET_DV), jnp.float32),
            pltpu.VMEM((tb, RET_V_DIM), jnp.bfloat16),
            pltpu.VMEM((LANES, 1), jnp.float32),
        ],
        compiler_params=pltpu.CompilerParams(
            dimension_semantics=("arbitrary", "arbitrary"),
            vmem_limit_bytes=VMEM_LIMIT),
        name="retention_layer",
    )(x2d, gain, w_in, w_out, cos, sin, dmat, qdec, kdec, cdec, gain2, wr, br, _tri(tb))


def _pool_kernel(x_ref, gain_ref, pw_ref, scale_ref, gain2_ref, wr_ref, br_ref, tri_ref,
                 x1_ref, route_ref, ext_ref, y_ref, cnt_ref):
    s = pl.program_id(1)
    tb = x_ref.shape[0] // TILES_PER_ROW

    @pl.when(s == 0)
    def _():
        ext_ref[0:POOL_HIST, :] = jnp.zeros((POOL_HIST, D_MODEL), jnp.float32)

    @pl.when(jnp.logical_and(pl.program_id(0) == 0, s == 0))
    def _():
        cnt_ref[...] = jnp.zeros_like(cnt_ref)

    x = _load_rows(x_ref, tb)
    ext_ref[POOL_HIST:, :] = _rms(x, gain_ref[...])
    pos = s * tb + lax.broadcasted_iota(jnp.int32, (tb, 1), 0)
    for g, w in enumerate(POOL_WINDOWS):
        cs = slice(g * POOL_GROUP_DIM, (g + 1) * POOL_GROUP_DIM)
        e = ext_ref[:, cs]
        acc = e
        k = 1
        while k < w:
            acc = acc + pltpu.roll(acc, k, axis=0)
            k *= 2
        inv_cnt = 1.0 / jnp.minimum(pos + 1, w).astype(jnp.float32)
        pooled = acc[POOL_HIST:, :] * inv_cnt - e[POOL_HIST:, :]
        y_ref[:, cs] = _dot(pooled.astype(jnp.bfloat16), pw_ref[g])
    ext_ref[0:POOL_HIST, :] = ext_ref[tb:tb + POOL_HIST, :]
    x1 = x + y_ref[...] * scale_ref[...]
    _store_rows(x1_ref, x1)
    route_ref[...] = _route(x1, gain2_ref[...], wr_ref[...], br_ref[...], tri_ref[...], cnt_ref)


def _pool_layer(x2d, batch, seq, gain, pw, scale, gain2, wr, br):
    tb = POOL_TILE
    n_s = seq // tb
    total = batch * seq
    tile = lambda b, s: (b * n_s + s, 0)
    return pl.pallas_call(
        _pool_kernel,
        grid=(batch, n_s),
        in_specs=[
            pl.BlockSpec((tb * TILES_PER_ROW, LANES), tile),
            _const_spec((1, D_MODEL)),
            _const_spec(pw.shape),
            _const_spec((1, D_MODEL)),
            _const_spec((1, D_MODEL)),
            _const_spec(wr.shape),
            _const_spec(br.shape),
            _const_spec((tb, tb)),
        ],
        out_specs=[
            pl.BlockSpec((tb * TILES_PER_ROW, LANES), tile),
            pl.BlockSpec((SUBLANES, tb), lambda b, s: (0, b * n_s + s)),
        ],
        out_shape=[
            jax.ShapeDtypeStruct((total * TILES_PER_ROW, LANES), jnp.float32),
            jax.ShapeDtypeStruct((SUBLANES, total), jnp.float32),
        ],
        scratch_shapes=[
            pltpu.VMEM((POOL_HIST + tb, D_MODEL), jnp.float32),
            pltpu.VMEM((tb, D_MODEL), jnp.float32),
            pltpu.VMEM((LANES, 1), jnp.float32),
        ],
        compiler_params=pltpu.CompilerParams(
            dimension_semantics=("arbitrary", "arbitrary"),
            vmem_limit_bytes=VMEM_LIMIT),
        name="pool_layer",
    )(x2d, gain, pw, scale, gain2, wr, br, _tri(tb))


def _tile_of(row):
    return pl.ds(pl.multiple_of(row * TILES_PER_ROW, TILES_PER_ROW), TILES_PER_ROW)


def _dispatch_kernel(slot_ref, run_start_ref, run_len_ref, n_used_ref,
                     x_ref, out_hbm, zero_ref, sem, sem_pad):
    i = pl.program_id(0)
    tile = x_ref.shape[0] // TILES_PER_ROW

    @pl.when(i == 0)
    def _():
        zero_ref[...] = jnp.zeros_like(zero_ref)

        def pad_copy(c):
            rows = pl.multiple_of(run_len_ref[c] * TILES_PER_ROW, TILES_PER_ROW)
            first = pl.multiple_of(run_start_ref[c] * TILES_PER_ROW, TILES_PER_ROW)
            return pltpu.make_async_copy(zero_ref.at[pl.ds(0, rows)], out_hbm.at[pl.ds(first, rows)],
                                         sem_pad.at[0])

        def block_copy(blk):
            first = pl.multiple_of(blk * zero_ref.shape[0], zero_ref.shape[0])
            return pltpu.make_async_copy(zero_ref, out_hbm.at[pl.ds(first, zero_ref.shape[0])],
                                         sem_pad.at[0])

        n_blocks = out_hbm.shape[0] // zero_ref.shape[0]
        for c in range(N_CLASSES):
            @pl.when(run_len_ref[c] > 0)
            def _():
                pad_copy(c).start()
        lax.fori_loop(n_used_ref[0], n_blocks, lambda blk, c: (block_copy(blk).start(), c)[1], 0)
        for c in range(N_CLASSES):
            @pl.when(run_len_ref[c] > 0)
            def _():
                pad_copy(c).wait()
        lax.fori_loop(n_used_ref[0], n_blocks, lambda blk, c: (block_copy(blk).wait(), c)[1], 0)

    for r in range(tile):
        slot = slot_ref[i * tile + r]
        pltpu.make_async_copy(x_ref.at[_tile_of(r)], out_hbm.at[_tile_of(slot)],
                              sem.at[0]).start(priority=r % DMA_THREADS)
    pltpu.make_async_copy(x_ref, out_hbm.at[pl.ds(0, x_ref.shape[0])], sem.at[0]).wait()


def _dispatch(x_tiled, slot_of_tok, run_start, run_len, n_used, n_slots):
    total = slot_of_tok.shape[0]
    tile = DISPATCH_TILE
    grid_spec = pltpu.PrefetchScalarGridSpec(
        num_scalar_prefetch=4,
        grid=(total // tile,),
        in_specs=[pl.BlockSpec((tile * TILES_PER_ROW, LANES), lambda i, *_: (i, 0))],
        out_specs=pl.BlockSpec(memory_space=pl.ANY),
        scratch_shapes=[
            pltpu.VMEM((MOE_ROWS * TILES_PER_ROW, LANES), jnp.float32),
            pltpu.SemaphoreType.DMA((1,)),
            pltpu.SemaphoreType.DMA((1,)),
        ],
    )
    return pl.pallas_call(
        _dispatch_kernel,
        grid_spec=grid_spec,
        out_shape=jax.ShapeDtypeStruct((n_slots * TILES_PER_ROW, LANES), jnp.float32),
        compiler_params=pltpu.CompilerParams(
            dimension_semantics=("arbitrary",),
            vmem_limit_bytes=VMEM_LIMIT),
        name="moe_dispatch",
    )(slot_of_tok, run_start, run_len, n_used, x_tiled)


def _moe_kernel(dst_ref, nv_ref, ea_ref, eb_ref, xblk_ref,
                x_ref, wts_ref, gain_ref, fgain_ref, wgu_a_ref, wd_a_ref, wgu_b_ref, wd_b_ref,
                out_hbm, obuf, sem_out, *, final_norm, total):
    b = pl.program_id(0)
    n = nv_ref[b]
    n_prev = nv_ref[jnp.maximum(b - 1, 0)]
    osl = b % MOE_OBUFS

    def out_copy(sl, r, tok):
        if final_norm:
            return pltpu.make_async_copy(obuf.at[sl, pl.ds(r, 1)], out_hbm.at[pl.ds(tok, 1)],
                                         sem_out.at[sl])
        return pltpu.make_async_copy(obuf.at[sl, _tile_of(r)], out_hbm.at[_tile_of(tok)],
                                     sem_out.at[sl])

    def start_scatter(blk, sl, rows):
        def unrolled():
            for r in range(MOE_ROWS):
                out_copy(sl, r, dst_ref[blk * MOE_ROWS + r]).start(priority=r % DMA_THREADS)

        if not final_norm:
            unrolled()
            return
        pl.when(rows == MOE_ROWS)(unrolled)

        @pl.when(rows < MOE_ROWS)
        def _():
            def body(r, c):
                out_copy(sl, r, dst_ref[blk * MOE_ROWS + r]).start()
                return c
            lax.fori_loop(0, rows, body, 0)

    def wait_scatter(sl, rows):
        if not final_norm:
            pltpu.make_async_copy(obuf.at[sl], out_hbm.at[pl.ds(0, obuf.shape[1])],
                                  sem_out.at[sl]).wait()
            return
        tiled = pl.multiple_of((rows // SUBLANES) * SUBLANES, SUBLANES)

        @pl.when(tiled > 0)
        def _():
            pltpu.make_async_copy(obuf.at[sl, pl.ds(0, tiled)], out_hbm.at[pl.ds(0, tiled)],
                                  sem_out.at[sl]).wait()

        def body(r, c):
            out_copy(sl, 0, 0).wait()
            return c
        lax.fori_loop(tiled, rows, body, 0)

    if not final_norm:
        @pl.when(b == 0)
        def _():
            obuf[...] = jnp.zeros(obuf.shape, jnp.float32)
            for sl in range(MOE_OBUFS):
                for r in range(MOE_ROWS):
                    out_copy(sl, r, total + sl * MOE_ROWS + r).start()

    n_prev2 = nv_ref[jnp.maximum(b - MOE_OBUFS, 0)]

    @pl.when(n > 0)
    def _():
        xb = _load_rows(x_ref, MOE_ROWS)
        hb = _rms(xb, gain_ref[...]).astype(jnp.bfloat16)
        wt = wts_ref[0].T

        def expert(wgu_ref, wd_ref):
            gu = _dot(hb, wgu_ref[0])
            hid = _silu(gu[:, :MOE_D_FF]) * gu[:, MOE_D_FF:]
            return _dot(hid.astype(jnp.bfloat16), wd_ref[0])

        o = (xb + wt[:, 0:1] * expert(wgu_a_ref, wd_a_ref)
             + wt[:, 1:2] * expert(wgu_b_ref, wd_b_ref))
        if final_norm:
            pl.when(b >= MOE_OBUFS)(lambda: wait_scatter(osl, n_prev2))
            obuf[osl] = _rms(o, fgain_ref[...])
        else:
            wait_scatter(osl, n_prev2)
            _store_rows(obuf.at[osl], o)
        start_scatter(b, osl, n)

    @pl.when(jnp.logical_and(n == 0, n_prev > 0))
    def _():
        if final_norm:
            wait_scatter(1 - osl, n_prev)
            pl.when(b >= MOE_OBUFS)(lambda: wait_scatter(osl, n_prev2))
        else:
            for sl in range(MOE_OBUFS):
                wait_scatter(sl, n_prev2)


def _class_tables():
    ea = np.zeros((N_CLASSES,), np.int32)
    eb = np.zeros((N_CLASSES,), np.int32)
    for g in range(MOE_GROUPS):
        c = g * PAIRS_PER_GROUP
        for a in range(MOE_EPG):
            for b in range(a + 1, MOE_EPG):
                ea[c], eb[c] = g * MOE_EPG + a, g * MOE_EPG + b
                c += 1
    return ea, eb


def _moe_layer(x_tiled, route, gain, fgain, wgu, wd, final_norm):
    total = route.shape[1]
    nb = total // MOE_ROWS + N_CLASSES + 1
    n_slots = nb * MOE_ROWS
    w_lo, w_hi = route[2], route[3]
    rank = route[4].astype(jnp.int32)
    cls = route[5].astype(jnp.int32)
    cls_ids = jnp.arange(N_CLASSES, dtype=jnp.int32)
    onehot_tok = (cls[:, None] == cls_ids[None, :]).astype(jnp.int32)
    counts = jnp.sum(onehot_tok, axis=0)
    nblk = (counts + MOE_ROWS - 1) // MOE_ROWS
    npad = nblk * MOE_ROWS - counts
    blk_end = jnp.cumsum(nblk)
    cls_start = (blk_end - nblk) * MOE_ROWS
    slot_of_tok = jnp.sum(onehot_tok * cls_start[None, :], axis=1) + rank
    x_sorted = _dispatch(x_tiled, slot_of_tok, cls_start + counts, npad, blk_end[-1:], n_slots)

    max_pad = MOE_ROWS - 1
    cand_cls = jnp.repeat(cls_ids, max_pad)
    cand_on = jnp.tile(jnp.arange(max_pad, dtype=jnp.int32), N_CLASSES) < jnp.repeat(npad, max_pad)
    n_fill = n_slots - total - N_CLASSES * max_pad
    keys = jnp.concatenate([2 * cls, jnp.where(cand_on, 2 * cand_cls + 1, 2 * N_CLASSES),
                            jnp.full((n_fill,), 2 * N_CLASSES, jnp.int32)])
    n_extra = n_slots - total
    toks = jnp.concatenate([jnp.arange(total, dtype=jnp.int32), jnp.full((n_extra,), -1, jnp.int32)])
    zeros = jnp.zeros((n_extra,), jnp.float32)
    _, tok_s, wlo_s, whi_s = lax.sort(
        (keys, toks, jnp.concatenate([w_lo, zeros]), jnp.concatenate([w_hi, zeros])),
        num_keys=1, is_stable=True)
    is_pad = tok_s < 0
    slot = jnp.arange(n_slots, dtype=jnp.int32)
    dump = total + ((slot // MOE_ROWS) % MOE_OBUFS) * MOE_ROWS + slot % MOE_ROWS
    dst = jnp.where(is_pad, dump, tok_s)
    nvalid = jnp.sum((~is_pad).reshape(nb, MOE_ROWS).astype(jnp.int32), axis=1)
    wts = jnp.concatenate(
        [wlo_s.reshape(nb, 1, MOE_ROWS), whi_s.reshape(nb, 1, MOE_ROWS),
         jnp.zeros((nb, SUBLANES - 2, MOE_ROWS), jnp.float32)], axis=1)
    bidx = jnp.arange(nb, dtype=jnp.int32)
    xblk = jnp.minimum(bidx, blk_end[-1] - 1)
    c_blk = jnp.sum((blk_end[None, :] <= xblk[:, None]).astype(jnp.int32), axis=1)
    ea_tab, eb_tab = _class_tables()
    onehot_blk = (c_blk[:, None] == cls_ids[None, :]).astype(jnp.int32)
    ea = jnp.sum(onehot_blk * jnp.asarray(ea_tab)[None, :], axis=1)
    eb = jnp.sum(onehot_blk * jnp.asarray(eb_tab)[None, :], axis=1)

    tiled_block = (MOE_ROWS * TILES_PER_ROW, LANES)
    if final_norm:
        out_block = (MOE_ROWS, D_MODEL)
        out_shape = (total, D_MODEL)
    else:
        out_block = tiled_block
        out_shape = ((total + MOE_OBUFS * MOE_ROWS) * TILES_PER_ROW, LANES)

    grid_spec = pltpu.PrefetchScalarGridSpec(
        num_scalar_prefetch=5,
        grid=(nb,),
        in_specs=[
            pl.BlockSpec(tiled_block, lambda b, dst, nv, ea, eb, xblk: (xblk[b], 0)),
            pl.BlockSpec((1, SUBLANES, MOE_ROWS), lambda b, *_: (b, 0, 0)),
            pl.BlockSpec((1, D_MODEL), lambda b, *_: (0, 0)),
            pl.BlockSpec((1, D_MODEL), lambda b, *_: (0, 0)),
            pl.BlockSpec((1, D_MODEL, 2 * MOE_D_FF), lambda b, dst, nv, ea, eb, xblk: (ea[b], 0, 0)),
            pl.BlockSpec((1, MOE_D_FF, D_MODEL), lambda b, dst, nv, ea, eb, xblk: (ea[b], 0, 0)),
            pl.BlockSpec((1, D_MODEL, 2 * MOE_D_FF), lambda b, dst, nv, ea, eb, xblk: (eb[b], 0, 0)),
            pl.BlockSpec((1, MOE_D_FF, D_MODEL), lambda b, dst, nv, ea, eb, xblk: (eb[b], 0, 0)),
        ],
        out_specs=pl.BlockSpec(memory_space=pl.ANY),
        scratch_shapes=[
            pltpu.VMEM((MOE_OBUFS,) + out_block, jnp.float32),
            pltpu.SemaphoreType.DMA((MOE_OBUFS,)),
        ],
    )
    return pl.pallas_call(
        functools.partial(_moe_kernel, final_norm=final_norm, total=total),
        grid_spec=grid_spec,
        out_shape=jax.ShapeDtypeStruct(out_shape, jnp.float32),
        compiler_params=pltpu.CompilerParams(
            dimension_semantics=("arbitrary",),
            vmem_limit_bytes=VMEM_LIMIT),
        name="moe_layer",
    )(dst, nvalid, ea, eb, xblk, x_sorted, wts, gain, fgain, wgu, wd, wgu, wd)


def _router_params(w_group, b_group, w_expert, b_expert):
    gap = ROUTE_EXPERT_COL - MOE_GROUPS
    pad = LANES - ROUTE_EXPERT_COL - MOE_EXPERTS
    wr = jnp.concatenate([w_group, jnp.zeros((D_MODEL, gap), jnp.float32), w_expert,
                          jnp.zeros((D_MODEL, pad), jnp.float32)], axis=1)
    br = jnp.concatenate([b_group, jnp.zeros((gap,), jnp.float32), b_expert.reshape(-1),
                          jnp.zeros((pad,), jnp.float32)])[None, :]
    wr_hi = wr.astype(jnp.bfloat16)
    wr_lo = (wr - wr_hi.astype(jnp.float32)).astype(jnp.bfloat16)
    return jnp.concatenate([wr_hi, wr_hi, wr_lo], axis=0), br


def kernel(x, norm_mix, norm_ffn, ret_w_in, ret_w_out, pool_w, pool_scale, moe_w_group, moe_b_group, moe_w_expert, moe_b_expert, moe_w_gate, moe_w_up, moe_w_down, final_norm):
    batch, seq, _ = x.shape
    x2d = x.reshape(batch * seq, D_MODEL)
    bf = jnp.bfloat16
    wgu = jnp.concatenate([moe_w_gate, moe_w_up], axis=-1).astype(bf)
    wd = moe_w_down.astype(bf)
    fgain = final_norm[None, :]

    wr0, br0 = _router_params(moe_w_group[0], moe_b_group[0], moe_w_expert[0], moe_b_expert[0])
    x1, route0 = _retention_layer(x2d, batch, seq, norm_mix[0][None, :], ret_w_in[0].astype(bf),
                                  ret_w_out[0].astype(bf), norm_ffn[0][None, :], wr0, br0)
    x2 = _moe_layer(x1, route0, norm_ffn[0][None, :], fgain, wgu[0], wd[0], final_norm=False)

    wr1, br1 = _router_params(moe_w_group[1], moe_b_group[1], moe_w_expert[1], moe_b_expert[1])
    x3, route1 = _pool_layer(x2, batch, seq, norm_mix[1][None, :], pool_w[0].astype(bf),
                             pool_scale[0][None, :], norm_ffn[1][None, :], wr1, br1)
    out = _moe_layer(x3, route1, norm_ffn[1][None, :], fgain, wgu[1], wd[1], final_norm=True)
    return out.reshape(batch, seq, D_MODEL)
```

```python
import functools

import numpy as np
import jax
import jax.numpy as jnp
from jax import lax
from jax.experimental import pallas as pl
from jax.experimental.pallas import tpu as pltpu

D_MODEL = 1024
EPS = 1e-6
CHUNK = 64

RET_HEADS = 4
RET_DK = 256
RET_DV = 512
RET_QK_DIM = RET_HEADS * RET_DK
RET_V_DIM = RET_HEADS * RET_DV
ROPE_BASE = 10000.0

POOL_WINDOWS = (2, 4, 8, 16)
POOL_GROUP_DIM = 256
POOL_HIST = 16

MOE_GROUPS = 4
MOE_EPG = 8
MOE_EXPERTS = MOE_GROUPS * MOE_EPG
MOE_D_FF = 256
PAIRS_PER_GROUP = MOE_EPG * (MOE_EPG - 1) // 2
N_CLASSES = MOE_GROUPS * PAIRS_PER_GROUP

LANES = 128
SUBLANES = 8

RET_TILE = 256
RET_BLOCK = 256
POOL_TILE = 512
MOE_ROWS = 128
MOE_STEP_BLOCKS = 2
DISPATCH_TILE = 1024
MOE_OBUFS = 2
DMA_THREADS = 2
VMEM_LIMIT = 56 * 1024 * 1024


def _rms(x, gain):
    ms = jnp.mean(x * x, axis=-1, keepdims=True)
    return x * lax.rsqrt(ms + EPS) * gain


def _silu(x):
    return x * jax.nn.sigmoid(x)


def _dot(a, b):
    return jnp.dot(a, b, preferred_element_type=jnp.float32)


ROUTE_EXPERT_COL = 8
TILES_PER_ROW = D_MODEL // LANES


def _load_rows(ref, rows):
    return jnp.concatenate(
        [ref[pl.ds(s, rows, stride=TILES_PER_ROW), :] for s in range(TILES_PER_ROW)], axis=1)


def _store_rows(ref, val):
    rows = val.shape[0]
    for s in range(TILES_PER_ROW):
        ref[pl.ds(s, rows, stride=TILES_PER_ROW), :] = val[:, s * LANES:(s + 1) * LANES]


def _route(x1, gain2, wr3, br, tri, cnt_ref):
    rows = x1.shape[0]
    h2 = _rms(x1, gain2)
    h_hi = h2.astype(jnp.bfloat16)
    h_lo = (h2 - h_hi.astype(jnp.float32)).astype(jnp.bfloat16)
    logits = _dot(jnp.concatenate([h_hi, h_lo, h_hi], axis=1), wr3) + br
    lt = logits.T
    sub = lax.broadcasted_iota(jnp.int32, (SUBLANES, rows), 0)
    neg = jnp.float32(-jnp.inf)
    gl = jnp.where(sub < MOE_GROUPS, lt[0:SUBLANES], neg)
    gmax = jnp.max(gl, axis=0, keepdims=True)
    grp = jnp.min(jnp.where(gl == gmax, sub, SUBLANES), axis=0, keepdims=True)
    g_gate = 1.0 / jnp.sum(jnp.exp(gl - gmax), axis=0, keepdims=True)
    el = lt[ROUTE_EXPERT_COL:ROUTE_EXPERT_COL + MOE_EPG]
    for g in range(1, MOE_GROUPS):
        first = ROUTE_EXPERT_COL + g * MOE_EPG
        el = jnp.where(grp == g, lt[first:first + MOE_EPG], el)
    m1 = jnp.max(el, axis=0, keepdims=True)
    i1 = jnp.min(jnp.where(el == m1, sub, SUBLANES), axis=0, keepdims=True)
    el2 = jnp.where(sub == i1, neg, el)
    m2 = jnp.max(el2, axis=0, keepdims=True)
    i2 = jnp.min(jnp.where(el2 == m2, sub, SUBLANES), axis=0, keepdims=True)
    t = jnp.exp(m2 - m1)
    w1 = g_gate / (1.0 + t)
    w2 = g_gate * t / (1.0 + t)
    first_lo = i1 < i2
    a = jnp.where(first_lo, i1, i2)
    bb = jnp.where(first_lo, i2, i1)
    base = grp * MOE_EPG
    lo = (base + a).astype(jnp.float32)
    hi = (base + bb).astype(jnp.float32)
    w_lo = jnp.where(first_lo, w1, w2)
    w_hi = jnp.where(first_lo, w2, w1)
    cls = grp * PAIRS_PER_GROUP + ((a * (2 * MOE_EPG - 1 - a)) >> 1) + (bb - a - 1)
    onehot = (lax.broadcasted_iota(jnp.int32, (LANES, rows), 0) == cls).astype(jnp.float32)
    earlier = _dot(onehot.astype(jnp.bfloat16), tri) + cnt_ref[...]
    rank = jnp.sum(onehot * earlier, axis=0, keepdims=True)
    cnt_ref[...] += jnp.sum(onehot, axis=1, keepdims=True)
    out = jnp.where(sub == 0, lo, jnp.where(sub == 1, hi, jnp.where(sub == 2, w_lo,
                    jnp.where(sub == 3, w_hi, 0.0))))
    return jnp.where(sub == 4, rank, jnp.where(sub == 5, cls.astype(jnp.float32), out))


def _ret_kernel(x_ref, gain_ref, win_ref, wout_ref, cos_ref, sin_ref, dmat_ref,
                qdec_ref, kdec_ref, cdec_ref, gain2_ref, wr_ref, br_ref, tri_ref,
                x1_ref, route_ref, state_ref, gated_ref, cnt_ref):
    @pl.when(pl.program_id(1) == 0)
    def _():
        state_ref[...] = jnp.zeros_like(state_ref)

    @pl.when(jnp.logical_and(pl.program_id(0) == 0, pl.program_id(1) == 0))
    def _():
        cnt_ref[...] = jnp.zeros_like(cnt_ref)

    x = x_ref[...]
    h = _rms(x, gain_ref[...]).astype(jnp.bfloat16)
    cos = cos_ref[...]
    sin = sin_ref[...]
    half = RET_DK // 2

    def rope(t):
        t1, t2 = t[:, :half], t[:, half:]
        return jnp.concatenate([t1 * cos - t2 * sin, t1 * sin + t2 * cos], axis=1)

    for hd in range(RET_HEADS):
        q = _dot(h, win_ref[:, hd * RET_DK:(hd + 1) * RET_DK])
        k = _dot(h, win_ref[:, RET_QK_DIM + hd * RET_DK:RET_QK_DIM + (hd + 1) * RET_DK])
        v0 = 2 * RET_QK_DIM + hd * RET_DV
        v = _dot(h, win_ref[:, v0:v0 + RET_DV])
        g0 = 2 * RET_QK_DIM + RET_V_DIM + hd * RET_DV
        g = _dot(h, win_ref[:, g0:g0 + RET_DV])
        q = rope(q)
        k = rope(k) * (RET_DK ** -0.5)
        for blk in range(x.shape[0] // RET_BLOCK):
            rs = slice(blk * RET_BLOCK, (blk + 1) * RET_BLOCK)
            qb = q[rs].astype(jnp.bfloat16)
            kb = k[rs].astype(jnp.bfloat16)
            vb = v[rs].astype(jnp.bfloat16)
            sc = lax.dot_general(qb, kb, (((1,), (1,)), ((), ())),
                                 preferred_element_type=jnp.float32)
            sc = sc * dmat_ref[hd]
            intra = _dot(sc.astype(jnp.bfloat16), vb)
            st = state_ref[hd]
            cross = _dot(qb, st.astype(jnp.bfloat16)) * qdec_ref[hd]
            o = intra + cross
            kd = (k[rs] * kdec_ref[hd]).astype(jnp.bfloat16)
            upd = lax.dot_general(kd, vb, (((0,), (0,)), ((), ())),
                                  preferred_element_type=jnp.float32)
            state_ref[hd] = st * cdec_ref[hd] + upd
            mu = jnp.mean(o, axis=-1, keepdims=True)
            oc = o - mu
            var = jnp.mean(oc * oc, axis=-1, keepdims=True)
            on = oc * lax.rsqrt(var + EPS)
            gated_ref[rs, hd * RET_DV:(hd + 1) * RET_DV] = (_silu(g[rs]) * on).astype(jnp.bfloat16)

    x1 = x + _dot(gated_ref[...], wout_ref[...])
    _store_rows(x1_ref, x1)
    route_ref[...] = _route(x1, gain2_ref[...], wr_ref[...], br_ref[...], tri_ref[...], cnt_ref)


def _tri(n):
    i = jnp.arange(n)
    return (i[:, None] < i[None, :]).astype(jnp.bfloat16)


def _const_spec(shape):
    nd = len(shape)
    return pl.BlockSpec(shape, lambda *_: (0,) * nd, pipeline_mode=pl.Buffered(1))


def _retention_layer(x2d, batch, seq, gain, w_in, w_out, gain2, wr, br):
    tb = RET_TILE
    n_s = seq // tb
    total = batch * seq
    rb = RET_BLOCK
    log_g = jnp.log(1.0 - jnp.exp2(-5.0 - jnp.arange(RET_HEADS, dtype=jnp.float32)))
    n = jnp.arange(rb, dtype=jnp.float32)
    diff = n[:, None] - n[None, :]
    cn = jnp.arange(rb)[:, None] // CHUNK
    cm = jnp.arange(rb)[None, :] // CHUNK
    expo = jnp.where(cn == cm, jnp.abs(diff), diff)
    dmat = jnp.where((cm <= cn)[None], jnp.exp(log_g[:, None, None] * expo[None]), 0.0)
    qdec = jnp.exp(log_g[:, None] * (n + 1.0)[None])[:, :, None]
    kdec = jnp.exp(log_g[:, None] * (rb - 1.0 - n)[None])[:, :, None]
    cdec = jnp.broadcast_to(jnp.exp(log_g * rb)[:, None, None], (RET_HEADS, 1, RET_DV))
    half = RET_DK // 2
    inv_freq = 1.0 / (ROPE_BASE ** (jnp.arange(half, dtype=jnp.float32) / half))
    ang = jnp.arange(seq, dtype=jnp.float32)[:, None] * inv_freq[None, :]
    cos, sin = jnp.cos(ang), jnp.sin(ang)

    tile = lambda b, s: (b * n_s + s, 0)
    return pl.pallas_call(
        _ret_kernel,
        grid=(batch, n_s),
        in_specs=[
            pl.BlockSpec((tb, D_MODEL), tile),
            _const_spec((1, D_MODEL)),
            _const_spec(w_in.shape),
            _const_spec(w_out.shape),
            pl.BlockSpec((tb, half), lambda b, s: (s, 0)),
            pl.BlockSpec((tb, half), lambda b, s: (s, 0)),
            _const_spec(dmat.shape),
            _const_spec(qdec.shape),
            _const_spec(kdec.shape),
            _const_spec(cdec.shape),
            _const_spec((1, D_MODEL)),
            _const_spec(wr.shape),
            _const_spec(br.shape),
            _const_spec((tb, tb)),
        ],
        out_specs=[
            pl.BlockSpec((tb * TILES_PER_ROW, LANES), tile),
            pl.BlockSpec((SUBLANES, tb), lambda b, s: (0, b * n_s + s)),
        ],
        out_shape=[
            jax.ShapeDtypeStruct((total * TILES_PER_ROW, LANES), jnp.float32),
            jax.ShapeDtypeStruct((SUBLANES, total), jnp.float32),
        ],
        scratch_shapes=[
            pltpu.VMEM((RET_HEADS, RET_DK, RET_DV), jnp.float32),
            pltpu.VMEM((tb, RET_V_DIM), jnp.bfloat16),
            pltpu.VMEM((LANES, 1), jnp.float32),
        ],
        compiler_params=pltpu.CompilerParams(
            dimension_semantics=("arbitrary", "arbitrary"),
            vmem_limit_bytes=VMEM_LIMIT),
        name="retention_layer",
    )(x2d, gain, w_in, w_out, cos, sin, dmat, qdec, kdec, cdec, gain2, wr, br, _tri(tb))


def _pool_kernel(x_ref, gain_ref, pw_ref, scale_ref, gain2_ref, wr_ref, br_ref, tri_ref,
                 x1_ref, route_ref, ext_ref, y_ref, cnt_ref):
    s = pl.program_id(1)
    tb = x_ref.shape[0] // TILES_PER_ROW

    @pl.when(s == 0)
    def _():
        ext_ref[0:POOL_HIST, :] = jnp.zeros((POOL_HIST, D_MODEL), jnp.float32)

    @pl.when(jnp.logical_and(pl.program_id(0) == 0, s == 0))
    def _():
        cnt_ref[...] = jnp.zeros_like(cnt_ref)

    x = _load_rows(x_ref, tb)
    ext_ref[POOL_HIST:, :] = _rms(x, gain_ref[...])
    pos = s * tb + lax.broadcasted_iota(jnp.int32, (tb, 1), 0)
    for g, w in enumerate(POOL_WINDOWS):
        cs = slice(g * POOL_GROUP_DIM, (g + 1) * POOL_GROUP_DIM)
        e = ext_ref[:, cs]
        acc = e
        k = 1
        while k < w:
            acc = acc + pltpu.roll(acc, k, axis=0)
            k *= 2
        inv_cnt = 1.0 / jnp.minimum(pos + 1, w).astype(jnp.float32)
        pooled = acc[POOL_HIST:, :] * inv_cnt - e[POOL_HIST:, :]
        y_ref[:, cs] = _dot(pooled.astype(jnp.bfloat16), pw_ref[g])
    ext_ref[0:POOL_HIST, :] = ext_ref[tb:tb + POOL_HIST, :]
    x1 = x + y_ref[...] * scale_ref[...]
    _store_rows(x1_ref, x1)
    route_ref[...] = _route(x1, gain2_ref[...], wr_ref[...], br_ref[...], tri_ref[...], cnt_ref)


def _pool_layer(x2d, batch, seq, gain, pw, scale, gain2, wr, br):
    tb = POOL_TILE
    n_s = seq // tb
    total = batch * seq
    tile = lambda b, s: (b * n_s + s, 0)
    return pl.pallas_call(
        _pool_kernel,
        grid=(batch, n_s),
        in_specs=[
            pl.BlockSpec((tb * TILES_PER_ROW, LANES), tile),
            _const_spec((1, D_MODEL)),
            _const_spec(pw.shape),
            _const_spec((1, D_MODEL)),
            _const_spec((1, D_MODEL)),
            _const_spec(wr.shape),
            _const_spec(br.shape),
            _const_spec((tb, tb)),
        ],
        out_specs=[
            pl.BlockSpec((tb * TILES_PER_ROW, LANES), tile),
            pl.BlockSpec((SUBLANES, tb), lambda b, s: (0, b * n_s + s)),
        ],
        out_shape=[
            jax.ShapeDtypeStruct((total * TILES_PER_ROW, LANES), jnp.float32),
            jax.ShapeDtypeStruct((SUBLANES, total), jnp.float32),
        ],
        scratch_shapes=[
            pltpu.VMEM((POOL_HIST + tb, D_MODEL), jnp.float32),
            pltpu.VMEM((tb, D_MODEL), jnp.float32),
            pltpu.VMEM((LANES, 1), jnp.float32),
        ],
        compiler_params=pltpu.CompilerParams(
            dimension_semantics=("arbitrary", "arbitrary"),
            vmem_limit_bytes=VMEM_LIMIT),
        name="pool_layer",
    )(x2d, gain, pw, scale, gain2, wr, br, _tri(tb))


def _tile_of(row):
    return pl.ds(pl.multiple_of(row * TILES_PER_ROW, TILES_PER_ROW), TILES_PER_ROW)


def _dispatch_kernel(slot_ref, run_start_ref, run_len_ref, n_used_ref,
                     x_ref, out_hbm, zero_ref, sem, sem_pad):
    i = pl.program_id(0)
    tile = x_ref.shape[0] // TILES_PER_ROW

    @pl.when(i == 0)
    def _():
        zero_ref[...] = jnp.zeros_like(zero_ref)

        def pad_copy(c):
            rows = pl.multiple_of(run_len_ref[c] * TILES_PER_ROW, TILES_PER_ROW)
            first = pl.multiple_of(run_start_ref[c] * TILES_PER_ROW, TILES_PER_ROW)
            return pltpu.make_async_copy(zero_ref.at[pl.ds(0, rows)], out_hbm.at[pl.ds(first, rows)],
                                         sem_pad.at[0])

        def block_copy(blk):
            first = pl.multiple_of(blk * zero_ref.shape[0], zero_ref.shape[0])
            return pltpu.make_async_copy(zero_ref, out_hbm.at[pl.ds(first, zero_ref.shape[0])],
                                         sem_pad.at[0])

        n_blocks = out_hbm.shape[0] // zero_ref.shape[0]
        for c in range(N_CLASSES):
            @pl.when(run_len_ref[c] > 0)
            def _():
                pad_copy(c).start()
        lax.fori_loop(n_used_ref[0], n_blocks, lambda blk, c: (block_copy(blk).start(), c)[1], 0)
        for c in range(N_CLASSES):
            @pl.when(run_len_ref[c] > 0)
            def _():
                pad_copy(c).wait()
        lax.fori_loop(n_used_ref[0], n_blocks, lambda blk, c: (block_copy(blk).wait(), c)[1], 0)

    for r in range(tile):
        slot = slot_ref[i * tile + r]
        pltpu.make_async_copy(x_ref.at[_tile_of(r)], out_hbm.at[_tile_of(slot)],
                              sem.at[0]).start(priority=r % DMA_THREADS)
    pltpu.make_async_copy(x_ref, out_hbm.at[pl.ds(0, x_ref.shape[0])], sem.at[0]).wait()


def _dispatch(x_tiled, slot_of_tok, run_start, run_len, n_used, n_slots):
    total = slot_of_tok.shape[0]
    tile = DISPATCH_TILE
    grid_spec = pltpu.PrefetchScalarGridSpec(
        num_scalar_prefetch=4,
        grid=(total // tile,),
        in_specs=[pl.BlockSpec((tile * TILES_PER_ROW, LANES), lambda i, *_: (i, 0))],
        out_specs=pl.BlockSpec(memory_space=pl.ANY),
        scratch_shapes=[
            pltpu.VMEM((MOE_ROWS * TILES_PER_ROW, LANES), jnp.float32),
            pltpu.SemaphoreType.DMA((1,)),
            pltpu.SemaphoreType.DMA((1,)),
        ],
    )
    return pl.pallas_call(
        _dispatch_kernel,
        grid_spec=grid_spec,
        out_shape=jax.ShapeDtypeStruct((n_slots * TILES_PER_ROW, LANES), jnp.float32),
        compiler_params=pltpu.CompilerParams(
            dimension_semantics=("arbitrary",),
            vmem_limit_bytes=VMEM_LIMIT),
        name="moe_dispatch",
    )(slot_of_tok, run_start, run_len, n_used, x_tiled)


def _moe_kernel(dst_ref, nv_ref, ea_ref, eb_ref, xstep_ref,
                x_ref, wts_ref, gain_ref, fgain_ref, *rest, final_norm, total):
    w_refs = rest[:4 * MOE_STEP_BLOCKS]
    out_hbm, obuf, sem_out = rest[4 * MOE_STEP_BLOCKS:]
    i = pl.program_id(0)
    blk0 = i * MOE_STEP_BLOCKS
    n_first = nv_ref[blk0]
    n_first_prev = nv_ref[jnp.maximum(blk0 - MOE_STEP_BLOCKS, 0)]
    osl = i % MOE_OBUFS
    step_rows = MOE_STEP_BLOCKS * MOE_ROWS

    def counts(step):
        first = jnp.maximum(step, 0) * MOE_STEP_BLOCKS
        return [nv_ref[first + j] for j in range(MOE_STEP_BLOCKS)]

    def out_copy(sl, r, tok):
        if final_norm:
            return pltpu.make_async_copy(obuf.at[sl, pl.ds(r, 1)], out_hbm.at[pl.ds(tok, 1)],
                                         sem_out.at[sl])
        return pltpu.make_async_copy(obuf.at[sl, _tile_of(r)], out_hbm.at[_tile_of(tok)],
                                     sem_out.at[sl])

    def start_scatter(step, sl, rows):
        def unrolled(j):
            for r in range(j * MOE_ROWS, (j + 1) * MOE_ROWS):
                out_copy(sl, r, dst_ref[step * step_rows + r]).start(priority=r % DMA_THREADS)

        for j in range(MOE_STEP_BLOCKS):
            if not final_norm:
                unrolled(j)
                continue
            pl.when(rows[j] == MOE_ROWS)(functools.partial(unrolled, j))

            @pl.when(rows[j] < MOE_ROWS)
            def _():
                def body(r, c):
                    out_copy(sl, j * MOE_ROWS + r, dst_ref[step * step_rows + j * MOE_ROWS + r]).start()
                    return c
                lax.fori_loop(0, rows[j], body, 0)

    def wait_scatter(sl, rows):
        if not final_norm:
            pltpu.make_async_copy(obuf.at[sl], out_hbm.at[pl.ds(0, obuf.shape[1])],
                                  sem_out.at[sl]).wait()
            return
        for j in range(MOE_STEP_BLOCKS):
            tiled = pl.multiple_of((rows[j] // SUBLANES) * SUBLANES, SUBLANES)

            @pl.when(tiled > 0)
            def _():
                pltpu.make_async_copy(obuf.at[sl, pl.ds(0, tiled)], out_hbm.at[pl.ds(0, tiled)],
                                      sem_out.at[sl]).wait()

            def body(r, c):
                out_copy(sl, 0, 0).wait()
                return c
            lax.fori_loop(tiled, rows[j], body, 0)

    if not final_norm:
        @pl.when(i == 0)
        def _():
            obuf[...] = jnp.zeros(obuf.shape, jnp.float32)
            for sl in range(MOE_OBUFS):
                for r in range(step_rows):
                    out_copy(sl, r, total + sl * step_rows + r).start()

    @pl.when(n_first > 0)
    def _():
        tile_rows = MOE_ROWS * TILES_PER_ROW
        blocks = range(MOE_STEP_BLOCKS)
        xs = [_load_rows(x_ref.at[pl.ds(j * tile_rows, tile_rows)], MOE_ROWS) for j in blocks]
        hs = [_rms(x, gain_ref[...]).astype(jnp.bfloat16) for x in xs]
        wts = [wts_ref[j].T for j in blocks]
        gus = [(_dot(hs[j], w_refs[4 * j][0]), _dot(hs[j], w_refs[4 * j + 2][0])) for j in blocks]
        hid = [[(_silu(g[:, :MOE_D_FF]) * g[:, MOE_D_FF:]).astype(jnp.bfloat16) for g in gus[j]]
               for j in blocks]
        outs = [xs[j] + wts[j][:, 0:1] * _dot(hid[j][0], w_refs[4 * j + 1][0])
                + wts[j][:, 1:2] * _dot(hid[j][1], w_refs[4 * j + 3][0]) for j in blocks]
        if final_norm:
            pl.when(i >= MOE_OBUFS)(lambda: wait_scatter(osl, counts(i - MOE_OBUFS)))
            for j in blocks:
                obuf[osl, j * MOE_ROWS:(j + 1) * MOE_ROWS] = _rms(outs[j], fgain_ref[...])
        else:
            wait_scatter(osl, None)
            for j in blocks:
                _store_rows(obuf.at[osl, pl.ds(j * tile_rows, tile_rows)], outs[j])
        start_scatter(i, osl, counts(i))

    @pl.when(jnp.logical_and(n_first == 0, n_first_prev > 0))
    def _():
        if final_norm:
            wait_scatter(1 - osl, counts(i - 1))
            pl.when(i >= MOE_OBUFS)(lambda: wait_scatter(osl, counts(i - MOE_OBUFS)))
        else:
            for sl in range(MOE_OBUFS):
                wait_scatter(sl, None)


def _class_tables():
    ea = np.zeros((N_CLASSES,), np.int32)
    eb = np.zeros((N_CLASSES,), np.int32)
    for g in range(MOE_GROUPS):
        c = g * PAIRS_PER_GROUP
        for a in range(MOE_EPG):
            for b in range(a + 1, MOE_EPG):
                ea[c], eb[c] = g * MOE_EPG + a, g * MOE_EPG + b
                c += 1
    return ea, eb


def _moe_layer(x_tiled, route, gain, fgain, wgu, wd, final_norm):
    total = route.shape[1]
    nb = (total // MOE_ROWS + N_CLASSES) // MOE_STEP_BLOCKS * MOE_STEP_BLOCKS + MOE_STEP_BLOCKS
    n_steps = nb // MOE_STEP_BLOCKS
    step_rows = MOE_STEP_BLOCKS * MOE_ROWS
    n_slots = nb * MOE_ROWS
    w_lo, w_hi = route[2], route[3]
    rank = route[4].astype(jnp.int32)
    cls = route[5].astype(jnp.int32)
    cls_ids = jnp.arange(N_CLASSES, dtype=jnp.int32)
    onehot_tok = (cls[:, None] == cls_ids[None, :]).astype(jnp.int32)
    counts = jnp.sum(onehot_tok, axis=0)
    nblk = (counts + MOE_ROWS - 1) // MOE_ROWS
    npad = nblk * MOE_ROWS - counts
    blk_end = jnp.cumsum(nblk)
    cls_start = (blk_end - nblk) * MOE_ROWS
    slot_of_tok = jnp.sum(onehot_tok * cls_start[None, :], axis=1) + rank
    x_sorted = _dispatch(x_tiled, slot_of_tok, cls_start + counts, npad, blk_end[-1:], n_slots)

    max_pad = MOE_ROWS - 1
    cand_cls = jnp.repeat(cls_ids, max_pad)
    cand_on = jnp.tile(jnp.arange(max_pad, dtype=jnp.int32), N_CLASSES) < jnp.repeat(npad, max_pad)
    n_fill = n_slots - total - N_CLASSES * max_pad
    keys = jnp.concatenate([2 * cls, jnp.where(cand_on, 2 * cand_cls + 1, 2 * N_CLASSES),
                            jnp.full((n_fill,), 2 * N_CLASSES, jnp.int32)])
    n_extra = n_slots - total
    toks = jnp.concatenate([jnp.arange(total, dtype=jnp.int32), jnp.full((n_extra,), -1, jnp.int32)])
    zeros = jnp.zeros((n_extra,), jnp.float32)
    _, tok_s, wlo_s, whi_s = lax.sort(
        (keys, toks, jnp.concatenate([w_lo, zeros]), jnp.concatenate([w_hi, zeros])),
        num_keys=1, is_stable=True)
    is_pad = tok_s < 0
    slot = jnp.arange(n_slots, dtype=jnp.int32)
    dump = total + ((slot // step_rows) % MOE_OBUFS) * step_rows + slot % step_rows
    dst = jnp.where(is_pad, dump, tok_s)
    nvalid = jnp.sum((~is_pad).reshape(nb, MOE_ROWS).astype(jnp.int32), axis=1)
    wts = jnp.concatenate(
        [wlo_s.reshape(nb, 1, MOE_ROWS), whi_s.reshape(nb, 1, MOE_ROWS),
         jnp.zeros((nb, SUBLANES - 2, MOE_ROWS), jnp.float32)], axis=1)
    bidx = jnp.arange(nb, dtype=jnp.int32)
    xblk = jnp.minimum(bidx, blk_end[-1] - 1)
    c_blk = jnp.sum((blk_end[None, :] <= xblk[:, None]).astype(jnp.int32), axis=1)
    xstep = jnp.minimum(jnp.arange(n_steps, dtype=jnp.int32), (blk_end[-1] - 1) // MOE_STEP_BLOCKS)
    ea_tab, eb_tab = _class_tables()
    onehot_blk = (c_blk[:, None] == cls_ids[None, :]).astype(jnp.int32)
    ea = jnp.sum(onehot_blk * jnp.asarray(ea_tab)[None, :], axis=1)
    eb = jnp.sum(onehot_blk * jnp.asarray(eb_tab)[None, :], axis=1)

    tiled_block = (step_rows * TILES_PER_ROW, LANES)
    if final_norm:
        out_block = (step_rows, D_MODEL)
        out_shape = (total, D_MODEL)
    else:
        out_block = tiled_block
        out_shape = ((total + MOE_OBUFS * step_rows) * TILES_PER_ROW, LANES)

    def weight_specs(j):
        def spec(shape, tab):
            return pl.BlockSpec(shape, lambda i, dst, nv, ea, eb, xstep:
                                ((ea, eb)[tab][i * MOE_STEP_BLOCKS + j], 0, 0))
        gu, dn = (1, D_MODEL, 2 * MOE_D_FF), (1, MOE_D_FF, D_MODEL)
        return [spec(gu, 0), spec(dn, 0), spec(gu, 1), spec(dn, 1)]

    grid_spec = pltpu.PrefetchScalarGridSpec(
        num_scalar_prefetch=5,
        grid=(n_steps,),
        in_specs=[
            pl.BlockSpec(tiled_block, lambda i, dst, nv, ea, eb, xstep: (xstep[i], 0)),
            pl.BlockSpec((MOE_STEP_BLOCKS, SUBLANES, MOE_ROWS), lambda i, *_: (i, 0, 0)),
            pl.BlockSpec((1, D_MODEL), lambda i, *_: (0, 0)),
            pl.BlockSpec((1, D_MODEL), lambda i, *_: (0, 0)),
        ] + [s for j in range(MOE_STEP_BLOCKS) for s in weight_specs(j)],
        out_specs=pl.BlockSpec(memory_space=pl.ANY),
        scratch_shapes=[
            pltpu.VMEM((MOE_OBUFS,) + out_block, jnp.float32),
            pltpu.SemaphoreType.DMA((MOE_OBUFS,)),
        ],
    )
    return pl.pallas_call(
        functools.partial(_moe_kernel, final_norm=final_norm, total=total),
        grid_spec=grid_spec,
        out_shape=jax.ShapeDtypeStruct(out_shape, jnp.float32),
        compiler_params=pltpu.CompilerParams(
            dimension_semantics=("arbitrary",),
            vmem_limit_bytes=VMEM_LIMIT),
        name="moe_layer",
    )(dst, nvalid, ea, eb, xstep, x_sorted, wts, gain, fgain, *([wgu, wd, wgu, wd] * MOE_STEP_BLOCKS))


def _router_params(w_group, b_group, w_expert, b_expert):
    gap = ROUTE_EXPERT_COL - MOE_GROUPS
    pad = LANES - ROUTE_EXPERT_COL - MOE_EXPERTS
    wr = jnp.concatenate([w_group, jnp.zeros((D_MODEL, gap), jnp.float32), w_expert,
                          jnp.zeros((D_MODEL, pad), jnp.float32)], axis=1)
    br = jnp.concatenate([b_group, jnp.zeros((gap,), jnp.float32), b_expert.reshape(-1),
                          jnp.zeros((pad,), jnp.float32)])[None, :]
    wr_hi = wr.astype(jnp.bfloat16)
    wr_lo = (wr - wr_hi.astype(jnp.float32)).astype(jnp.bfloat16)
    return jnp.concatenate([wr_hi, wr_hi, wr_lo], axis=0), br


def kernel(x, norm_mix, norm_ffn, ret_w_in, ret_w_out, pool_w, pool_scale, moe_w_group, moe_b_group, moe_w_expert, moe_b_expert, moe_w_gate, moe_w_up, moe_w_down, final_norm):
    batch, seq, _ = x.shape
    x2d = x.reshape(batch * seq, D_MODEL)
    bf = jnp.bfloat16
    wgu = jnp.concatenate([moe_w_gate, moe_w_up], axis=-1).astype(bf)
    wd = moe_w_down.astype(bf)
    fgain = final_norm[None, :]

    wr0, br0 = _router_params(moe_w_group[0], moe_b_group[0], moe_w_expert[0], moe_b_expert[0])
    x1, route0 = _retention_layer(x2d, batch, seq, norm_mix[0][None, :], ret_w_in[0].astype(bf),
                                  ret_w_out[0].astype(bf), norm_ffn[0][None, :], wr0, br0)
    x2 = _moe_layer(x1, route0, norm_ffn[0][None, :], fgain, wgu[0], wd[0], final_norm=False)

    wr1, br1 = _router_params(moe_w_group[1], moe_b_group[1], moe_w_expert[1], moe_b_expert[1])
    x3, route1 = _pool_layer(x2, batch, seq, norm_mix[1][None, :], pool_w[0].astype(bf),
                             pool_scale[0][None, :], norm_ffn[1][None, :], wr1, br1)
    out = _moe_layer(x3, route1, norm_ffn[1][None, :], fgain, wgu[1], wd[1], final_norm=True)
    return out.reshape(batch, seq, D_MODEL)
```

```python
import functools

import numpy as np
import jax
import jax.numpy as jnp
from jax import lax
from jax.experimental import pallas as pl
from jax.experimental.pallas import tpu as pltpu

D_MODEL = 1024
EPS = 1e-6
CHUNK = 64

RET_HEADS = 4
RET_DK = 256
RET_DV = 512
RET_QK_DIM = RET_HEADS * RET_DK
RET_V_DIM = RET_HEADS * RET_DV
ROPE_BASE = 10000.0

POOL_WINDOWS = (2, 4, 8, 16)
POOL_GROUP_DIM = 256
POOL_HIST = 16

MOE_GROUPS = 4
MOE_EPG = 8
MOE_EXPERTS = MOE_GROUPS * MOE_EPG
MOE_D_FF = 256
PAIRS_PER_GROUP = MOE_EPG * (MOE_EPG - 1) // 2
N_CLASSES = MOE_GROUPS * PAIRS_PER_GROUP

LANES = 128
SUBLANES = 8

RET_TILE = 256
RET_BLOCK = 256
POOL_TILE = 512
MOE_ROWS = 128
MOE_STEP_BLOCKS = 2
DISPATCH_TILE = 2048
MOE_OBUFS = 2
DMA_THREADS = 2
VMEM_LIMIT = 56 * 1024 * 1024


def _rms(x, gain):
    ms = jnp.mean(x * x, axis=-1, keepdims=True)
    return x * lax.rsqrt(ms + EPS) * gain


def _silu(x):
    return x * jax.nn.sigmoid(x)


def _dot(a, b):
    return jnp.dot(a, b, preferred_element_type=jnp.float32)


ROUTE_EXPERT_COL = 8
TILES_PER_ROW = D_MODEL // LANES


def _load_rows(ref, rows):
    return jnp.concatenate(
        [ref[pl.ds(s, rows, stride=TILES_PER_ROW), :] for s in range(TILES_PER_ROW)], axis=1)


def _store_rows(ref, val):
    rows = val.shape[0]
    for s in range(TILES_PER_ROW):
        ref[pl.ds(s, rows, stride=TILES_PER_ROW), :] = val[:, s * LANES:(s + 1) * LANES]


def _route(x1, gain2, wr3, br, tri, cnt_ref):
    rows = x1.shape[0]
    h2 = _rms(x1, gain2)
    h_hi = h2.astype(jnp.bfloat16)
    h_lo = (h2 - h_hi.astype(jnp.float32)).astype(jnp.bfloat16)
    logits = _dot(jnp.concatenate([h_hi, h_lo, h_hi], axis=1), wr3) + br
    lt = logits.T
    sub = lax.broadcasted_iota(jnp.int32, (SUBLANES, rows), 0)
    neg = jnp.float32(-jnp.inf)
    gl = jnp.where(sub < MOE_GROUPS, lt[0:SUBLANES], neg)
    gmax = jnp.max(gl, axis=0, keepdims=True)
    grp = jnp.min(jnp.where(gl == gmax, sub, SUBLANES), axis=0, keepdims=True)
    g_gate = 1.0 / jnp.sum(jnp.exp(gl - gmax), axis=0, keepdims=True)
    el = lt[ROUTE_EXPERT_COL:ROUTE_EXPERT_COL + MOE_EPG]
    for g in range(1, MOE_GROUPS):
        first = ROUTE_EXPERT_COL + g * MOE_EPG
        el = jnp.where(grp == g, lt[first:first + MOE_EPG], el)
    m1 = jnp.max(el, axis=0, keepdims=True)
    i1 = jnp.min(jnp.where(el == m1, sub, SUBLANES), axis=0, keepdims=True)
    el2 = jnp.where(sub == i1, neg, el)
    m2 = jnp.max(el2, axis=0, keepdims=True)
    i2 = jnp.min(jnp.where(el2 == m2, sub, SUBLANES), axis=0, keepdims=True)
    t = jnp.exp(m2 - m1)
    w1 = g_gate / (1.0 + t)
    w2 = g_gate * t / (1.0 + t)
    first_lo = i1 < i2
    a = jnp.where(first_lo, i1, i2)
    bb = jnp.where(first_lo, i2, i1)
    base = grp * MOE_EPG
    lo = (base + a).astype(jnp.float32)
    hi = (base + bb).astype(jnp.float32)
    w_lo = jnp.where(first_lo, w1, w2)
    w_hi = jnp.where(first_lo, w2, w1)
    cls = grp * PAIRS_PER_GROUP + ((a * (2 * MOE_EPG - 1 - a)) >> 1) + (bb - a - 1)
    onehot = (lax.broadcasted_iota(jnp.int32, (LANES, rows), 0) == cls).astype(jnp.float32)
    earlier = _dot(onehot.astype(jnp.bfloat16), tri) + cnt_ref[...]
    rank = jnp.sum(onehot * earlier, axis=0, keepdims=True)
    cnt_ref[...] += jnp.sum(onehot, axis=1, keepdims=True)
    out = jnp.where(sub == 0, lo, jnp.where(sub == 1, hi, jnp.where(sub == 2, w_lo,
                    jnp.where(sub == 3, w_hi, 0.0))))
    return jnp.where(sub == 4, rank, jnp.where(sub == 5, cls.astype(jnp.float32), out))


def _ret_kernel(x_ref, gain_ref, win_ref, wout_ref, cos_ref, sin_ref, dmat_ref,
                qdec_ref, kdec_ref, cdec_ref, gain2_ref, wr_ref, br_ref, tri_ref,
                x1_ref, route_ref, state_ref, gated_ref, cnt_ref):
    @pl.when(pl.program_id(1) == 0)
    def _():
        state_ref[...] = jnp.zeros_like(state_ref)

    @pl.when(jnp.logical_and(pl.program_id(0) == 0, pl.program_id(1) == 0))
    def _():
        cnt_ref[...] = jnp.zeros_like(cnt_ref)

    x = x_ref[...]
    h = _rms(x, gain_ref[...]).astype(jnp.bfloat16)
    cos = cos_ref[...]
    sin = sin_ref[...]
    half = RET_DK // 2

    def rope(t):
        t1, t2 = t[:, :half], t[:, half:]
        return jnp.concatenate([t1 * cos - t2 * sin, t1 * sin + t2 * cos], axis=1)

    for hd in range(RET_HEADS):
        q = _dot(h, win_ref[:, hd * RET_DK:(hd + 1) * RET_DK])
        k = _dot(h, win_ref[:, RET_QK_DIM + hd * RET_DK:RET_QK_DIM + (hd + 1) * RET_DK])
        v0 = 2 * RET_QK_DIM + hd * RET_DV
        v = _dot(h, win_ref[:, v0:v0 + RET_DV])
        g0 = 2 * RET_QK_DIM + RET_V_DIM + hd * RET_DV
        g = _dot(h, win_ref[:, g0:g0 + RET_DV])
        q = rope(q)
        k = rope(k) * (RET_DK ** -0.5)
        for blk in range(x.shape[0] // RET_BLOCK):
            rs = slice(blk * RET_BLOCK, (blk + 1) * RET_BLOCK)
            qb = q[rs].astype(jnp.bfloat16)
            kb = k[rs].astype(jnp.bfloat16)
            vb = v[rs].astype(jnp.bfloat16)
            sc = lax.dot_general(qb, kb, (((1,), (1,)), ((), ())),
                                 preferred_element_type=jnp.float32)
            sc = sc * dmat_ref[hd]
            intra = _dot(sc.astype(jnp.bfloat16), vb)
            st = state_ref[hd]
            cross = _dot(qb, st.astype(jnp.bfloat16)) * qdec_ref[hd]
            o = intra + cross
            kd = (k[rs] * kdec_ref[hd]).astype(jnp.bfloat16)
            upd = lax.dot_general(kd, vb, (((0,), (0,)), ((), ())),
                                  preferred_element_type=jnp.float32)
            state_ref[hd] = st * cdec_ref[hd] + upd
            mu = jnp.mean(o, axis=-1, keepdims=True)
            oc = o - mu
            var = jnp.mean(oc * oc, axis=-1, keepdims=True)
            on = oc * lax.rsqrt(var + EPS)
            gated_ref[rs, hd * RET_DV:(hd + 1) * RET_DV] = (_silu(g[rs]) * on).astype(jnp.bfloat16)

    x1 = x + _dot(gated_ref[...], wout_ref[...])
    _store_rows(x1_ref, x1)
    route_ref[...] = _route(x1, gain2_ref[...], wr_ref[...], br_ref[...], tri_ref[...], cnt_ref)


def _tri(n):
    i = jnp.arange(n)
    return (i[:, None] < i[None, :]).astype(jnp.bfloat16)


def _const_spec(shape):
    nd = len(shape)
    return pl.BlockSpec(shape, lambda *_: (0,) * nd, pipeline_mode=pl.Buffered(1))


def _retention_layer(x2d, batch, seq, gain, w_in, w_out, gain2, wr, br):
    tb = RET_TILE
    n_s = seq // tb
    total = batch * seq
    rb = RET_BLOCK
    log_g = jnp.log(1.0 - jnp.exp2(-5.0 - jnp.arange(RET_HEADS, dtype=jnp.float32)))
    n = jnp.arange(rb, dtype=jnp.float32)
    diff = n[:, None] - n[None, :]
    cn = jnp.arange(rb)[:, None] // CHUNK
    cm = jnp.arange(rb)[None, :] // CHUNK
    expo = jnp.where(cn == cm, jnp.abs(diff), diff)
    dmat = jnp.where((cm <= cn)[None], jnp.exp(log_g[:, None, None] * expo[None]), 0.0)
    qdec = jnp.exp(log_g[:, None] * (n + 1.0)[None])[:, :, None]
    kdec = jnp.exp(log_g[:, None] * (rb - 1.0 - n)[None])[:, :, None]
    cdec = jnp.broadcast_to(jnp.exp(log_g * rb)[:, None, None], (RET_HEADS, 1, RET_DV))
    half = RET_DK // 2
    inv_freq = 1.0 / (ROPE_BASE ** (jnp.arange(half, dtype=jnp.float32) / half))
    ang = jnp.arange(seq, dtype=jnp.float32)[:, None] * inv_freq[None, :]
    cos, sin = jnp.cos(ang), jnp.sin(ang)

    tile = lambda b, s: (b * n_s + s, 0)
    return pl.pallas_call(
        _ret_kernel,
        grid=(batch, n_s),
        in_specs=[
            pl.BlockSpec((tb, D_MODEL), tile),
            _const_spec((1, D_MODEL)),
            _const_spec(w_in.shape),
            _const_spec(w_out.shape),
            pl.BlockSpec((tb, half), lambda b, s: (s, 0)),
            pl.BlockSpec((tb, half), lambda b, s: (s, 0)),
            _const_spec(dmat.shape),
            _const_spec(qdec.shape),
            _const_spec(kdec.shape),
            _const_spec(cdec.shape),
            _const_spec((1, D_MODEL)),
            _const_spec(wr.shape),
            _const_spec(br.shape),
            _const_spec((tb, tb)),
        ],
        out_specs=[
            pl.BlockSpec((tb * TILES_PER_ROW, LANES), tile),
            pl.BlockSpec((SUBLANES, tb), lambda b, s: (0, b * n_s + s)),
        ],
        out_shape=[
            jax.ShapeDtypeStruct((total * TILES_PER_ROW, LANES), jnp.float32),
            jax.ShapeDtypeStruct((SUBLANES, total), jnp.float32),
        ],
        scratch_shapes=[
            pltpu.VMEM((RET_HEADS, RET_DK, RET_DV), jnp.float32),
            pltpu.VMEM((tb, RET_V_DIM), jnp.bfloat16),
            pltpu.VMEM((LANES, 1), jnp.float32),
        ],
        compiler_params=pltpu.CompilerParams(
            dimension_semantics=("arbitrary", "arbitrary"),
            vmem_limit_bytes=VMEM_LIMIT),
        name="retention_layer",
    )(x2d, gain, w_in, w_out, cos, sin, dmat, qdec, kdec, cdec, gain2, wr, br, _tri(tb))


def _pool_kernel(x_ref, gain_ref, pw_ref, scale_ref, gain2_ref, wr_ref, br_ref, tri_ref,
                 x1_ref, route_ref, ext_ref, y_ref, cnt_ref):
    s = pl.program_id(1)
    tb = x_ref.shape[0] // TILES_PER_ROW

    @pl.when(s == 0)
    def _():
        ext_ref[0:POOL_HIST, :] = jnp.zeros((POOL_HIST, D_MODEL), jnp.float32)

    @pl.when(jnp.logical_and(pl.program_id(0) == 0, s == 0))
    def _():
        cnt_ref[...] = jnp.zeros_like(cnt_ref)

    x = _load_rows(x_ref, tb)
    ext_ref[POOL_HIST:, :] = _rms(x, gain_ref[...])
    pos = s * tb + lax.broadcasted_iota(jnp.int32, (tb, 1), 0)
    for g, w in enumerate(POOL_WINDOWS):
        cs = slice(g * POOL_GROUP_DIM, (g + 1) * POOL_GROUP_DIM)
        e = ext_ref[:, cs]
        acc = e
        k = 1
        while k < w:
            acc = acc + pltpu.roll(acc, k, axis=0)
            k *= 2
        inv_cnt = 1.0 / jnp.minimum(pos + 1, w).astype(jnp.float32)
        pooled = acc[POOL_HIST:, :] * inv_cnt - e[POOL_HIST:, :]
        y_ref[:, cs] = _dot(pooled.astype(jnp.bfloat16), pw_ref[g])
    ext_ref[0:POOL_HIST, :] = ext_ref[tb:tb + POOL_HIST, :]
    x1 = x + y_ref[...] * scale_ref[...]
    _store_rows(x1_ref, x1)
    route_ref[...] = _route(x1, gain2_ref[...], wr_ref[...], br_ref[...], tri_ref[...], cnt_ref)


def _pool_layer(x2d, batch, seq, gain, pw, scale, gain2, wr, br):
    tb = POOL_TILE
    n_s = seq // tb
    total = batch * seq
    tile = lambda b, s: (b * n_s + s, 0)
    return pl.pallas_call(
        _pool_kernel,
        grid=(batch, n_s),
        in_specs=[
            pl.BlockSpec((tb * TILES_PER_ROW, LANES), tile),
            _const_spec((1, D_MODEL)),
            _const_spec(pw.shape),
            _const_spec((1, D_MODEL)),
            _const_spec((1, D_MODEL)),
            _const_spec(wr.shape),
            _const_spec(br.shape),
            _const_spec((tb, tb)),
        ],
        out_specs=[
            pl.BlockSpec((tb * TILES_PER_ROW, LANES), tile),
            pl.BlockSpec((SUBLANES, tb), lambda b, s: (0, b * n_s + s)),
        ],
        out_shape=[
            jax.ShapeDtypeStruct((total * TILES_PER_ROW, LANES), jnp.float32),
            jax.ShapeDtypeStruct((SUBLANES, total), jnp.float32),
        ],
        scratch_shapes=[
            pltpu.VMEM((POOL_HIST + tb, D_MODEL), jnp.float32),
            pltpu.VMEM((tb, D_MODEL), jnp.float32),
            pltpu.VMEM((LANES, 1), jnp.float32),
        ],
        compiler_params=pltpu.CompilerParams(
            dimension_semantics=("arbitrary", "arbitrary"),
            vmem_limit_bytes=VMEM_LIMIT),
        name="pool_layer",
    )(x2d, gain, pw, scale, gain2, wr, br, _tri(tb))


def _tile_of(row):
    return pl.ds(pl.multiple_of(row * TILES_PER_ROW, TILES_PER_ROW), TILES_PER_ROW)


def _dispatch_kernel(slot_ref, run_start_ref, run_len_ref, n_used_ref,
                     x_ref, out_hbm, zero_ref, sem, sem_pad):
    i = pl.program_id(0)
    tile = x_ref.shape[0] // TILES_PER_ROW

    @pl.when(i == 0)
    def _():
        zero_ref[...] = jnp.zeros_like(zero_ref)

        def pad_copy(c):
            rows = pl.multiple_of(run_len_ref[c] * TILES_PER_ROW, TILES_PER_ROW)
            first = pl.multiple_of(run_start_ref[c] * TILES_PER_ROW, TILES_PER_ROW)
            return pltpu.make_async_copy(zero_ref.at[pl.ds(0, rows)], out_hbm.at[pl.ds(first, rows)],
                                         sem_pad.at[0])

        def block_copy(blk):
            first = pl.multiple_of(blk * zero_ref.shape[0], zero_ref.shape[0])
            return pltpu.make_async_copy(zero_ref, out_hbm.at[pl.ds(first, zero_ref.shape[0])],
                                         sem_pad.at[0])

        n_blocks = out_hbm.shape[0] // zero_ref.shape[0]
        for c in range(N_CLASSES):
            @pl.when(run_len_ref[c] > 0)
            def _():
                pad_copy(c).start()
        lax.fori_loop(n_used_ref[0], n_blocks, lambda blk, c: (block_copy(blk).start(), c)[1], 0)
        for c in range(N_CLASSES):
            @pl.when(run_len_ref[c] > 0)
            def _():
                pad_copy(c).wait()
        lax.fori_loop(n_used_ref[0], n_blocks, lambda blk, c: (block_copy(blk).wait(), c)[1], 0)

    for r in range(tile):
        slot = slot_ref[i * tile + r]
        pltpu.make_async_copy(x_ref.at[_tile_of(r)], out_hbm.at[_tile_of(slot)],
                              sem.at[0]).start(priority=r % DMA_THREADS)
    pltpu.make_async_copy(x_ref, out_hbm.at[pl.ds(0, x_ref.shape[0])], sem.at[0]).wait()


def _dispatch(x_tiled, slot_of_tok, run_start, run_len, n_used, n_slots):
    total = slot_of_tok.shape[0]
    tile = DISPATCH_TILE
    grid_spec = pltpu.PrefetchScalarGridSpec(
        num_scalar_prefetch=4,
        grid=(total // tile,),
        in_specs=[pl.BlockSpec((tile * TILES_PER_ROW, LANES), lambda i, *_: (i, 0))],
        out_specs=pl.BlockSpec(memory_space=pl.ANY),
        scratch_shapes=[
            pltpu.VMEM((MOE_ROWS * TILES_PER_ROW, LANES), jnp.float32),
            pltpu.SemaphoreType.DMA((1,)),
            pltpu.SemaphoreType.DMA((1,)),
        ],
    )
    return pl.pallas_call(
        _dispatch_kernel,
        grid_spec=grid_spec,
        out_shape=jax.ShapeDtypeStruct((n_slots * TILES_PER_ROW, LANES), jnp.float32),
        compiler_params=pltpu.CompilerParams(
            dimension_semantics=("arbitrary",),
            vmem_limit_bytes=VMEM_LIMIT),
        name="moe_dispatch",
    )(slot_of_tok, run_start, run_len, n_used, x_tiled)


def _moe_kernel(dst_ref, nv_ref, ea_ref, eb_ref, xstep_ref,
                x_ref, wts_ref, gain_ref, fgain_ref, *rest, final_norm, total):
    w_refs = rest[:4 * MOE_STEP_BLOCKS]
    out_hbm, obuf, sem_out = rest[4 * MOE_STEP_BLOCKS:]
    i = pl.program_id(0)
    blk0 = i * MOE_STEP_BLOCKS
    n_first = nv_ref[blk0]
    n_first_prev = nv_ref[jnp.maximum(blk0 - MOE_STEP_BLOCKS, 0)]
    osl = i % MOE_OBUFS
    step_rows = MOE_STEP_BLOCKS * MOE_ROWS

    def counts(step):
        first = jnp.maximum(step, 0) * MOE_STEP_BLOCKS
        return [nv_ref[first + j] for j in range(MOE_STEP_BLOCKS)]

    def out_copy(sl, r, tok):
        if final_norm:
            return pltpu.make_async_copy(obuf.at[sl, pl.ds(r, 1)], out_hbm.at[pl.ds(tok, 1)],
                                         sem_out.at[sl])
        return pltpu.make_async_copy(obuf.at[sl, _tile_of(r)], out_hbm.at[_tile_of(tok)],
                                     sem_out.at[sl])

    def start_scatter(step, sl, rows):
        def unrolled(j):
            for r in range(j * MOE_ROWS, (j + 1) * MOE_ROWS):
                out_copy(sl, r, dst_ref[step * step_rows + r]).start(priority=r % DMA_THREADS)

        for j in range(MOE_STEP_BLOCKS):
            if not final_norm:
                unrolled(j)
                continue
            pl.when(rows[j] == MOE_ROWS)(functools.partial(unrolled, j))

            @pl.when(rows[j] < MOE_ROWS)
            def _():
                def body(r, c):
                    out_copy(sl, j * MOE_ROWS + r, dst_ref[step * step_rows + j * MOE_ROWS + r]).start()
                    return c
                lax.fori_loop(0, rows[j], body, 0)

    def wait_scatter(sl, rows):
        if not final_norm:
            pltpu.make_async_copy(obuf.at[sl], out_hbm.at[pl.ds(0, obuf.shape[1])],
                                  sem_out.at[sl]).wait()
            return
        for j in range(MOE_STEP_BLOCKS):
            tiled = pl.multiple_of((rows[j] // SUBLANES) * SUBLANES, SUBLANES)

            @pl.when(tiled > 0)
            def _():
                pltpu.make_async_copy(obuf.at[sl, pl.ds(0, tiled)], out_hbm.at[pl.ds(0, tiled)],
                                      sem_out.at[sl]).wait()

            def body(r, c):
                out_copy(sl, 0, 0).wait()
                return c
            lax.fori_loop(tiled, rows[j], body, 0)

    if not final_norm:
        @pl.when(i == 0)
        def _():
            obuf[...] = jnp.zeros(obuf.shape, jnp.float32)
            for sl in range(MOE_OBUFS):
                for r in range(step_rows):
                    out_copy(sl, r, total + sl * step_rows + r).start()

    @pl.when(n_first > 0)
    def _():
        tile_rows = MOE_ROWS * TILES_PER_ROW
        blocks = range(MOE_STEP_BLOCKS)
        xs = [_load_rows(x_ref.at[pl.ds(j * tile_rows, tile_rows)], MOE_ROWS) for j in blocks]
        hs = [_rms(x, gain_ref[...]).astype(jnp.bfloat16) for x in xs]
        wts = [wts_ref[j].T for j in blocks]
        gus = [(_dot(hs[j], w_refs[4 * j][0, 0]), _dot(hs[j], w_refs[4 * j + 2][0, 0])) for j in blocks]
        hid = [[(_silu(g[:, :MOE_D_FF]) * g[:, MOE_D_FF:]).astype(jnp.bfloat16) for g in gus[j]]
               for j in blocks]
        outs = [xs[j] + wts[j][:, 0:1] * _dot(hid[j][0], w_refs[4 * j + 1][0, 0])
                + wts[j][:, 1:2] * _dot(hid[j][1], w_refs[4 * j + 3][0, 0]) for j in blocks]
        if final_norm:
            pl.when(i >= MOE_OBUFS)(lambda: wait_scatter(osl, counts(i - MOE_OBUFS)))
            for j in blocks:
                obuf[osl, j * MOE_ROWS:(j + 1) * MOE_ROWS] = _rms(outs[j], fgain_ref[...])
        else:
            wait_scatter(osl, None)
            for j in blocks:
                _store_rows(obuf.at[osl, pl.ds(j * tile_rows, tile_rows)], outs[j])
        start_scatter(i, osl, counts(i))

    @pl.when(jnp.logical_and(n_first == 0, n_first_prev > 0))
    def _():
        if final_norm:
            wait_scatter(1 - osl, counts(i - 1))
            pl.when(i >= MOE_OBUFS)(lambda: wait_scatter(osl, counts(i - MOE_OBUFS)))
        else:
            for sl in range(MOE_OBUFS):
                wait_scatter(sl, None)


def _class_tables():
    ea = np.zeros((N_CLASSES,), np.int32)
    eb = np.zeros((N_CLASSES,), np.int32)
    for g in range(MOE_GROUPS):
        c = g * PAIRS_PER_GROUP
        for a in range(MOE_EPG):
            for b in range(a + 1, MOE_EPG):
                ea[c], eb[c] = g * MOE_EPG + a, g * MOE_EPG + b
                c += 1
    return ea, eb


def _moe_layer(x_tiled, route, gain, fgain, wgu, wd, layer, final_norm):
    total = route.shape[1]
    nb = (total // MOE_ROWS + N_CLASSES) // MOE_STEP_BLOCKS * MOE_STEP_BLOCKS + MOE_STEP_BLOCKS
    n_steps = nb // MOE_STEP_BLOCKS
    step_rows = MOE_STEP_BLOCKS * MOE_ROWS
    n_slots = nb * MOE_ROWS
    w_lo, w_hi = route[2], route[3]
    rank = route[4].astype(jnp.int32)
    cls = route[5].astype(jnp.int32)
    cls_ids = jnp.arange(N_CLASSES, dtype=jnp.int32)
    onehot_tok = (cls[:, None] == cls_ids[None, :]).astype(jnp.int32)
    counts = jnp.sum(onehot_tok, axis=0)
    nblk = (counts + MOE_ROWS - 1) // MOE_ROWS
    npad = nblk * MOE_ROWS - counts
    blk_end = jnp.cumsum(nblk)
    cls_start = (blk_end - nblk) * MOE_ROWS
    slot_of_tok = jnp.sum(onehot_tok * cls_start[None, :], axis=1) + rank
    x_sorted = _dispatch(x_tiled, slot_of_tok, cls_start + counts, npad, blk_end[-1:], n_slots)

    max_pad = MOE_ROWS - 1
    cand_cls = jnp.repeat(cls_ids, max_pad)
    cand_on = jnp.tile(jnp.arange(max_pad, dtype=jnp.int32), N_CLASSES) < jnp.repeat(npad, max_pad)
    n_fill = n_slots - total - N_CLASSES * max_pad
    tok_bits = 16
    assert total <= 1 << tok_bits
    keys = jnp.concatenate([
        (2 * cls << tok_bits) + jnp.arange(total, dtype=jnp.int32),
        jnp.where(cand_on, 2 * cand_cls + 1, 2 * N_CLASSES + 1) << tok_bits,
        jnp.full((n_fill,), (2 * N_CLASSES + 1) << tok_bits, jnp.int32)])
    zeros = jnp.zeros((n_slots - total,), jnp.float32)
    key_s, wlo_s, whi_s = lax.sort(
        (keys, jnp.concatenate([w_lo, zeros]), jnp.concatenate([w_hi, zeros])),
        num_keys=1, is_stable=False)
    is_pad = ((key_s >> tok_bits) & 1) == 1
    tok_s = key_s & ((1 << tok_bits) - 1)
    slot = jnp.arange(n_slots, dtype=jnp.int32)
    dump = total + ((slot // step_rows) % MOE_OBUFS) * step_rows + slot % step_rows
    dst = jnp.where(is_pad, dump, tok_s)
    nvalid = jnp.sum((~is_pad).reshape(nb, MOE_ROWS).astype(jnp.int32), axis=1)
    wts = jnp.concatenate(
        [wlo_s.reshape(nb, 1, MOE_ROWS), whi_s.reshape(nb, 1, MOE_ROWS),
         jnp.zeros((nb, SUBLANES - 2, MOE_ROWS), jnp.float32)], axis=1)
    bidx = jnp.arange(nb, dtype=jnp.int32)
    xblk = jnp.minimum(bidx, blk_end[-1] - 1)
    c_blk = jnp.sum((blk_end[None, :] <= xblk[:, None]).astype(jnp.int32), axis=1)
    xstep = jnp.minimum(jnp.arange(n_steps, dtype=jnp.int32), (blk_end[-1] - 1) // MOE_STEP_BLOCKS)
    ea_tab, eb_tab = _class_tables()
    onehot_blk = (c_blk[:, None] == cls_ids[None, :]).astype(jnp.int32)
    ea = jnp.sum(onehot_blk * jnp.asarray(ea_tab)[None, :], axis=1)
    eb = jnp.sum(onehot_blk * jnp.asarray(eb_tab)[None, :], axis=1)

    tiled_block = (step_rows * TILES_PER_ROW, LANES)
    if final_norm:
        out_block = (step_rows, D_MODEL)
        out_shape = (total, D_MODEL)
    else:
        out_block = tiled_block
        out_shape = ((total + MOE_OBUFS * step_rows) * TILES_PER_ROW, LANES)

    def weight_specs(j):
        def spec(shape, tab):
            return pl.BlockSpec(shape, lambda i, dst, nv, ea, eb, xstep:
                                (layer, (ea, eb)[tab][i * MOE_STEP_BLOCKS + j], 0, 0))
        gu, dn = (1, 1, D_MODEL, 2 * MOE_D_FF), (1, 1, MOE_D_FF, D_MODEL)
        return [spec(gu, 0), spec(dn, 0), spec(gu, 1), spec(dn, 1)]

    grid_spec = pltpu.PrefetchScalarGridSpec(
        num_scalar_prefetch=5,
        grid=(n_steps,),
        in_specs=[
            pl.BlockSpec(tiled_block, lambda i, dst, nv, ea, eb, xstep: (xstep[i], 0)),
            pl.BlockSpec((MOE_STEP_BLOCKS, SUBLANES, MOE_ROWS), lambda i, *_: (i, 0, 0)),
            pl.BlockSpec((1, D_MODEL), lambda i, *_: (0, 0)),
            pl.BlockSpec((1, D_MODEL), lambda i, *_: (0, 0)),
        ] + [s for j in range(MOE_STEP_BLOCKS) for s in weight_specs(j)],
        out_specs=pl.BlockSpec(memory_space=pl.ANY),
        scratch_shapes=[
            pltpu.VMEM((MOE_OBUFS,) + out_block, jnp.float32),
            pltpu.SemaphoreType.DMA((MOE_OBUFS,)),
        ],
    )
    return pl.pallas_call(
        functools.partial(_moe_kernel, final_norm=final_norm, total=total),
        grid_spec=grid_spec,
        out_shape=jax.ShapeDtypeStruct(out_shape, jnp.float32),
        compiler_params=pltpu.CompilerParams(
            dimension_semantics=("arbitrary",),
            vmem_limit_bytes=VMEM_LIMIT),
        name="moe_layer",
    )(dst, nvalid, ea, eb, xstep, x_sorted, wts, gain, fgain, *([wgu, wd, wgu, wd] * MOE_STEP_BLOCKS))


def _router_params(w_group, b_group, w_expert, b_expert):
    gap = ROUTE_EXPERT_COL - MOE_GROUPS
    pad = LANES - ROUTE_EXPERT_COL - MOE_EXPERTS
    wr = jnp.concatenate([w_group, jnp.zeros((D_MODEL, gap), jnp.float32), w_expert,
                          jnp.zeros((D_MODEL, pad), jnp.float32)], axis=1)
    br = jnp.concatenate([b_group, jnp.zeros((gap,), jnp.float32), b_expert.reshape(-1),
                          jnp.zeros((pad,), jnp.float32)])[None, :]
    wr_hi = wr.astype(jnp.bfloat16)
    wr_lo = (wr - wr_hi.astype(jnp.float32)).astype(jnp.bfloat16)
    return jnp.concatenate([wr_hi, wr_hi, wr_lo], axis=0), br


def kernel(x, norm_mix, norm_ffn, ret_w_in, ret_w_out, pool_w, pool_scale, moe_w_group, moe_b_group, moe_w_expert, moe_b_expert, moe_w_gate, moe_w_up, moe_w_down, final_norm):
    batch, seq, _ = x.shape
    x2d = x.reshape(batch * seq, D_MODEL)
    bf = jnp.bfloat16
    wgu = jnp.concatenate([moe_w_gate, moe_w_up], axis=-1).astype(bf)
    wd = moe_w_down.astype(bf)
    fgain = final_norm[None, :]

    wr0, br0 = _router_params(moe_w_group[0], moe_b_group[0], moe_w_expert[0], moe_b_expert[0])
    x1, route0 = _retention_layer(x2d, batch, seq, norm_mix[0][None, :], ret_w_in[0].astype(bf),
                                  ret_w_out[0].astype(bf), norm_ffn[0][None, :], wr0, br0)
    x2 = _moe_layer(x1, route0, norm_ffn[0][None, :], fgain, wgu, wd, 0, final_norm=False)

    wr1, br1 = _router_params(moe_w_group[1], moe_b_group[1], moe_w_expert[1], moe_b_expert[1])
    x3, route1 = _pool_layer(x2, batch, seq, norm_mix[1][None, :], pool_w[0].astype(bf),
                             pool_scale[0][None, :], norm_ffn[1][None, :], wr1, br1)
    out = _moe_layer(x3, route1, norm_ffn[1][None, :], fgain, wgu, wd, 1, final_norm=True)
    return out.reshape(batch, seq, D_MODEL)
```

```python
import functools

import numpy as np
import jax
import jax.numpy as jnp
from jax import lax
from jax.experimental import pallas as pl
from jax.experimental.pallas import tpu as pltpu

D_MODEL = 1024
EPS = 1e-6
CHUNK = 64

RET_HEADS = 4
RET_DK = 256
RET_DV = 512
RET_QK_DIM = RET_HEADS * RET_DK
RET_V_DIM = RET_HEADS * RET_DV
ROPE_BASE = 10000.0

POOL_WINDOWS = (2, 4, 8, 16)
POOL_GROUP_DIM = 256
POOL_HIST = 16

MOE_GROUPS = 4
MOE_EPG = 8
MOE_EXPERTS = MOE_GROUPS * MOE_EPG
MOE_D_FF = 256
PAIRS_PER_GROUP = MOE_EPG * (MOE_EPG - 1) // 2
N_CLASSES = MOE_GROUPS * PAIRS_PER_GROUP

LANES = 128
SUBLANES = 8

RET_TILE = 256
RET_BLOCK = 256
POOL_TILE = 512
MOE_ROWS = 128
MOE_STEP_BLOCKS = 2
MOE_WEIGHTS = ("gate", "up", "down")
DISPATCH_TILE = 2048
MOE_OBUFS = 2
DMA_THREADS = 2
VMEM_LIMIT = 56 * 1024 * 1024


def _rms(x, gain):
    ms = jnp.mean(x * x, axis=-1, keepdims=True)
    return x * lax.rsqrt(ms + EPS) * gain


def _silu(x):
    return x * jax.nn.sigmoid(x)


def _dot(a, b):
    return jnp.dot(a, b, preferred_element_type=jnp.float32)


ROUTE_EXPERT_COL = 8
TILES_PER_ROW = D_MODEL // LANES


def _load_rows(ref, rows):
    return jnp.concatenate(
        [ref[pl.ds(s, rows, stride=TILES_PER_ROW), :] for s in range(TILES_PER_ROW)], axis=1)


def _store_rows(ref, val):
    rows = val.shape[0]
    for s in range(TILES_PER_ROW):
        ref[pl.ds(s, rows, stride=TILES_PER_ROW), :] = val[:, s * LANES:(s + 1) * LANES]


def _route(x1, gain2, wr3, br, tri, cnt_ref):
    rows = x1.shape[0]
    h2 = _rms(x1, gain2)
    h_hi = h2.astype(jnp.bfloat16)
    h_lo = (h2 - h_hi.astype(jnp.float32)).astype(jnp.bfloat16)
    logits = _dot(jnp.concatenate([h_hi, h_lo, h_hi], axis=1), wr3) + br
    lt = logits.T
    sub = lax.broadcasted_iota(jnp.int32, (SUBLANES, rows), 0)
    neg = jnp.float32(-jnp.inf)
    gl = jnp.where(sub < MOE_GROUPS, lt[0:SUBLANES], neg)
    gmax = jnp.max(gl, axis=0, keepdims=True)
    grp = jnp.min(jnp.where(gl == gmax, sub, SUBLANES), axis=0, keepdims=True)
    g_gate = 1.0 / jnp.sum(jnp.exp(gl - gmax), axis=0, keepdims=True)
    el = lt[ROUTE_EXPERT_COL:ROUTE_EXPERT_COL + MOE_EPG]
    for g in range(1, MOE_GROUPS):
        first = ROUTE_EXPERT_COL + g * MOE_EPG
        el = jnp.where(grp == g, lt[first:first + MOE_EPG], el)
    m1 = jnp.max(el, axis=0, keepdims=True)
    i1 = jnp.min(jnp.where(el == m1, sub, SUBLANES), axis=0, keepdims=True)
    el2 = jnp.where(sub == i1, neg, el)
    m2 = jnp.max(el2, axis=0, keepdims=True)
    i2 = jnp.min(jnp.where(el2 == m2, sub, SUBLANES), axis=0, keepdims=True)
    t = jnp.exp(m2 - m1)
    w1 = g_gate / (1.0 + t)
    w2 = g_gate * t / (1.0 + t)
    first_lo = i1 < i2
    a = jnp.where(first_lo, i1, i2)
    bb = jnp.where(first_lo, i2, i1)
    base = grp * MOE_EPG
    lo = (base + a).astype(jnp.float32)
    hi = (base + bb).astype(jnp.float32)
    w_lo = jnp.where(first_lo, w1, w2)
    w_hi = jnp.where(first_lo, w2, w1)
    cls = grp * PAIRS_PER_GROUP + ((a * (2 * MOE_EPG - 1 - a)) >> 1) + (bb - a - 1)
    onehot = (lax.broadcasted_iota(jnp.int32, (LANES, rows), 0) == cls).astype(jnp.float32)
    earlier = _dot(onehot.astype(jnp.bfloat16), tri) + cnt_ref[...]
    rank = jnp.sum(onehot * earlier, axis=0, keepdims=True)
    cnt_ref[...] += jnp.sum(onehot, axis=1, keepdims=True)
    out = jnp.where(sub == 0, lo, jnp.where(sub == 1, hi, jnp.where(sub == 2, w_lo,
                    jnp.where(sub == 3, w_hi, 0.0))))
    return jnp.where(sub == 4, rank, jnp.where(sub == 5, cls.astype(jnp.float32), out))


def _ret_kernel(x_ref, gain_ref, win_ref, wout_ref, cos_ref, sin_ref, dmat_ref,
                qdec_ref, kdec_ref, cdec_ref, gain2_ref, wr_ref, br_ref, tri_ref,
                x1_ref, route_ref, state_ref, gated_ref, cnt_ref):
    @pl.when(pl.program_id(1) == 0)
    def _():
        state_ref[...] = jnp.zeros_like(state_ref)

    @pl.when(jnp.logical_and(pl.program_id(0) == 0, pl.program_id(1) == 0))
    def _():
        cnt_ref[...] = jnp.zeros_like(cnt_ref)

    x = x_ref[...]
    h = _rms(x, gain_ref[...]).astype(jnp.bfloat16)
    cos = cos_ref[...]
    sin = sin_ref[...]
    half = RET_DK // 2

    def rope(t):
        t1, t2 = t[:, :half], t[:, half:]
        return jnp.concatenate([t1 * cos - t2 * sin, t1 * sin + t2 * cos], axis=1)

    for hd in range(RET_HEADS):
        q = _dot(h, win_ref[:, hd * RET_DK:(hd + 1) * RET_DK])
        k = _dot(h, win_ref[:, RET_QK_DIM + hd * RET_DK:RET_QK_DIM + (hd + 1) * RET_DK])
        v0 = 2 * RET_QK_DIM + hd * RET_DV
        v = _dot(h, win_ref[:, v0:v0 + RET_DV])
        g0 = 2 * RET_QK_DIM + RET_V_DIM + hd * RET_DV
        g = _dot(h, win_ref[:, g0:g0 + RET_DV])
        q = rope(q)
        k = rope(k) * (RET_DK ** -0.5)
        for blk in range(x.shape[0] // RET_BLOCK):
            rs = slice(blk * RET_BLOCK, (blk + 1) * RET_BLOCK)
            qb = q[rs].astype(jnp.bfloat16)
            kb = k[rs].astype(jnp.bfloat16)
            vb = v[rs].astype(jnp.bfloat16)
            sc = lax.dot_general(qb, kb, (((1,), (1,)), ((), ())),
                                 preferred_element_type=jnp.float32)
            sc = sc * dmat_ref[hd]
            intra = _dot(sc.astype(jnp.bfloat16), vb)
            st = state_ref[hd]
            cross = _dot(qb, st.astype(jnp.bfloat16)) * qdec_ref[hd]
            o = intra + cross
            kd = (k[rs] * kdec_ref[hd]).astype(jnp.bfloat16)
            upd = lax.dot_general(kd, vb, (((0,), (0,)), ((), ())),
                                  preferred_element_type=jnp.float32)
            state_ref[hd] = st * cdec_ref[hd] + upd
            mu = jnp.mean(o, axis=-1, keepdims=True)
            oc = o - mu
            var = jnp.mean(oc * oc, axis=-1, keepdims=True)
            on = oc * lax.rsqrt(var + EPS)
            gated_ref[rs, hd * RET_DV:(hd + 1) * RET_DV] = (_silu(g[rs]) * on).astype(jnp.bfloat16)

    x1 = x + _dot(gated_ref[...], wout_ref[...])
    _store_rows(x1_ref, x1)
    route_ref[...] = _route(x1, gain2_ref[...], wr_ref[...], br_ref[...], tri_ref[...], cnt_ref)


def _tri(n):
    i = jnp.arange(n)
    return (i[:, None] < i[None, :]).astype(jnp.bfloat16)


def _const_spec(shape):
    nd = len(shape)
    return pl.BlockSpec(shape, lambda *_: (0,) * nd, pipeline_mode=pl.Buffered(1))


def _retention_layer(x2d, batch, seq, gain, w_in, w_out, gain2, wr, br):
    tb = RET_TILE
    n_s = seq // tb
    total = batch * seq
    rb = RET_BLOCK
    log_g = jnp.log(1.0 - jnp.exp2(-5.0 - jnp.arange(RET_HEADS, dtype=jnp.float32)))
    n = jnp.arange(rb, dtype=jnp.float32)
    diff = n[:, None] - n[None, :]
    cn = jnp.arange(rb)[:, None] // CHUNK
    cm = jnp.arange(rb)[None, :] // CHUNK
    expo = jnp.where(cn == cm, jnp.abs(diff), diff)
    dmat = jnp.where((cm <= cn)[None], jnp.exp(log_g[:, None, None] * expo[None]), 0.0)
    qdec = jnp.exp(log_g[:, None] * (n + 1.0)[None])[:, :, None]
    kdec = jnp.exp(log_g[:, None] * (rb - 1.0 - n)[None])[:, :, None]
    cdec = jnp.broadcast_to(jnp.exp(log_g * rb)[:, None, None], (RET_HEADS, 1, RET_DV))
    half = RET_DK // 2
    inv_freq = 1.0 / (ROPE_BASE ** (jnp.arange(half, dtype=jnp.float32) / half))
    ang = jnp.arange(seq, dtype=jnp.float32)[:, None] * inv_freq[None, :]
    cos, sin = jnp.cos(ang), jnp.sin(ang)

    tile = lambda b, s: (b * n_s + s, 0)
    return pl.pallas_call(
        _ret_kernel,
        grid=(batch, n_s),
        in_specs=[
            pl.BlockSpec((tb, D_MODEL), tile),
            _const_spec((1, D_MODEL)),
            _const_spec(w_in.shape),
            _const_spec(w_out.shape),
            pl.BlockSpec((tb, half), lambda b, s: (s, 0)),
            pl.BlockSpec((tb, half), lambda b, s: (s, 0)),
            _const_spec(dmat.shape),
            _const_spec(qdec.shape),
            _const_spec(kdec.shape),
            _const_spec(cdec.shape),
            _const_spec((1, D_MODEL)),
            _const_spec(wr.shape),
            _const_spec(br.shape),
            _const_spec((tb, tb)),
        ],
        out_specs=[
            pl.BlockSpec((tb * TILES_PER_ROW, LANES), tile),
            pl.BlockSpec((SUBLANES, tb), lambda b, s: (0, b * n_s + s)),
        ],
        out_shape=[
            jax.ShapeDtypeStruct((total * TILES_PER_ROW, LANES), jnp.float32),
            jax.ShapeDtypeStruct((SUBLANES, total), jnp.float32),
        ],
        scratch_shapes=[
            pltpu.VMEM((RET_HEADS, RET_DK, RET_DV), jnp.float32),
            pltpu.VMEM((tb, RET_V_DIM), jnp.bfloat16),
            pltpu.VMEM((LANES, 1), jnp.float32),
        ],
        compiler_params=pltpu.CompilerParams(
            dimension_semantics=("arbitrary", "arbitrary"),
            vmem_limit_bytes=VMEM_LIMIT),
        name="retention_layer",
    )(x2d, gain, w_in, w_out, cos, sin, dmat, qdec, kdec, cdec, gain2, wr, br, _tri(tb))


def _pool_kernel(x_ref, gain_ref, pw_ref, scale_ref, gain2_ref, wr_ref, br_ref, tri_ref,
                 x1_ref, route_ref, ext_ref, y_ref, cnt_ref):
    s = pl.program_id(1)
    tb = x_ref.shape[0] // TILES_PER_ROW

    @pl.when(s == 0)
    def _():
        ext_ref[0:POOL_HIST, :] = jnp.zeros((POOL_HIST, D_MODEL), jnp.float32)

    @pl.when(jnp.logical_and(pl.program_id(0) == 0, s == 0))
    def _():
        cnt_ref[...] = jnp.zeros_like(cnt_ref)

    x = _load_rows(x_ref, tb)
    ext_ref[POOL_HIST:, :] = _rms(x, gain_ref[...])
    pos = s * tb + lax.broadcasted_iota(jnp.int32, (tb, 1), 0)
    for g, w in enumerate(POOL_WINDOWS):
        cs = slice(g * POOL_GROUP_DIM, (g + 1) * POOL_GROUP_DIM)
        e = ext_ref[:, cs]
        acc = e
        k = 1
        while k < w:
            acc = acc + pltpu.roll(acc, k, axis=0)
            k *= 2
        inv_cnt = 1.0 / jnp.minimum(pos + 1, w).astype(jnp.float32)
        pooled = acc[POOL_HIST:, :] * inv_cnt - e[POOL_HIST:, :]
        y_ref[:, cs] = _dot(pooled.astype(jnp.bfloat16), pw_ref[g])
    ext_ref[0:POOL_HIST, :] = ext_ref[tb:tb + POOL_HIST, :]
    x1 = x + y_ref[...] * scale_ref[...]
    _store_rows(x1_ref, x1)
    route_ref[...] = _route(x1, gain2_ref[...], wr_ref[...], br_ref[...], tri_ref[...], cnt_ref)


def _pool_layer(x2d, batch, seq, gain, pw, scale, gain2, wr, br):
    tb = POOL_TILE
    n_s = seq // tb
    total = batch * seq
    tile = lambda b, s: (b * n_s + s, 0)
    return pl.pallas_call(
        _pool_kernel,
        grid=(batch, n_s),
        in_specs=[
            pl.BlockSpec((tb * TILES_PER_ROW, LANES), tile),
            _const_spec((1, D_MODEL)),
            _const_spec(pw.shape),
            _const_spec((1, D_MODEL)),
            _const_spec((1, D_MODEL)),
            _const_spec(wr.shape),
            _const_spec(br.shape),
            _const_spec((tb, tb)),
        ],
        out_specs=[
            pl.BlockSpec((tb * TILES_PER_ROW, LANES), tile),
            pl.BlockSpec((SUBLANES, tb), lambda b, s: (0, b * n_s + s)),
        ],
        out_shape=[
            jax.ShapeDtypeStruct((total * TILES_PER_ROW, LANES), jnp.float32),
            jax.ShapeDtypeStruct((SUBLANES, total), jnp.float32),
        ],
        scratch_shapes=[
            pltpu.VMEM((POOL_HIST + tb, D_MODEL), jnp.float32),
            pltpu.VMEM((tb, D_MODEL), jnp.float32),
            pltpu.VMEM((LANES, 1), jnp.float32),
        ],
        compiler_params=pltpu.CompilerParams(
            dimension_semantics=("arbitrary", "arbitrary"),
            vmem_limit_bytes=VMEM_LIMIT),
        name="pool_layer",
    )(x2d, gain, pw, scale, gain2, wr, br, _tri(tb))


def _tile_of(row):
    return pl.ds(pl.multiple_of(row * TILES_PER_ROW, TILES_PER_ROW), TILES_PER_ROW)


def _dispatch_kernel(slot_ref, run_start_ref, run_len_ref, n_used_ref,
                     x_ref, out_hbm, zero_ref, sem, sem_pad):
    i = pl.program_id(0)
    tile = x_ref.shape[0] // TILES_PER_ROW

    @pl.when(i == 0)
    def _():
        zero_ref[...] = jnp.zeros_like(zero_ref)

        def pad_copy(c):
            rows = pl.multiple_of(run_len_ref[c] * TILES_PER_ROW, TILES_PER_ROW)
            first = pl.multiple_of(run_start_ref[c] * TILES_PER_ROW, TILES_PER_ROW)
            return pltpu.make_async_copy(zero_ref.at[pl.ds(0, rows)], out_hbm.at[pl.ds(first, rows)],
                                         sem_pad.at[0])

        def block_copy(blk):
            first = pl.multiple_of(blk * zero_ref.shape[0], zero_ref.shape[0])
            return pltpu.make_async_copy(zero_ref, out_hbm.at[pl.ds(first, zero_ref.shape[0])],
                                         sem_pad.at[0])

        n_blocks = out_hbm.shape[0] // zero_ref.shape[0]
        for c in range(N_CLASSES):
            @pl.when(run_len_ref[c] > 0)
            def _():
                pad_copy(c).start()
        lax.fori_loop(n_used_ref[0], n_blocks, lambda blk, c: (block_copy(blk).start(), c)[1], 0)
        for c in range(N_CLASSES):
            @pl.when(run_len_ref[c] > 0)
            def _():
                pad_copy(c).wait()
        lax.fori_loop(n_used_ref[0], n_blocks, lambda blk, c: (block_copy(blk).wait(), c)[1], 0)

    for r in range(tile):
        slot = slot_ref[i * tile + r]
        pltpu.make_async_copy(x_ref.at[_tile_of(r)], out_hbm.at[_tile_of(slot)],
                              sem.at[0]).start(priority=r % DMA_THREADS)
    pltpu.make_async_copy(x_ref, out_hbm.at[pl.ds(0, x_ref.shape[0])], sem.at[0]).wait()


def _dispatch(x_tiled, slot_of_tok, run_start, run_len, n_used, n_slots):
    total = slot_of_tok.shape[0]
    tile = DISPATCH_TILE
    grid_spec = pltpu.PrefetchScalarGridSpec(
        num_scalar_prefetch=4,
        grid=(total // tile,),
        in_specs=[pl.BlockSpec((tile * TILES_PER_ROW, LANES), lambda i, *_: (i, 0))],
        out_specs=pl.BlockSpec(memory_space=pl.ANY),
        scratch_shapes=[
            pltpu.VMEM((MOE_ROWS * TILES_PER_ROW, LANES), jnp.float32),
            pltpu.SemaphoreType.DMA((1,)),
            pltpu.SemaphoreType.DMA((1,)),
        ],
    )
    return pl.pallas_call(
        _dispatch_kernel,
        grid_spec=grid_spec,
        out_shape=jax.ShapeDtypeStruct((n_slots * TILES_PER_ROW, LANES), jnp.float32),
        compiler_params=pltpu.CompilerParams(
            dimension_semantics=("arbitrary",),
            vmem_limit_bytes=VMEM_LIMIT),
        name="moe_dispatch",
    )(slot_of_tok, run_start, run_len, n_used, x_tiled)


def _moe_kernel(dst_ref, nv_ref, ea_ref, eb_ref, xstep_ref,
                x_ref, wts_ref, gain_ref, fgain_ref, *rest, final_norm, total):
    n_w = 2 * len(MOE_WEIGHTS) * MOE_STEP_BLOCKS
    w_refs = rest[:n_w]
    out_hbm, obuf, sem_out = rest[n_w:]
    i = pl.program_id(0)
    blk0 = i * MOE_STEP_BLOCKS
    n_first = nv_ref[blk0]
    n_first_prev = nv_ref[jnp.maximum(blk0 - MOE_STEP_BLOCKS, 0)]
    osl = i % MOE_OBUFS
    step_rows = MOE_STEP_BLOCKS * MOE_ROWS

    def counts(step):
        first = jnp.maximum(step, 0) * MOE_STEP_BLOCKS
        return [nv_ref[first + j] for j in range(MOE_STEP_BLOCKS)]

    def out_copy(sl, r, tok):
        if final_norm:
            return pltpu.make_async_copy(obuf.at[sl, pl.ds(r, 1)], out_hbm.at[pl.ds(tok, 1)],
                                         sem_out.at[sl])
        return pltpu.make_async_copy(obuf.at[sl, _tile_of(r)], out_hbm.at[_tile_of(tok)],
                                     sem_out.at[sl])

    def start_scatter(step, sl, rows):
        def unrolled(j):
            for r in range(j * MOE_ROWS, (j + 1) * MOE_ROWS):
                out_copy(sl, r, dst_ref[step * step_rows + r]).start(priority=r % DMA_THREADS)

        for j in range(MOE_STEP_BLOCKS):
            if not final_norm:
                unrolled(j)
                continue
            pl.when(rows[j] == MOE_ROWS)(functools.partial(unrolled, j))

            @pl.when(rows[j] < MOE_ROWS)
            def _():
                def body(r, c):
                    out_copy(sl, j * MOE_ROWS + r, dst_ref[step * step_rows + j * MOE_ROWS + r]).start()
                    return c
                lax.fori_loop(0, rows[j], body, 0)

    def wait_scatter(sl, rows):
        if not final_norm:
            pltpu.make_async_copy(obuf.at[sl], out_hbm.at[pl.ds(0, obuf.shape[1])],
                                  sem_out.at[sl]).wait()
            return
        for j in range(MOE_STEP_BLOCKS):
            tiled = pl.multiple_of((rows[j] // SUBLANES) * SUBLANES, SUBLANES)

            @pl.when(tiled > 0)
            def _():
                pltpu.make_async_copy(obuf.at[sl, pl.ds(0, tiled)], out_hbm.at[pl.ds(0, tiled)],
                                      sem_out.at[sl]).wait()

            def body(r, c):
                out_copy(sl, 0, 0).wait()
                return c
            lax.fori_loop(tiled, rows[j], body, 0)

    if not final_norm:
        @pl.when(i == 0)
        def _():
            obuf[...] = jnp.zeros(obuf.shape, jnp.float32)
            for sl in range(MOE_OBUFS):
                for r in range(step_rows):
                    out_copy(sl, r, total + sl * step_rows + r).start()

    @pl.when(n_first > 0)
    def _():
        tile_rows = MOE_ROWS * TILES_PER_ROW
        blocks = range(MOE_STEP_BLOCKS)
        xs = [_load_rows(x_ref.at[pl.ds(j * tile_rows, tile_rows)], MOE_ROWS) for j in blocks]
        hs = [_rms(x, gain_ref[...]).astype(jnp.bfloat16) for x in xs]
        wts = [wts_ref[j].T for j in blocks]
        def weight(j, e, kind):
            return w_refs[(2 * j + e) * len(MOE_WEIGHTS) + MOE_WEIGHTS.index(kind)][0, 0]

        experts = [(j, e) for j in blocks for e in range(2)]
        gates = {je: _dot(hs[je[0]], weight(*je, "gate")) for je in experts}
        ups = {je: _dot(hs[je[0]], weight(*je, "up")) for je in experts}
        hid = {je: (_silu(gates[je]) * ups[je]).astype(jnp.bfloat16) for je in experts}
        outs = [xs[j] + wts[j][:, 0:1] * _dot(hid[j, 0], weight(j, 0, "down"))
                + wts[j][:, 1:2] * _dot(hid[j, 1], weight(j, 1, "down")) for j in blocks]
        if final_norm:
            pl.when(i >= MOE_OBUFS)(lambda: wait_scatter(osl, counts(i - MOE_OBUFS)))
            for j in blocks:
                obuf[osl, j * MOE_ROWS:(j + 1) * MOE_ROWS] = _rms(outs[j], fgain_ref[...])
        else:
            wait_scatter(osl, None)
            for j in blocks:
                _store_rows(obuf.at[osl, pl.ds(j * tile_rows, tile_rows)], outs[j])
        start_scatter(i, osl, counts(i))

    @pl.when(jnp.logical_and(n_first == 0, n_first_prev > 0))
    def _():
        if final_norm:
            wait_scatter(1 - osl, counts(i - 1))
            pl.when(i >= MOE_OBUFS)(lambda: wait_scatter(osl, counts(i - MOE_OBUFS)))
        else:
            for sl in range(MOE_OBUFS):
                wait_scatter(sl, None)


def _class_tables():
    ea = np.zeros((N_CLASSES,), np.int32)
    eb = np.zeros((N_CLASSES,), np.int32)
    for g in range(MOE_GROUPS):
        c = g * PAIRS_PER_GROUP
        for a in range(MOE_EPG):
            for b in range(a + 1, MOE_EPG):
                ea[c], eb[c] = g * MOE_EPG + a, g * MOE_EPG + b
                c += 1
    return ea, eb


def _moe_layer(x_tiled, route, gain, fgain, weights, layer, final_norm):
    total = route.shape[1]
    nb = (total // MOE_ROWS + N_CLASSES) // MOE_STEP_BLOCKS * MOE_STEP_BLOCKS + MOE_STEP_BLOCKS
    n_steps = nb // MOE_STEP_BLOCKS
    step_rows = MOE_STEP_BLOCKS * MOE_ROWS
    n_slots = nb * MOE_ROWS
    w_lo, w_hi = route[2], route[3]
    rank = route[4].astype(jnp.int32)
    cls = route[5].astype(jnp.int32)
    cls_ids = jnp.arange(N_CLASSES, dtype=jnp.int32)
    onehot_tok = (cls[:, None] == cls_ids[None, :]).astype(jnp.int32)
    counts = jnp.sum(onehot_tok, axis=0)
    nblk = (counts + MOE_ROWS - 1) // MOE_ROWS
    npad = nblk * MOE_ROWS - counts
    blk_end = jnp.cumsum(nblk)
    cls_start = (blk_end - nblk) * MOE_ROWS
    slot_of_tok = jnp.sum(onehot_tok * cls_start[None, :], axis=1) + rank
    x_sorted = _dispatch(x_tiled, slot_of_tok, cls_start + counts, npad, blk_end[-1:], n_slots)

    max_pad = MOE_ROWS - 1
    cand_cls = jnp.repeat(cls_ids, max_pad)
    cand_on = jnp.tile(jnp.arange(max_pad, dtype=jnp.int32), N_CLASSES) < jnp.repeat(npad, max_pad)
    n_fill = n_slots - total - N_CLASSES * max_pad
    tok_bits = 16
    assert total <= 1 << tok_bits
    keys = jnp.concatenate([
        (2 * cls << tok_bits) + jnp.arange(total, dtype=jnp.int32),
        jnp.where(cand_on, 2 * cand_cls + 1, 2 * N_CLASSES + 1) << tok_bits,
        jnp.full((n_fill,), (2 * N_CLASSES + 1) << tok_bits, jnp.int32)])
    zeros = jnp.zeros((n_slots - total,), jnp.float32)
    key_s, wlo_s, whi_s = lax.sort(
        (keys, jnp.concatenate([w_lo, zeros]), jnp.concatenate([w_hi, zeros])),
        num_keys=1, is_stable=False)
    is_pad = ((key_s >> tok_bits) & 1) == 1
    tok_s = key_s & ((1 << tok_bits) - 1)
    slot = jnp.arange(n_slots, dtype=jnp.int32)
    dump = total + ((slot // step_rows) % MOE_OBUFS) * step_rows + slot % step_rows
    dst = jnp.where(is_pad, dump, tok_s)
    nvalid = jnp.sum((~is_pad).reshape(nb, MOE_ROWS).astype(jnp.int32), axis=1)
    wts = jnp.concatenate(
        [wlo_s.reshape(nb, 1, MOE_ROWS), whi_s.reshape(nb, 1, MOE_ROWS),
         jnp.zeros((nb, SUBLANES - 2, MOE_ROWS), jnp.float32)], axis=1)
    bidx = jnp.arange(nb, dtype=jnp.int32)
    xblk = jnp.minimum(bidx, blk_end[-1] - 1)
    c_blk = jnp.sum((blk_end[None, :] <= xblk[:, None]).astype(jnp.int32), axis=1)
    xstep = jnp.minimum(jnp.arange(n_steps, dtype=jnp.int32), (blk_end[-1] - 1) // MOE_STEP_BLOCKS)
    ea_tab, eb_tab = _class_tables()
    onehot_blk = (c_blk[:, None] == cls_ids[None, :]).astype(jnp.int32)
    ea = jnp.sum(onehot_blk * jnp.asarray(ea_tab)[None, :], axis=1)
    eb = jnp.sum(onehot_blk * jnp.asarray(eb_tab)[None, :], axis=1)

    tiled_block = (step_rows * TILES_PER_ROW, LANES)
    if final_norm:
        out_block = (step_rows, D_MODEL)
        out_shape = (total, D_MODEL)
    else:
        out_block = tiled_block
        out_shape = ((total + MOE_OBUFS * step_rows) * TILES_PER_ROW, LANES)

    def weight_specs(j):
        def spec(shape, tab):
            return pl.BlockSpec(shape, lambda i, dst, nv, ea, eb, xstep:
                                (layer, (ea, eb)[tab][i * MOE_STEP_BLOCKS + j], 0, 0))
        shapes = {"gate": (1, 1, D_MODEL, MOE_D_FF), "up": (1, 1, D_MODEL, MOE_D_FF),
                  "down": (1, 1, MOE_D_FF, D_MODEL)}
        return [spec(shapes[kind], tab) for tab in range(2) for kind in MOE_WEIGHTS]

    grid_spec = pltpu.PrefetchScalarGridSpec(
        num_scalar_prefetch=5,
        grid=(n_steps,),
        in_specs=[
            pl.BlockSpec(tiled_block, lambda i, dst, nv, ea, eb, xstep: (xstep[i], 0)),
            pl.BlockSpec((MOE_STEP_BLOCKS, SUBLANES, MOE_ROWS), lambda i, *_: (i, 0, 0)),
            pl.BlockSpec((1, D_MODEL), lambda i, *_: (0, 0)),
            pl.BlockSpec((1, D_MODEL), lambda i, *_: (0, 0)),
        ] + [s for j in range(MOE_STEP_BLOCKS) for s in weight_specs(j)],
        out_specs=pl.BlockSpec(memory_space=pl.ANY),
        scratch_shapes=[
            pltpu.VMEM((MOE_OBUFS,) + out_block, jnp.float32),
            pltpu.SemaphoreType.DMA((MOE_OBUFS,)),
        ],
    )
    return pl.pallas_call(
        functools.partial(_moe_kernel, final_norm=final_norm, total=total),
        grid_spec=grid_spec,
        out_shape=jax.ShapeDtypeStruct(out_shape, jnp.float32),
        compiler_params=pltpu.CompilerParams(
            dimension_semantics=("arbitrary",),
            vmem_limit_bytes=VMEM_LIMIT),
        name="moe_layer",
    )(dst, nvalid, ea, eb, xstep, x_sorted, wts, gain, fgain, *(list(weights) * (2 * MOE_STEP_BLOCKS)))


def _router_params(w_group, b_group, w_expert, b_expert):
    gap = ROUTE_EXPERT_COL - MOE_GROUPS
    pad = LANES - ROUTE_EXPERT_COL - MOE_EXPERTS
    wr = jnp.concatenate([w_group, jnp.zeros((D_MODEL, gap), jnp.float32), w_expert,
                          jnp.zeros((D_MODEL, pad), jnp.float32)], axis=1)
    br = jnp.concatenate([b_group, jnp.zeros((gap,), jnp.float32), b_expert.reshape(-1),
                          jnp.zeros((pad,), jnp.float32)])[None, :]
    wr_hi = wr.astype(jnp.bfloat16)
    wr_lo = (wr - wr_hi.astype(jnp.float32)).astype(jnp.bfloat16)
    return jnp.concatenate([wr_hi, wr_hi, wr_lo], axis=0), br


def kernel(x, norm_mix, norm_ffn, ret_w_in, ret_w_out, pool_w, pool_scale, moe_w_group, moe_b_group, moe_w_expert, moe_b_expert, moe_w_gate, moe_w_up, moe_w_down, final_norm):
    batch, seq, _ = x.shape
    x2d = x.reshape(batch * seq, D_MODEL)
    bf = jnp.bfloat16
    weights = (moe_w_gate.astype(bf), moe_w_up.astype(bf), moe_w_down.astype(bf))
    fgain = final_norm[None, :]

    wr0, br0 = _router_params(moe_w_group[0], moe_b_group[0], moe_w_expert[0], moe_b_expert[0])
    x1, route0 = _retention_layer(x2d, batch, seq, norm_mix[0][None, :], ret_w_in[0].astype(bf),
                                  ret_w_out[0].astype(bf), norm_ffn[0][None, :], wr0, br0)
    x2 = _moe_layer(x1, route0, norm_ffn[0][None, :], fgain, weights, 0, final_norm=False)

    wr1, br1 = _router_params(moe_w_group[1], moe_b_group[1], moe_w_expert[1], moe_b_expert[1])
    x3, route1 = _pool_layer(x2, batch, seq, norm_mix[1][None, :], pool_w[0].astype(bf),
                             pool_scale[0][None, :], norm_ffn[1][None, :], wr1, br1)
    out = _moe_layer(x3, route1, norm_ffn[1][None, :], fgain, weights, 1, final_norm=True)
    return out.reshape(batch, seq, D_MODEL)
```

```python
import functools

import numpy as np
import jax
import jax.numpy as jnp
from jax import lax
from jax.experimental import pallas as pl
from jax.experimental.pallas import tpu as pltpu

D_MODEL = 1024
EPS = 1e-6
CHUNK = 64

RET_HEADS = 4
RET_DK = 256
RET_DV = 512
RET_QK_DIM = RET_HEADS * RET_DK
RET_V_DIM = RET_HEADS * RET_DV
ROPE_BASE = 10000.0

POOL_WINDOWS = (2, 4, 8, 16)
POOL_GROUP_DIM = 256
POOL_HIST = 16

MOE_GROUPS = 4
MOE_EPG = 8
MOE_EXPERTS = MOE_GROUPS * MOE_EPG
MOE_D_FF = 256
PAIRS_PER_GROUP = MOE_EPG * (MOE_EPG - 1) // 2
N_CLASSES = MOE_GROUPS * PAIRS_PER_GROUP

LANES = 128
SUBLANES = 8

RET_TILE = 256
RET_BLOCK = 256
POOL_TILE = 1024
MOE_ROWS = 128
MOE_STEP_BLOCKS = 2
MOE_WEIGHTS = ("gate", "up", "down")
DISPATCH_TILE = 2048
MOE_OBUFS = 2
DMA_THREADS = 2
VMEM_LIMIT = 56 * 1024 * 1024


def _rms(x, gain):
    ms = jnp.mean(x * x, axis=-1, keepdims=True)
    return x * lax.rsqrt(ms + EPS) * gain


def _silu(x):
    return x * jax.nn.sigmoid(x)


def _dot(a, b):
    return jnp.dot(a, b, preferred_element_type=jnp.float32)


ROUTE_EXPERT_COL = 8
TILES_PER_ROW = D_MODEL // LANES


def _load_rows(ref, rows):
    return jnp.concatenate(
        [ref[pl.ds(s, rows, stride=TILES_PER_ROW), :] for s in range(TILES_PER_ROW)], axis=1)


def _store_rows(ref, val):
    rows = val.shape[0]
    for s in range(TILES_PER_ROW):
        ref[pl.ds(s, rows, stride=TILES_PER_ROW), :] = val[:, s * LANES:(s + 1) * LANES]


def _route(x1, gain2, wr2, br, tri, cnt_ref):
    rows = x1.shape[0]
    h2 = _rms(x1, gain2)
    h_hi = h2.astype(jnp.bfloat16)
    h_lo = (h2 - h_hi.astype(jnp.float32)).astype(jnp.bfloat16)
    parts = _dot(jnp.concatenate([h_hi, h_lo], axis=0), wr2)
    logits = (parts[:rows, :LANES] + parts[rows:, :LANES]) + parts[:rows, LANES:] + br
    lt = logits.T
    sub = lax.broadcasted_iota(jnp.int32, (SUBLANES, rows), 0)
    neg = jnp.float32(-jnp.inf)
    gl = jnp.where(sub < MOE_GROUPS, lt[0:SUBLANES], neg)
    gmax = jnp.max(gl, axis=0, keepdims=True)
    grp = jnp.min(jnp.where(gl == gmax, sub, SUBLANES), axis=0, keepdims=True)
    g_gate = 1.0 / jnp.sum(jnp.exp(gl - gmax), axis=0, keepdims=True)
    el = lt[ROUTE_EXPERT_COL:ROUTE_EXPERT_COL + MOE_EPG]
    for g in range(1, MOE_GROUPS):
        first = ROUTE_EXPERT_COL + g * MOE_EPG
        el = jnp.where(grp == g, lt[first:first + MOE_EPG], el)
    m1 = jnp.max(el, axis=0, keepdims=True)
    i1 = jnp.min(jnp.where(el == m1, sub, SUBLANES), axis=0, keepdims=True)
    el2 = jnp.where(sub == i1, neg, el)
    m2 = jnp.max(el2, axis=0, keepdims=True)
    i2 = jnp.min(jnp.where(el2 == m2, sub, SUBLANES), axis=0, keepdims=True)
    t = jnp.exp(m2 - m1)
    w1 = g_gate / (1.0 + t)
    w2 = g_gate * t / (1.0 + t)
    first_lo = i1 < i2
    a = jnp.where(first_lo, i1, i2)
    bb = jnp.where(first_lo, i2, i1)
    base = grp * MOE_EPG
    lo = (base + a).astype(jnp.float32)
    hi = (base + bb).astype(jnp.float32)
    w_lo = jnp.where(first_lo, w1, w2)
    w_hi = jnp.where(first_lo, w2, w1)
    cls = grp * PAIRS_PER_GROUP + ((a * (2 * MOE_EPG - 1 - a)) >> 1) + (bb - a - 1)
    onehot = (lax.broadcasted_iota(jnp.int32, (LANES, rows), 0) == cls).astype(jnp.float32)
    earlier = _dot(onehot.astype(jnp.bfloat16), tri) + cnt_ref[...]
    rank = jnp.sum(onehot * earlier, axis=0, keepdims=True)
    cnt_ref[...] += jnp.sum(onehot, axis=1, keepdims=True)
    out = jnp.where(sub == 0, lo, jnp.where(sub == 1, hi, jnp.where(sub == 2, w_lo,
                    jnp.where(sub == 3, w_hi, 0.0))))
    return jnp.where(sub == 4, rank, jnp.where(sub == 5, cls.astype(jnp.float32), out))


def _ret_kernel(x_ref, gain_ref, win_ref, wout_ref, cos_ref, sin_ref, dmat_ref,
                qdec_ref, kdec_ref, cdec_ref, gain2_ref, wr_ref, br_ref, tri_ref,
                x1_ref, route_ref, state_ref, gated_ref, cnt_ref):
    @pl.when(pl.program_id(1) == 0)
    def _():
        state_ref[...] = jnp.zeros_like(state_ref)

    @pl.when(jnp.logical_and(pl.program_id(0) == 0, pl.program_id(1) == 0))
    def _():
        cnt_ref[...] = jnp.zeros_like(cnt_ref)

    x = x_ref[...]
    h = _rms(x, gain_ref[...]).astype(jnp.bfloat16)
    cos = cos_ref[...]
    sin = sin_ref[...]
    half = RET_DK // 2

    def rope(t):
        t1, t2 = t[:, :half], t[:, half:]
        return jnp.concatenate([t1 * cos - t2 * sin, t1 * sin + t2 * cos], axis=1)

    for hd in range(RET_HEADS):
        q = _dot(h, win_ref[:, hd * RET_DK:(hd + 1) * RET_DK])
        k = _dot(h, win_ref[:, RET_QK_DIM + hd * RET_DK:RET_QK_DIM + (hd + 1) * RET_DK])
        v0 = 2 * RET_QK_DIM + hd * RET_DV
        v = _dot(h, win_ref[:, v0:v0 + RET_DV])
        g0 = 2 * RET_QK_DIM + RET_V_DIM + hd * RET_DV
        g = _dot(h, win_ref[:, g0:g0 + RET_DV])
        q = rope(q)
        k = rope(k) * (RET_DK ** -0.5)
        for blk in range(x.shape[0] // RET_BLOCK):
            rs = slice(blk * RET_BLOCK, (blk + 1) * RET_BLOCK)
            qb = q[rs].astype(jnp.bfloat16)
            kb = k[rs].astype(jnp.bfloat16)
            vb = v[rs].astype(jnp.bfloat16)
            sc = lax.dot_general(qb, kb, (((1,), (1,)), ((), ())),
                                 preferred_element_type=jnp.float32)
            sc = sc * dmat_ref[hd]
            intra = _dot(sc.astype(jnp.bfloat16), vb)
            st = state_ref[hd]
            cross = _dot(qb, st.astype(jnp.bfloat16)) * qdec_ref[hd]
            o = intra + cross
            kd = (k[rs] * kdec_ref[hd]).astype(jnp.bfloat16)
            upd = lax.dot_general(kd, vb, (((0,), (0,)), ((), ())),
                                  preferred_element_type=jnp.float32)
            state_ref[hd] = st * cdec_ref[hd] + upd
            mu = jnp.mean(o, axis=-1, keepdims=True)
            oc = o - mu
            var = jnp.mean(oc * oc, axis=-1, keepdims=True)
            on = oc * lax.rsqrt(var + EPS)
            gated_ref[rs, hd * RET_DV:(hd + 1) * RET_DV] = (_silu(g[rs]) * on).astype(jnp.bfloat16)

    x1 = x + _dot(gated_ref[...], wout_ref[...])
    _store_rows(x1_ref, x1)
    route_ref[...] = _route(x1, gain2_ref[...], wr_ref[...], br_ref[...], tri_ref[...], cnt_ref)


def _tri(n):
    i = jnp.arange(n)
    return (i[:, None] < i[None, :]).astype(jnp.bfloat16)


def _const_spec(shape):
    nd = len(shape)
    return pl.BlockSpec(shape, lambda *_: (0,) * nd, pipeline_mode=pl.Buffered(1))


def _retention_layer(x2d, batch, seq, gain, w_in, w_out, gain2, wr, br):
    tb = RET_TILE
    n_s = seq // tb
    total = batch * seq
    rb = RET_BLOCK
    log_g = jnp.log(1.0 - jnp.exp2(-5.0 - jnp.arange(RET_HEADS, dtype=jnp.float32)))
    n = jnp.arange(rb, dtype=jnp.float32)
    diff = n[:, None] - n[None, :]
    cn = jnp.arange(rb)[:, None] // CHUNK
    cm = jnp.arange(rb)[None, :] // CHUNK
    expo = jnp.where(cn == cm, jnp.abs(diff), diff)
    dmat = jnp.where((cm <= cn)[None], jnp.exp(log_g[:, None, None] * expo[None]), 0.0)
    qdec = jnp.exp(log_g[:, None] * (n + 1.0)[None])[:, :, None]
    kdec = jnp.exp(log_g[:, None] * (rb - 1.0 - n)[None])[:, :, None]
    cdec = jnp.broadcast_to(jnp.exp(log_g * rb)[:, None, None], (RET_HEADS, 1, RET_DV))
    half = RET_DK // 2
    inv_freq = 1.0 / (ROPE_BASE ** (jnp.arange(half, dtype=jnp.float32) / half))
    ang = jnp.arange(seq, dtype=jnp.float32)[:, None] * inv_freq[None, :]
    cos, sin = jnp.cos(ang), jnp.sin(ang)

    tile = lambda b, s: (b * n_s + s, 0)
    return pl.pallas_call(
        _ret_kernel,
        grid=(batch, n_s),
        in_specs=[
            pl.BlockSpec((tb, D_MODEL), tile),
            _const_spec((1, D_MODEL)),
            _const_spec(w_in.shape),
            _const_spec(w_out.shape),
            pl.BlockSpec((tb, half), lambda b, s: (s, 0)),
            pl.BlockSpec((tb, half), lambda b, s: (s, 0)),
            _const_spec(dmat.shape),
            _const_spec(qdec.shape),
            _const_spec(kdec.shape),
            _const_spec(cdec.shape),
            _const_spec((1, D_MODEL)),
            _const_spec(wr.shape),
            _const_spec(br.shape),
            _const_spec((tb, tb)),
        ],
        out_specs=[
            pl.BlockSpec((tb * TILES_PER_ROW, LANES), tile),
            pl.BlockSpec((SUBLANES, tb), lambda b, s: (0, b * n_s + s)),
        ],
        out_shape=[
            jax.ShapeDtypeStruct((total * TILES_PER_ROW, LANES), jnp.float32),
            jax.ShapeDtypeStruct((SUBLANES, total), jnp.float32),
        ],
        scratch_shapes=[
            pltpu.VMEM((RET_HEADS, RET_DK, RET_DV), jnp.float32),
            pltpu.VMEM((tb, RET_V_DIM), jnp.bfloat16),
            pltpu.VMEM((LANES, 1), jnp.float32),
        ],
        compiler_params=pltpu.CompilerParams(
            dimension_semantics=("arbitrary", "arbitrary"),
            vmem_limit_bytes=VMEM_LIMIT),
        name="retention_layer",
    )(x2d, gain, w_in, w_out, cos, sin, dmat, qdec, kdec, cdec, gain2, wr, br, _tri(tb))


def _pool_kernel(x_ref, gain_ref, pw_ref, scale_ref, gain2_ref, wr_ref, br_ref, tri_ref,
                 x1_ref, route_ref, ext_ref, y_ref, cnt_ref):
    s = pl.program_id(1)
    tb = x_ref.shape[0] // TILES_PER_ROW

    @pl.when(s == 0)
    def _():
        ext_ref[0:POOL_HIST, :] = jnp.zeros((POOL_HIST, D_MODEL), jnp.float32)

    @pl.when(jnp.logical_and(pl.program_id(0) == 0, s == 0))
    def _():
        cnt_ref[...] = jnp.zeros_like(cnt_ref)

    x = _load_rows(x_ref, tb)
    ext_ref[POOL_HIST:, :] = _rms(x, gain_ref[...])
    pos = s * tb + lax.broadcasted_iota(jnp.int32, (tb, 1), 0)
    for g, w in enumerate(POOL_WINDOWS):
        cs = slice(g * POOL_GROUP_DIM, (g + 1) * POOL_GROUP_DIM)
        e = ext_ref[:, cs]
        acc = e
        k = 1
        while k < w:
            acc = acc + pltpu.roll(acc, k, axis=0)
            k *= 2
        inv_cnt = 1.0 / jnp.minimum(pos + 1, w).astype(jnp.float32)
        pooled = acc[POOL_HIST:, :] * inv_cnt - e[POOL_HIST:, :]
        y_ref[:, cs] = _dot(pooled.astype(jnp.bfloat16), pw_ref[g])
    ext_ref[0:POOL_HIST, :] = ext_ref[tb:tb + POOL_HIST, :]
    x1 = x + y_ref[...] * scale_ref[...]
    _store_rows(x1_ref, x1)
    route_ref[...] = _route(x1, gain2_ref[...], wr_ref[...], br_ref[...], tri_ref[...], cnt_ref)


def _pool_layer(x2d, batch, seq, gain, pw, scale, gain2, wr, br):
    tb = POOL_TILE
    n_s = seq // tb
    total = batch * seq
    tile = lambda b, s: (b * n_s + s, 0)
    return pl.pallas_call(
        _pool_kernel,
        grid=(batch, n_s),
        in_specs=[
            pl.BlockSpec((tb * TILES_PER_ROW, LANES), tile),
            _const_spec((1, D_MODEL)),
            _const_spec(pw.shape),
            _const_spec((1, D_MODEL)),
            _const_spec((1, D_MODEL)),
            _const_spec(wr.shape),
            _const_spec(br.shape),
            _const_spec((tb, tb)),
        ],
        out_specs=[
            pl.BlockSpec((tb * TILES_PER_ROW, LANES), tile),
            pl.BlockSpec((SUBLANES, tb), lambda b, s: (0, b * n_s + s)),
        ],
        out_shape=[
            jax.ShapeDtypeStruct((total * TILES_PER_ROW, LANES), jnp.float32),
            jax.ShapeDtypeStruct((SUBLANES, total), jnp.float32),
        ],
        scratch_shapes=[
            pltpu.VMEM((POOL_HIST + tb, D_MODEL), jnp.float32),
            pltpu.VMEM((tb, D_MODEL), jnp.float32),
            pltpu.VMEM((LANES, 1), jnp.float32),
        ],
        compiler_params=pltpu.CompilerParams(
            dimension_semantics=("arbitrary", "arbitrary"),
            vmem_limit_bytes=VMEM_LIMIT),
        name="pool_layer",
    )(x2d, gain, pw, scale, gain2, wr, br, _tri(tb))


def _tile_of(row):
    return pl.ds(pl.multiple_of(row * TILES_PER_ROW, TILES_PER_ROW), TILES_PER_ROW)


def _dispatch_kernel(slot_ref, run_start_ref, run_len_ref, n_used_ref,
                     x_ref, out_hbm, zero_ref, sem, sem_pad):
    i = pl.program_id(0)
    tile = x_ref.shape[0] // TILES_PER_ROW

    @pl.when(i == 0)
    def _():
        zero_ref[...] = jnp.zeros_like(zero_ref)

        def pad_copy(c):
            rows = pl.multiple_of(run_len_ref[c] * TILES_PER_ROW, TILES_PER_ROW)
            first = pl.multiple_of(run_start_ref[c] * TILES_PER_ROW, TILES_PER_ROW)
            return pltpu.make_async_copy(zero_ref.at[pl.ds(0, rows)], out_hbm.at[pl.ds(first, rows)],
                                         sem_pad.at[0])

        def block_copy(blk):
            first = pl.multiple_of(blk * zero_ref.shape[0], zero_ref.shape[0])
            return pltpu.make_async_copy(zero_ref, out_hbm.at[pl.ds(first, zero_ref.shape[0])],
                                         sem_pad.at[0])

        n_blocks = out_hbm.shape[0] // zero_ref.shape[0]
        for c in range(N_CLASSES):
            @pl.when(run_len_ref[c] > 0)
            def _():
                pad_copy(c).start()
        lax.fori_loop(n_used_ref[0], n_blocks, lambda blk, c: (block_copy(blk).start(), c)[1], 0)
        for c in range(N_CLASSES):
            @pl.when(run_len_ref[c] > 0)
            def _():
                pad_copy(c).wait()
        lax.fori_loop(n_used_ref[0], n_blocks, lambda blk, c: (block_copy(blk).wait(), c)[1], 0)

    for r in range(tile):
        slot = slot_ref[i * tile + r]
        pltpu.make_async_copy(x_ref.at[_tile_of(r)], out_hbm.at[_tile_of(slot)],
                              sem.at[0]).start(priority=r % DMA_THREADS)
    pltpu.make_async_copy(x_ref, out_hbm.at[pl.ds(0, x_ref.shape[0])], sem.at[0]).wait()


def _dispatch(x_tiled, slot_of_tok, run_start, run_len, n_used, n_slots):
    total = slot_of_tok.shape[0]
    tile = DISPATCH_TILE
    grid_spec = pltpu.PrefetchScalarGridSpec(
        num_scalar_prefetch=4,
        grid=(total // tile,),
        in_specs=[pl.BlockSpec((tile * TILES_PER_ROW, LANES), lambda i, *_: (i, 0))],
        out_specs=pl.BlockSpec(memory_space=pl.ANY),
        scratch_shapes=[
            pltpu.VMEM((MOE_ROWS * TILES_PER_ROW, LANES), jnp.float32),
            pltpu.SemaphoreType.DMA((1,)),
            pltpu.SemaphoreType.DMA((1,)),
        ],
    )
    return pl.pallas_call(
        _dispatch_kernel,
        grid_spec=grid_spec,
        out_shape=jax.ShapeDtypeStruct((n_slots * TILES_PER_ROW, LANES), jnp.float32),
        compiler_params=pltpu.CompilerParams(
            dimension_semantics=("arbitrary",),
            vmem_limit_bytes=VMEM_LIMIT),
        name="moe_dispatch",
    )(slot_of_tok, run_start, run_len, n_used, x_tiled)


def _moe_kernel(dst_ref, nv_ref, ea_ref, eb_ref, xstep_ref,
                x_ref, wts_ref, gain_ref, fgain_ref, *rest, final_norm, total):
    n_w = 2 * len(MOE_WEIGHTS) * MOE_STEP_BLOCKS
    w_refs = rest[:n_w]
    out_hbm, obuf, sem_out = rest[n_w:]
    i = pl.program_id(0)
    blk0 = i * MOE_STEP_BLOCKS
    n_first = nv_ref[blk0]
    n_first_prev = nv_ref[jnp.maximum(blk0 - MOE_STEP_BLOCKS, 0)]
    osl = i % MOE_OBUFS
    step_rows = MOE_STEP_BLOCKS * MOE_ROWS

    def counts(step):
        first = jnp.maximum(step, 0) * MOE_STEP_BLOCKS
        return [nv_ref[first + j] for j in range(MOE_STEP_BLOCKS)]

    def out_copy(sl, r, tok):
        if final_norm:
            return pltpu.make_async_copy(obuf.at[sl, pl.ds(r, 1)], out_hbm.at[pl.ds(tok, 1)],
                                         sem_out.at[sl])
        return pltpu.make_async_copy(obuf.at[sl, _tile_of(r)], out_hbm.at[_tile_of(tok)],
                                     sem_out.at[sl])

    def start_scatter(step, sl, rows):
        def unrolled(j):
            for r in range(j * MOE_ROWS, (j + 1) * MOE_ROWS):
                out_copy(sl, r, dst_ref[step * step_rows + r]).start(priority=r % DMA_THREADS)

        for j in range(MOE_STEP_BLOCKS):
            if not final_norm:
                unrolled(j)
                continue
            pl.when(rows[j] == MOE_ROWS)(functools.partial(unrolled, j))

            @pl.when(rows[j] < MOE_ROWS)
            def _():
                def body(r, c):
                    out_copy(sl, j * MOE_ROWS + r, dst_ref[step * step_rows + j * MOE_ROWS + r]).start()
                    return c
                lax.fori_loop(0, rows[j], body, 0)

    def wait_scatter(sl, rows):
        if not final_norm:
            pltpu.make_async_copy(obuf.at[sl], out_hbm.at[pl.ds(0, obuf.shape[1])],
                                  sem_out.at[sl]).wait()
            return
        for j in range(MOE_STEP_BLOCKS):
            tiled = pl.multiple_of((rows[j] // SUBLANES) * SUBLANES, SUBLANES)

            @pl.when(tiled > 0)
            def _():
                pltpu.make_async_copy(obuf.at[sl, pl.ds(0, tiled)], out_hbm.at[pl.ds(0, tiled)],
                                      sem_out.at[sl]).wait()

            def body(r, c):
                out_copy(sl, 0, 0).wait()
                return c
            lax.fori_loop(tiled, rows[j], body, 0)

    if not final_norm:
        @pl.when(i == 0)
        def _():
            obuf[...] = jnp.zeros(obuf.shape, jnp.float32)
            for sl in range(MOE_OBUFS):
                for r in range(step_rows):
                    out_copy(sl, r, total + sl * step_rows + r).start()

    @pl.when(n_first > 0)
    def _():
        tile_rows = MOE_ROWS * TILES_PER_ROW
        blocks = range(MOE_STEP_BLOCKS)
        xs = [_load_rows(x_ref.at[pl.ds(j * tile_rows, tile_rows)], MOE_ROWS) for j in blocks]
        hs = [_rms(x, gain_ref[...]).astype(jnp.bfloat16) for x in xs]
        wts = [wts_ref[j].T for j in blocks]
        def weight(j, e, kind):
            return w_refs[(2 * j + e) * len(MOE_WEIGHTS) + MOE_WEIGHTS.index(kind)][0, 0]

        experts = [(j, e) for j in blocks for e in range(2)]
        gates = {je: _dot(hs[je[0]], weight(*je, "gate")) for je in experts}
        ups = {je: _dot(hs[je[0]], weight(*je, "up")) for je in experts}
        hid = {je: (_silu(gates[je]) * ups[je]).astype(jnp.bfloat16) for je in experts}
        outs = [xs[j] + wts[j][:, 0:1] * _dot(hid[j, 0], weight(j, 0, "down"))
                + wts[j][:, 1:2] * _dot(hid[j, 1], weight(j, 1, "down")) for j in blocks]
        if final_norm:
            pl.when(i >= MOE_OBUFS)(lambda: wait_scatter(osl, counts(i - MOE_OBUFS)))
            for j in blocks:
                obuf[osl, j * MOE_ROWS:(j + 1) * MOE_ROWS] = _rms(outs[j], fgain_ref[...])
        else:
            wait_scatter(osl, None)
            for j in blocks:
                _store_rows(obuf.at[osl, pl.ds(j * tile_rows, tile_rows)], outs[j])
        start_scatter(i, osl, counts(i))

    @pl.when(jnp.logical_and(n_first == 0, n_first_prev > 0))
    def _():
        if final_norm:
            wait_scatter(1 - osl, counts(i - 1))
            pl.when(i >= MOE_OBUFS)(lambda: wait_scatter(osl, counts(i - MOE_OBUFS)))
        else:
            for sl in range(MOE_OBUFS):
                wait_scatter(sl, None)


def _class_tables():
    ea = np.zeros((N_CLASSES,), np.int32)
    eb = np.zeros((N_CLASSES,), np.int32)
    for g in range(MOE_GROUPS):
        c = g * PAIRS_PER_GROUP
        for a in range(MOE_EPG):
            for b in range(a + 1, MOE_EPG):
                ea[c], eb[c] = g * MOE_EPG + a, g * MOE_EPG + b
                c += 1
    return ea, eb


def _moe_layer(x_tiled, route, gain, fgain, weights, layer, final_norm):
    total = route.shape[1]
    nb = (total // MOE_ROWS + N_CLASSES) // MOE_STEP_BLOCKS * MOE_STEP_BLOCKS + MOE_STEP_BLOCKS
    n_steps = nb // MOE_STEP_BLOCKS
    step_rows = MOE_STEP_BLOCKS * MOE_ROWS
    n_slots = nb * MOE_ROWS
    w_lo, w_hi = route[2], route[3]
    rank = route[4].astype(jnp.int32)
    cls = route[5].astype(jnp.int32)
    cls_ids = jnp.arange(N_CLASSES, dtype=jnp.int32)
    onehot_tok = (cls[:, None] == cls_ids[None, :]).astype(jnp.int32)
    counts = jnp.sum(onehot_tok, axis=0)
    nblk = (counts + MOE_ROWS - 1) // MOE_ROWS
    npad = nblk * MOE_ROWS - counts
    blk_end = jnp.cumsum(nblk)
    cls_start = (blk_end - nblk) * MOE_ROWS
    slot_of_tok = jnp.sum(onehot_tok * cls_start[None, :], axis=1) + rank
    x_sorted = _dispatch(x_tiled, slot_of_tok, cls_start + counts, npad, blk_end[-1:], n_slots)

    max_pad = MOE_ROWS - 1
    cand_cls = jnp.repeat(cls_ids, max_pad)
    cand_on = jnp.tile(jnp.arange(max_pad, dtype=jnp.int32), N_CLASSES) < jnp.repeat(npad, max_pad)
    n_fill = n_slots - total - N_CLASSES * max_pad
    tok_bits = 16
    assert total <= 1 << tok_bits
    keys = jnp.concatenate([
        (2 * cls << tok_bits) + jnp.arange(total, dtype=jnp.int32),
        jnp.where(cand_on, 2 * cand_cls + 1, 2 * N_CLASSES + 1) << tok_bits,
        jnp.full((n_fill,), (2 * N_CLASSES + 1) << tok_bits, jnp.int32)])
    zeros = jnp.zeros((n_slots - total,), jnp.float32)
    key_s, wlo_s, whi_s = lax.sort(
        (keys, jnp.concatenate([w_lo, zeros]), jnp.concatenate([w_hi, zeros])),
        num_keys=1, is_stable=False)
    is_pad = ((key_s >> tok_bits) & 1) == 1
    tok_s = key_s & ((1 << tok_bits) - 1)
    slot = jnp.arange(n_slots, dtype=jnp.int32)
    dump = total + ((slot // step_rows) % MOE_OBUFS) * step_rows + slot % step_rows
    dst = jnp.where(is_pad, dump, tok_s)
    nvalid = jnp.sum((~is_pad).reshape(nb, MOE_ROWS).astype(jnp.int32), axis=1)
    wts = jnp.concatenate(
        [wlo_s.reshape(nb, 1, MOE_ROWS), whi_s.reshape(nb, 1, MOE_ROWS),
         jnp.zeros((nb, SUBLANES - 2, MOE_ROWS), jnp.float32)], axis=1)
    bidx = jnp.arange(nb, dtype=jnp.int32)
    xblk = jnp.minimum(bidx, blk_end[-1] - 1)
    c_blk = jnp.sum((blk_end[None, :] <= xblk[:, None]).astype(jnp.int32), axis=1)
    xstep = jnp.minimum(jnp.arange(n_steps, dtype=jnp.int32), (blk_end[-1] - 1) // MOE_STEP_BLOCKS)
    ea_tab, eb_tab = _class_tables()
    onehot_blk = (c_blk[:, None] == cls_ids[None, :]).astype(jnp.int32)
    ea = jnp.sum(onehot_blk * jnp.asarray(ea_tab)[None, :], axis=1)
    eb = jnp.sum(onehot_blk * jnp.asarray(eb_tab)[None, :], axis=1)

    tiled_block = (step_rows * TILES_PER_ROW, LANES)
    if final_norm:
        out_block = (step_rows, D_MODEL)
        out_shape = (total, D_MODEL)
    else:
        out_block = tiled_block
        out_shape = ((total + MOE_OBUFS * step_rows) * TILES_PER_ROW, LANES)

    def weight_specs(j):
        def spec(shape, tab):
            return pl.BlockSpec(shape, lambda i, dst, nv, ea, eb, xstep:
                                (layer, (ea, eb)[tab][i * MOE_STEP_BLOCKS + j], 0, 0))
        shapes = {"gate": (1, 1, D_MODEL, MOE_D_FF), "up": (1, 1, D_MODEL, MOE_D_FF),
                  "down": (1, 1, MOE_D_FF, D_MODEL)}
        return [spec(shapes[kind], tab) for tab in range(2) for kind in MOE_WEIGHTS]

    grid_spec = pltpu.PrefetchScalarGridSpec(
        num_scalar_prefetch=5,
        grid=(n_steps,),
        in_specs=[
            pl.BlockSpec(tiled_block, lambda i, dst, nv, ea, eb, xstep: (xstep[i], 0)),
            pl.BlockSpec((MOE_STEP_BLOCKS, SUBLANES, MOE_ROWS), lambda i, *_: (i, 0, 0)),
            pl.BlockSpec((1, D_MODEL), lambda i, *_: (0, 0)),
            pl.BlockSpec((1, D_MODEL), lambda i, *_: (0, 0)),
        ] + [s for j in range(MOE_STEP_BLOCKS) for s in weight_specs(j)],
        out_specs=pl.BlockSpec(memory_space=pl.ANY),
        scratch_shapes=[
            pltpu.VMEM((MOE_OBUFS,) + out_block, jnp.float32),
            pltpu.SemaphoreType.DMA((MOE_OBUFS,)),
        ],
    )
    return pl.pallas_call(
        functools.partial(_moe_kernel, final_norm=final_norm, total=total),
        grid_spec=grid_spec,
        out_shape=jax.ShapeDtypeStruct(out_shape, jnp.float32),
        compiler_params=pltpu.CompilerParams(
            dimension_semantics=("arbitrary",),
            vmem_limit_bytes=VMEM_LIMIT),
        name="moe_layer",
    )(dst, nvalid, ea, eb, xstep, x_sorted, wts, gain, fgain, *(list(weights) * (2 * MOE_STEP_BLOCKS)))


def _router_params(w_group, b_group, w_expert, b_expert):
    gap = ROUTE_EXPERT_COL - MOE_GROUPS
    pad = LANES - ROUTE_EXPERT_COL - MOE_EXPERTS
    wr = jnp.concatenate([w_group, jnp.zeros((D_MODEL, gap), jnp.float32), w_expert,
                          jnp.zeros((D_MODEL, pad), jnp.float32)], axis=1)
    br = jnp.concatenate([b_group, jnp.zeros((gap,), jnp.float32), b_expert.reshape(-1),
                          jnp.zeros((pad,), jnp.float32)])[None, :]
    wr_hi = wr.astype(jnp.bfloat16)
    wr_lo = (wr - wr_hi.astype(jnp.float32)).astype(jnp.bfloat16)
    return jnp.concatenate([wr_hi, wr_lo], axis=1), br


def kernel(x, norm_mix, norm_ffn, ret_w_in, ret_w_out, pool_w, pool_scale, moe_w_group, moe_b_group, moe_w_expert, moe_b_expert, moe_w_gate, moe_w_up, moe_w_down, final_norm):
    batch, seq, _ = x.shape
    x2d = x.reshape(batch * seq, D_MODEL)
    bf = jnp.bfloat16
    weights = (moe_w_gate.astype(bf), moe_w_up.astype(bf), moe_w_down.astype(bf))
    fgain = final_norm[None, :]

    wr0, br0 = _router_params(moe_w_group[0], moe_b_group[0], moe_w_expert[0], moe_b_expert[0])
    x1, route0 = _retention_layer(x2d, batch, seq, norm_mix[0][None, :], ret_w_in[0].astype(bf),
                                  ret_w_out[0].astype(bf), norm_ffn[0][None, :], wr0, br0)
    x2 = _moe_layer(x1, route0, norm_ffn[0][None, :], fgain, weights, 0, final_norm=False)

    wr1, br1 = _router_params(moe_w_group[1], moe_b_group[1], moe_w_expert[1], moe_b_expert[1])
    x3, route1 = _pool_layer(x2, batch, seq, norm_mix[1][None, :], pool_w[0].astype(bf),
                             pool_scale[0][None, :], norm_ffn[1][None, :], wr1, br1)
    out = _moe_layer(x3, route1, norm_ffn[1][None, :], fgain, weights, 1, final_norm=True)
    return out.reshape(batch, seq, D_MODEL)
```

```python
import functools

import numpy as np
import jax
import jax.numpy as jnp
from jax import lax
from jax.experimental import pallas as pl
from jax.experimental.pallas import tpu as pltpu

D_MODEL = 1024
EPS = 1e-6
CHUNK = 64

RET_HEADS = 4
RET_DK = 256
RET_DV = 512
RET_QK_DIM = RET_HEADS * RET_DK
RET_V_DIM = RET_HEADS * RET_DV
ROPE_BASE = 10000.0

POOL_WINDOWS = (2, 4, 8, 16)
POOL_GROUP_DIM = 256
POOL_HIST = 16

MOE_GROUPS = 4
MOE_EPG = 8
MOE_EXPERTS = MOE_GROUPS * MOE_EPG
MOE_D_FF = 256
PAIRS_PER_GROUP = MOE_EPG * (MOE_EPG - 1) // 2
N_CLASSES = MOE_GROUPS * PAIRS_PER_GROUP

LANES = 128
SUBLANES = 8

RET_TILE = 256
RET_BLOCK = 256
POOL_TILE = 1024
MOE_ROWS = 128
MOE_STEP_BLOCKS = 2
MOE_WEIGHTS = ("gate", "up", "down")
DISPATCH_TILE = 2048
MOE_OBUFS = 2
DMA_THREADS = 2
VMEM_LIMIT = 56 * 1024 * 1024


def _rms(x, gain):
    ms = jnp.mean(x * x, axis=-1, keepdims=True)
    return x * lax.rsqrt(ms + EPS) * gain


def _silu(x):
    return x * jax.nn.sigmoid(x)


def _dot(a, b):
    return jnp.dot(a, b, preferred_element_type=jnp.float32)


ROUTE_EXPERT_COL = 8
TILES_PER_ROW = D_MODEL // LANES


def _load_rows(ref, rows):
    return jnp.concatenate(
        [ref[pl.ds(s, rows, stride=TILES_PER_ROW), :] for s in range(TILES_PER_ROW)], axis=1)


def _store_rows(ref, val):
    rows = val.shape[0]
    for s in range(TILES_PER_ROW):
        ref[pl.ds(s, rows, stride=TILES_PER_ROW), :] = val[:, s * LANES:(s + 1) * LANES]


def _route(x1, gain2, wr2, br, tri, cnt_ref):
    rows = x1.shape[0]
    h2 = _rms(x1, gain2)
    h_hi = h2.astype(jnp.bfloat16)
    h_lo = (h2 - h_hi.astype(jnp.float32)).astype(jnp.bfloat16)
    parts = _dot(jnp.concatenate([h_hi, h_lo], axis=0), wr2)
    logits = (parts[:rows, :LANES] + parts[rows:, :LANES]) + parts[:rows, LANES:] + br
    lt = logits.T
    sub = lax.broadcasted_iota(jnp.int32, (SUBLANES, rows), 0)
    neg = jnp.float32(-jnp.inf)
    gl = jnp.where(sub < MOE_GROUPS, lt[0:SUBLANES], neg)
    gmax = jnp.max(gl, axis=0, keepdims=True)
    grp = jnp.min(jnp.where(gl == gmax, sub, SUBLANES), axis=0, keepdims=True)
    g_gate = 1.0 / jnp.sum(jnp.exp(gl - gmax), axis=0, keepdims=True)
    el = lt[ROUTE_EXPERT_COL:ROUTE_EXPERT_COL + MOE_EPG]
    for g in range(1, MOE_GROUPS):
        first = ROUTE_EXPERT_COL + g * MOE_EPG
        el = jnp.where(grp == g, lt[first:first + MOE_EPG], el)
    m1 = jnp.max(el, axis=0, keepdims=True)
    i1 = jnp.min(jnp.where(el == m1, sub, SUBLANES), axis=0, keepdims=True)
    el2 = jnp.where(sub == i1, neg, el)
    m2 = jnp.max(el2, axis=0, keepdims=True)
    i2 = jnp.min(jnp.where(el2 == m2, sub, SUBLANES), axis=0, keepdims=True)
    t = jnp.exp(m2 - m1)
    w1 = g_gate / (1.0 + t)
    w2 = g_gate * t / (1.0 + t)
    first_lo = i1 < i2
    a = jnp.where(first_lo, i1, i2)
    bb = jnp.where(first_lo, i2, i1)
    base = grp * MOE_EPG
    lo = (base + a).astype(jnp.float32)
    hi = (base + bb).astype(jnp.float32)
    w_lo = jnp.where(first_lo, w1, w2)
    w_hi = jnp.where(first_lo, w2, w1)
    cls = grp * PAIRS_PER_GROUP + ((a * (2 * MOE_EPG - 1 - a)) >> 1) + (bb - a - 1)
    onehot = (lax.broadcasted_iota(jnp.int32, (LANES, rows), 0) == cls).astype(jnp.float32)
    earlier = _dot(onehot.astype(jnp.bfloat16), tri) + cnt_ref[...]
    rank = jnp.sum(onehot * earlier, axis=0, keepdims=True)
    cnt_ref[...] += jnp.sum(onehot, axis=1, keepdims=True)
    out = jnp.where(sub == 0, lo, jnp.where(sub == 1, hi, jnp.where(sub == 2, w_lo,
                    jnp.where(sub == 3, w_hi, 0.0))))
    return jnp.where(sub == 4, rank, jnp.where(sub == 5, cls.astype(jnp.float32), out))


def _ret_kernel(x_ref, gain_ref, win_ref, wout_ref, cos_ref, sin_ref, dmat_ref,
                qdec_ref, kdec_ref, cdec_ref, gain2_ref, wr_ref, br_ref, tri_ref,
                x1_ref, route_ref, state_ref, gated_ref, cnt_ref):
    @pl.when(pl.program_id(1) == 0)
    def _():
        state_ref[...] = jnp.zeros_like(state_ref)

    @pl.when(jnp.logical_and(pl.program_id(0) == 0, pl.program_id(1) == 0))
    def _():
        cnt_ref[...] = jnp.zeros_like(cnt_ref)

    x = x_ref[...]
    h = _rms(x, gain_ref[...]).astype(jnp.bfloat16)
    cos = cos_ref[...]
    sin = sin_ref[...]
    half = RET_DK // 2

    def rope(t):
        t1, t2 = t[:, :half], t[:, half:]
        return jnp.concatenate([t1 * cos - t2 * sin, t1 * sin + t2 * cos], axis=1)

    for hd in range(RET_HEADS):
        q = _dot(h, win_ref[:, hd * RET_DK:(hd + 1) * RET_DK])
        k = _dot(h, win_ref[:, RET_QK_DIM + hd * RET_DK:RET_QK_DIM + (hd + 1) * RET_DK])
        v0 = 2 * RET_QK_DIM + hd * RET_DV
        v = _dot(h, win_ref[:, v0:v0 + RET_DV])
        g0 = 2 * RET_QK_DIM + RET_V_DIM + hd * RET_DV
        g = _dot(h, win_ref[:, g0:g0 + RET_DV])
        q = rope(q)
        k = rope(k) * (RET_DK ** -0.5)
        for blk in range(x.shape[0] // RET_BLOCK):
            rs = slice(blk * RET_BLOCK, (blk + 1) * RET_BLOCK)
            qb = q[rs].astype(jnp.bfloat16)
            kb = k[rs].astype(jnp.bfloat16)
            vb = v[rs].astype(jnp.bfloat16)
            sc = lax.dot_general(qb, kb, (((1,), (1,)), ((), ())),
                                 preferred_element_type=jnp.float32)
            sc = sc * dmat_ref[hd]
            intra = _dot(sc.astype(jnp.bfloat16), vb)
            st = state_ref[hd]
            cross = _dot(qb, st.astype(jnp.bfloat16)) * qdec_ref[hd]
            o = intra + cross
            kd = (k[rs] * kdec_ref[hd]).astype(jnp.bfloat16)
            upd = lax.dot_general(kd, vb, (((0,), (0,)), ((), ())),
                                  preferred_element_type=jnp.float32)
            state_ref[hd] = st * cdec_ref[hd] + upd
            mu = jnp.mean(o, axis=-1, keepdims=True)
            oc = o - mu
            var = jnp.mean(oc * oc, axis=-1, keepdims=True)
            on = oc * lax.rsqrt(var + EPS)
            gated_ref[rs, hd * RET_DV:(hd + 1) * RET_DV] = (_silu(g[rs]) * on).astype(jnp.bfloat16)

    x1 = x + _dot(gated_ref[...], wout_ref[...])
    _store_rows(x1_ref, x1)
    route_ref[...] = _route(x1, gain2_ref[...], wr_ref[...], br_ref[...], tri_ref[...], cnt_ref)


def _tri(n):
    i = jnp.arange(n)
    return (i[:, None] < i[None, :]).astype(jnp.bfloat16)


def _const_spec(shape):
    nd = len(shape)
    return pl.BlockSpec(shape, lambda *_: (0,) * nd, pipeline_mode=pl.Buffered(1))


def _retention_layer(x2d, batch, seq, gain, w_in, w_out, gain2, wr, br):
    tb = RET_TILE
    n_s = seq // tb
    total = batch * seq
    rb = RET_BLOCK
    log_g = jnp.log(1.0 - jnp.exp2(-5.0 - jnp.arange(RET_HEADS, dtype=jnp.float32)))
    n = jnp.arange(rb, dtype=jnp.float32)
    diff = n[:, None] - n[None, :]
    cn = jnp.arange(rb)[:, None] // CHUNK
    cm = jnp.arange(rb)[None, :] // CHUNK
    expo = jnp.where(cn == cm, jnp.abs(diff), diff)
    dmat = jnp.where((cm <= cn)[None], jnp.exp(log_g[:, None, None] * expo[None]), 0.0)
    qdec = jnp.exp(log_g[:, None] * (n + 1.0)[None])[:, :, None]
    kdec = jnp.exp(log_g[:, None] * (rb - 1.0 - n)[None])[:, :, None]
    cdec = jnp.broadcast_to(jnp.exp(log_g * rb)[:, None, None], (RET_HEADS, 1, RET_DV))
    half = RET_DK // 2
    inv_freq = 1.0 / (ROPE_BASE ** (jnp.arange(half, dtype=jnp.float32) / half))
    ang = jnp.arange(seq, dtype=jnp.float32)[:, None] * inv_freq[None, :]
    cos, sin = jnp.cos(ang), jnp.sin(ang)

    tile = lambda b, s: (b * n_s + s, 0)
    return pl.pallas_call(
        _ret_kernel,
        grid=(batch, n_s),
        in_specs=[
            pl.BlockSpec((tb, D_MODEL), tile),
            _const_spec((1, D_MODEL)),
            _const_spec(w_in.shape),
            _const_spec(w_out.shape),
            pl.BlockSpec((tb, half), lambda b, s: (s, 0)),
            pl.BlockSpec((tb, half), lambda b, s: (s, 0)),
            _const_spec(dmat.shape),
            _const_spec(qdec.shape),
            _const_spec(kdec.shape),
            _const_spec(cdec.shape),
            _const_spec((1, D_MODEL)),
            _const_spec(wr.shape),
            _const_spec(br.shape),
            _const_spec((tb, tb)),
        ],
        out_specs=[
            pl.BlockSpec((tb * TILES_PER_ROW, LANES), tile),
            pl.BlockSpec((SUBLANES, tb), lambda b, s: (0, b * n_s + s)),
        ],
        out_shape=[
            jax.ShapeDtypeStruct((total * TILES_PER_ROW, LANES), jnp.float32),
            jax.ShapeDtypeStruct((SUBLANES, total), jnp.float32),
        ],
        scratch_shapes=[
            pltpu.VMEM((RET_HEADS, RET_DK, RET_DV), jnp.float32),
            pltpu.VMEM((tb, RET_V_DIM), jnp.bfloat16),
            pltpu.VMEM((LANES, 1), jnp.float32),
        ],
        compiler_params=pltpu.CompilerParams(
            dimension_semantics=("arbitrary", "arbitrary"),
            vmem_limit_bytes=VMEM_LIMIT),
        name="retention_layer",
    )(x2d, gain, w_in, w_out, cos, sin, dmat, qdec, kdec, cdec, gain2, wr, br, _tri(tb))


def _pool_kernel(x_ref, gain_ref, pw_ref, scale_ref, gain2_ref, wr_ref, br_ref, tri_ref,
                 x1_ref, route_ref, ext_ref, y_ref, cnt_ref):
    s = pl.program_id(1)
    tb = x_ref.shape[0] // TILES_PER_ROW

    @pl.when(s == 0)
    def _():
        ext_ref[0:POOL_HIST, :] = jnp.zeros((POOL_HIST, D_MODEL), jnp.float32)

    @pl.when(jnp.logical_and(pl.program_id(0) == 0, s == 0))
    def _():
        cnt_ref[...] = jnp.zeros_like(cnt_ref)

    x = _load_rows(x_ref, tb)
    ext_ref[POOL_HIST:, :] = _rms(x, gain_ref[...])
    pos = s * tb + lax.broadcasted_iota(jnp.int32, (tb, 1), 0)
    for g, w in enumerate(POOL_WINDOWS):
        cs = slice(g * POOL_GROUP_DIM, (g + 1) * POOL_GROUP_DIM)
        e = ext_ref[:, cs]
        acc = e
        k = 1
        while k < w:
            acc = acc + pltpu.roll(acc, k, axis=0)
            k *= 2
        inv_cnt = 1.0 / jnp.minimum(pos + 1, w).astype(jnp.float32)
        pooled = acc[POOL_HIST:, :] * inv_cnt - e[POOL_HIST:, :]
        y_ref[:, cs] = _dot(pooled.astype(jnp.bfloat16), pw_ref[g])
    ext_ref[0:POOL_HIST, :] = ext_ref[tb:tb + POOL_HIST, :]
    x1 = x + y_ref[...] * scale_ref[...]
    _store_rows(x1_ref, x1)
    route_ref[...] = _route(x1, gain2_ref[...], wr_ref[...], br_ref[...], tri_ref[...], cnt_ref)


def _pool_layer(x2d, batch, seq, gain, pw, scale, gain2, wr, br):
    tb = POOL_TILE
    n_s = seq // tb
    total = batch * seq
    tile = lambda b, s: (b * n_s + s, 0)
    return pl.pallas_call(
        _pool_kernel,
        grid=(batch, n_s),
        in_specs=[
            pl.BlockSpec((tb * TILES_PER_ROW, LANES), tile),
            _const_spec((1, D_MODEL)),
            _const_spec(pw.shape),
            _const_spec((1, D_MODEL)),
            _const_spec((1, D_MODEL)),
            _const_spec(wr.shape),
            _const_spec(br.shape),
            _const_spec((tb, tb)),
        ],
        out_specs=[
            pl.BlockSpec((tb * TILES_PER_ROW, LANES), tile),
            pl.BlockSpec((SUBLANES, tb), lambda b, s: (0, b * n_s + s)),
        ],
        out_shape=[
            jax.ShapeDtypeStruct((total * TILES_PER_ROW, LANES), jnp.float32),
            jax.ShapeDtypeStruct((SUBLANES, total), jnp.float32),
        ],
        scratch_shapes=[
            pltpu.VMEM((POOL_HIST + tb, D_MODEL), jnp.float32),
            pltpu.VMEM((tb, D_MODEL), jnp.float32),
            pltpu.VMEM((LANES, 1), jnp.float32),
        ],
        compiler_params=pltpu.CompilerParams(
            dimension_semantics=("arbitrary", "arbitrary"),
            vmem_limit_bytes=VMEM_LIMIT),
        name="pool_layer",
    )(x2d, gain, pw, scale, gain2, wr, br, _tri(tb))


def _tile_of(row):
    return pl.ds(pl.multiple_of(row * TILES_PER_ROW, TILES_PER_ROW), TILES_PER_ROW)


def _dispatch_kernel(slot_ref, run_start_ref, run_len_ref, n_used_ref,
                     x_ref, out_hbm, zero_ref, sem, sem_pad):
    i = pl.program_id(0)
    tile = x_ref.shape[0] // TILES_PER_ROW

    @pl.when(i == 0)
    def _():
        zero_ref[...] = jnp.zeros_like(zero_ref)

        def pad_copy(c):
            rows = pl.multiple_of(run_len_ref[c] * TILES_PER_ROW, TILES_PER_ROW)
            first = pl.multiple_of(run_start_ref[c] * TILES_PER_ROW, TILES_PER_ROW)
            return pltpu.make_async_copy(zero_ref.at[pl.ds(0, rows)], out_hbm.at[pl.ds(first, rows)],
                                         sem_pad.at[0])

        def block_copy(blk):
            first = pl.multiple_of(blk * zero_ref.shape[0], zero_ref.shape[0])
            return pltpu.make_async_copy(zero_ref, out_hbm.at[pl.ds(first, zero_ref.shape[0])],
                                         sem_pad.at[0])

        n_blocks = out_hbm.shape[0] // zero_ref.shape[0]
        for c in range(N_CLASSES):
            @pl.when(run_len_ref[c] > 0)
            def _():
                pad_copy(c).start()
        lax.fori_loop(n_used_ref[0], n_blocks, lambda blk, c: (block_copy(blk).start(), c)[1], 0)
        for c in range(N_CLASSES):
            @pl.when(run_len_ref[c] > 0)
            def _():
                pad_copy(c).wait()
        lax.fori_loop(n_used_ref[0], n_blocks, lambda blk, c: (block_copy(blk).wait(), c)[1], 0)

    for r in range(tile):
        slot = slot_ref[i * tile + r]
        pltpu.make_async_copy(x_ref.at[_tile_of(r)], out_hbm.at[_tile_of(slot)],
                              sem.at[0]).start(priority=r % DMA_THREADS)
    pltpu.make_async_copy(x_ref, out_hbm.at[pl.ds(0, x_ref.shape[0])], sem.at[0]).wait()


def _dispatch(x_tiled, slot_of_tok, run_start, run_len, n_used, n_slots):
    total = slot_of_tok.shape[0]
    tile = DISPATCH_TILE
    grid_spec = pltpu.PrefetchScalarGridSpec(
        num_scalar_prefetch=4,
        grid=(total // tile,),
        in_specs=[pl.BlockSpec((tile * TILES_PER_ROW, LANES), lambda i, *_: (i, 0))],
        out_specs=pl.BlockSpec(memory_space=pl.ANY),
        scratch_shapes=[
            pltpu.VMEM((MOE_ROWS * TILES_PER_ROW, LANES), jnp.float32),
            pltpu.SemaphoreType.DMA((1,)),
            pltpu.SemaphoreType.DMA((1,)),
        ],
    )
    return pl.pallas_call(
        _dispatch_kernel,
        grid_spec=grid_spec,
        out_shape=jax.ShapeDtypeStruct((n_slots * TILES_PER_ROW, LANES), jnp.float32),
        compiler_params=pltpu.CompilerParams(
            dimension_semantics=("arbitrary",),
            vmem_limit_bytes=VMEM_LIMIT),
        name="moe_dispatch",
    )(slot_of_tok, run_start, run_len, n_used, x_tiled)


def _moe_kernel(dst_ref, nv_ref, ea_ref, eb_ref, xstep_ref,
                x_ref, wts_ref, gain_ref, fgain_ref, *rest, final_norm, total):
    n_w = 2 * len(MOE_WEIGHTS) * MOE_STEP_BLOCKS
    w_refs = rest[:n_w]
    out_hbm, obuf, sem_out = rest[n_w:]
    i = pl.program_id(0)
    blk0 = i * MOE_STEP_BLOCKS
    n_first = nv_ref[blk0]
    n_first_prev = nv_ref[jnp.maximum(blk0 - MOE_STEP_BLOCKS, 0)]
    osl = i % MOE_OBUFS
    step_rows = MOE_STEP_BLOCKS * MOE_ROWS

    def counts(step):
        first = jnp.maximum(step, 0) * MOE_STEP_BLOCKS
        return [nv_ref[first + j] for j in range(MOE_STEP_BLOCKS)]

    def out_copy(sl, r, tok):
        if final_norm:
            return pltpu.make_async_copy(obuf.at[sl, pl.ds(r, 1)], out_hbm.at[pl.ds(tok, 1)],
                                         sem_out.at[sl])
        return pltpu.make_async_copy(obuf.at[sl, _tile_of(r)], out_hbm.at[_tile_of(tok)],
                                     sem_out.at[sl])

    def start_scatter(step, sl, rows):
        def unrolled(j):
            for r in range(j * MOE_ROWS, (j + 1) * MOE_ROWS):
                out_copy(sl, r, dst_ref[step * step_rows + r]).start(priority=r % DMA_THREADS)

        for j in range(MOE_STEP_BLOCKS):
            if not final_norm:
                unrolled(j)
                continue
            pl.when(rows[j] == MOE_ROWS)(functools.partial(unrolled, j))

            @pl.when(rows[j] < MOE_ROWS)
            def _():
                def body(r, c):
                    out_copy(sl, j * MOE_ROWS + r, dst_ref[step * step_rows + j * MOE_ROWS + r]).start()
                    return c
                lax.fori_loop(0, rows[j], body, 0)

    def wait_scatter(sl, rows):
        if not final_norm:
            pltpu.make_async_copy(obuf.at[sl], out_hbm.at[pl.ds(0, obuf.shape[1])],
                                  sem_out.at[sl]).wait()
            return
        for j in range(MOE_STEP_BLOCKS):
            tiled = pl.multiple_of((rows[j] // SUBLANES) * SUBLANES, SUBLANES)

            @pl.when(tiled > 0)
            def _():
                pltpu.make_async_copy(obuf.at[sl, pl.ds(0, tiled)], out_hbm.at[pl.ds(0, tiled)],
                                      sem_out.at[sl]).wait()

            def body(r, c):
                out_copy(sl, 0, 0).wait()
                return c
            lax.fori_loop(tiled, rows[j], body, 0)

    if not final_norm:
        @pl.when(i == 0)
        def _():
            obuf[...] = jnp.zeros(obuf.shape, jnp.float32)
            for r in range(step_rows):
                out_copy(0, r, total + r).start()

    @pl.when(n_first > 0)
    def _():
        tile_rows = MOE_ROWS * TILES_PER_ROW
        blocks = range(MOE_STEP_BLOCKS)
        if not final_norm:
            start_scatter(i, 1 - osl, None)
        xs = [_load_rows(x_ref.at[pl.ds(j * tile_rows, tile_rows)], MOE_ROWS) for j in blocks]
        hs = [_rms(x, gain_ref[...]).astype(jnp.bfloat16) for x in xs]
        wts = [wts_ref[j].T for j in blocks]
        def weight(j, e, kind):
            return w_refs[(2 * j + e) * len(MOE_WEIGHTS) + MOE_WEIGHTS.index(kind)][0, 0]

        experts = [(j, e) for j in blocks for e in range(2)]
        gates = {je: _dot(hs[je[0]], weight(*je, "gate")) for je in experts}
        ups = {je: _dot(hs[je[0]], weight(*je, "up")) for je in experts}
        hid = {je: (_silu(gates[je]) * ups[je]).astype(jnp.bfloat16) for je in experts}
        outs = [xs[j] + wts[j][:, 0:1] * _dot(hid[j, 0], weight(j, 0, "down"))
                + wts[j][:, 1:2] * _dot(hid[j, 1], weight(j, 1, "down")) for j in blocks]
        if final_norm:
            pl.when(i >= MOE_OBUFS)(lambda: wait_scatter(osl, counts(i - MOE_OBUFS)))
            for j in blocks:
                obuf[osl, j * MOE_ROWS:(j + 1) * MOE_ROWS] = _rms(outs[j], fgain_ref[...])
        else:
            wait_scatter(osl, None)
            for j in blocks:
                _store_rows(obuf.at[osl, pl.ds(j * tile_rows, tile_rows)], outs[j])
        if final_norm:
            start_scatter(i, osl, counts(i))

    @pl.when(jnp.logical_and(n_first == 0, n_first_prev > 0))
    def _():
        if final_norm:
            wait_scatter(1 - osl, counts(i - 1))
            pl.when(i >= MOE_OBUFS)(lambda: wait_scatter(osl, counts(i - MOE_OBUFS)))
        else:
            start_scatter(i, 1 - osl, None)
            for sl in range(MOE_OBUFS):
                wait_scatter(sl, None)


def _class_tables():
    ea = np.zeros((N_CLASSES,), np.int32)
    eb = np.zeros((N_CLASSES,), np.int32)
    for g in range(MOE_GROUPS):
        c = g * PAIRS_PER_GROUP
        for a in range(MOE_EPG):
            for b in range(a + 1, MOE_EPG):
                ea[c], eb[c] = g * MOE_EPG + a, g * MOE_EPG + b
                c += 1
    return ea, eb


def _moe_layer(x_tiled, route, gain, fgain, weights, layer, final_norm):
    total = route.shape[1]
    nb = (total // MOE_ROWS + N_CLASSES) // MOE_STEP_BLOCKS * MOE_STEP_BLOCKS + MOE_STEP_BLOCKS
    n_steps = nb // MOE_STEP_BLOCKS
    step_rows = MOE_STEP_BLOCKS * MOE_ROWS
    n_slots = nb * MOE_ROWS
    w_lo, w_hi = route[2], route[3]
    rank = route[4].astype(jnp.int32)
    cls = route[5].astype(jnp.int32)
    cls_ids = jnp.arange(N_CLASSES, dtype=jnp.int32)
    onehot_tok = (cls[:, None] == cls_ids[None, :]).astype(jnp.int32)
    counts = jnp.sum(onehot_tok, axis=0)
    nblk = (counts + MOE_ROWS - 1) // MOE_ROWS
    npad = nblk * MOE_ROWS - counts
    blk_end = jnp.cumsum(nblk)
    cls_start = (blk_end - nblk) * MOE_ROWS
    slot_of_tok = jnp.sum(onehot_tok * cls_start[None, :], axis=1) + rank
    x_sorted = _dispatch(x_tiled, slot_of_tok, cls_start + counts, npad, blk_end[-1:], n_slots)

    max_pad = MOE_ROWS - 1
    cand_cls = jnp.repeat(cls_ids, max_pad)
    cand_on = jnp.tile(jnp.arange(max_pad, dtype=jnp.int32), N_CLASSES) < jnp.repeat(npad, max_pad)
    n_fill = n_slots - total - N_CLASSES * max_pad
    tok_bits = 16
    assert total <= 1 << tok_bits
    keys = jnp.concatenate([
        (2 * cls << tok_bits) + jnp.arange(total, dtype=jnp.int32),
        jnp.where(cand_on, 2 * cand_cls + 1, 2 * N_CLASSES + 1) << tok_bits,
        jnp.full((n_fill,), (2 * N_CLASSES + 1) << tok_bits, jnp.int32)])
    zeros = jnp.zeros((n_slots - total,), jnp.float32)
    key_s, wlo_s, whi_s = lax.sort(
        (keys, jnp.concatenate([w_lo, zeros]), jnp.concatenate([w_hi, zeros])),
        num_keys=1, is_stable=False)
    is_pad = ((key_s >> tok_bits) & 1) == 1
    tok_s = key_s & ((1 << tok_bits) - 1)
    slot = jnp.arange(n_slots, dtype=jnp.int32)
    dump = total + ((slot // step_rows) % MOE_OBUFS) * step_rows + slot % step_rows
    dst = jnp.where(is_pad, dump, tok_s)
    if not final_norm:
        dst = jnp.concatenate([total + step_rows + jnp.arange(step_rows, dtype=jnp.int32),
                               dst[:-step_rows]])
    nvalid = jnp.sum((~is_pad).reshape(nb, MOE_ROWS).astype(jnp.int32), axis=1)
    wts = jnp.concatenate(
        [wlo_s.reshape(nb, 1, MOE_ROWS), whi_s.reshape(nb, 1, MOE_ROWS),
         jnp.zeros((nb, SUBLANES - 2, MOE_ROWS), jnp.float32)], axis=1)
    bidx = jnp.arange(nb, dtype=jnp.int32)
    xblk = jnp.minimum(bidx, blk_end[-1] - 1)
    c_blk = jnp.sum((blk_end[None, :] <= xblk[:, None]).astype(jnp.int32), axis=1)
    xstep = jnp.minimum(jnp.arange(n_steps, dtype=jnp.int32), (blk_end[-1] - 1) // MOE_STEP_BLOCKS)
    ea_tab, eb_tab = _class_tables()
    onehot_blk = (c_blk[:, None] == cls_ids[None, :]).astype(jnp.int32)
    ea = jnp.sum(onehot_blk * jnp.asarray(ea_tab)[None, :], axis=1)
    eb = jnp.sum(onehot_blk * jnp.asarray(eb_tab)[None, :], axis=1)

    tiled_block = (step_rows * TILES_PER_ROW, LANES)
    if final_norm:
        out_block = (step_rows, D_MODEL)
        out_shape = (total, D_MODEL)
    else:
        out_block = tiled_block
        out_shape = ((total + MOE_OBUFS * step_rows) * TILES_PER_ROW, LANES)

    def weight_specs(j):
        def spec(shape, tab):
            return pl.BlockSpec(shape, lambda i, dst, nv, ea, eb, xstep:
                                (layer, (ea, eb)[tab][i * MOE_STEP_BLOCKS + j], 0, 0))
        shapes = {"gate": (1, 1, D_MODEL, MOE_D_FF), "up": (1, 1, D_MODEL, MOE_D_FF),
                  "down": (1, 1, MOE_D_FF, D_MODEL)}
        return [spec(shapes[kind], tab) for tab in range(2) for kind in MOE_WEIGHTS]

    grid_spec = pltpu.PrefetchScalarGridSpec(
        num_scalar_prefetch=5,
        grid=(n_steps,),
        in_specs=[
            pl.BlockSpec(tiled_block, lambda i, dst, nv, ea, eb, xstep: (xstep[i], 0)),
            pl.BlockSpec((MOE_STEP_BLOCKS, SUBLANES, MOE_ROWS), lambda i, *_: (i, 0, 0)),
            pl.BlockSpec((1, D_MODEL), lambda i, *_: (0, 0)),
            pl.BlockSpec((1, D_MODEL), lambda i, *_: (0, 0)),
        ] + [s for j in range(MOE_STEP_BLOCKS) for s in weight_specs(j)],
        out_specs=pl.BlockSpec(memory_space=pl.ANY),
        scratch_shapes=[
            pltpu.VMEM((MOE_OBUFS,) + out_block, jnp.float32),
            pltpu.SemaphoreType.DMA((MOE_OBUFS,)),
        ],
    )
    return pl.pallas_call(
        functools.partial(_moe_kernel, final_norm=final_norm, total=total),
        grid_spec=grid_spec,
        out_shape=jax.ShapeDtypeStruct(out_shape, jnp.float32),
        compiler_params=pltpu.CompilerParams(
            dimension_semantics=("arbitrary",),
            vmem_limit_bytes=VMEM_LIMIT),
        name="moe_layer",
    )(dst, nvalid, ea, eb, xstep, x_sorted, wts, gain, fgain, *(list(weights) * (2 * MOE_STEP_BLOCKS)))


def _router_params(w_group, b_group, w_expert, b_expert):
    gap = ROUTE_EXPERT_COL - MOE_GROUPS
    pad = LANES - ROUTE_EXPERT_COL - MOE_EXPERTS
    wr = jnp.concatenate([w_group, jnp.zeros((D_MODEL, gap), jnp.float32), w_expert,
                          jnp.zeros((D_MODEL, pad), jnp.float32)], axis=1)
    br = jnp.concatenate([b_group, jnp.zeros((gap,), jnp.float32), b_expert.reshape(-1),
                          jnp.zeros((pad,), jnp.float32)])[None, :]
    wr_hi = wr.astype(jnp.bfloat16)
    wr_lo = (wr - wr_hi.astype(jnp.float32)).astype(jnp.bfloat16)
    return jnp.concatenate([wr_hi, wr_lo], axis=1), br


def kernel(x, norm_mix, norm_ffn, ret_w_in, ret_w_out, pool_w, pool_scale, moe_w_group, moe_b_group, moe_w_expert, moe_b_expert, moe_w_gate, moe_w_up, moe_w_down, final_norm):
    batch, seq, _ = x.shape
    x2d = x.reshape(batch * seq, D_MODEL)
    bf = jnp.bfloat16
    weights = (moe_w_gate.astype(bf), moe_w_up.astype(bf), moe_w_down.astype(bf))
    fgain = final_norm[None, :]

    wr0, br0 = _router_params(moe_w_group[0], moe_b_group[0], moe_w_expert[0], moe_b_expert[0])
    x1, route0 = _retention_layer(x2d, batch, seq, norm_mix[0][None, :], ret_w_in[0].astype(bf),
                                  ret_w_out[0].astype(bf), norm_ffn[0][None, :], wr0, br0)
    x2 = _moe_layer(x1, route0, norm_ffn[0][None, :], fgain, weights, 0, final_norm=False)

    wr1, br1 = _router_params(moe_w_group[1], moe_b_group[1], moe_w_expert[1], moe_b_expert[1])
    x3, route1 = _pool_layer(x2, batch, seq, norm_mix[1][None, :], pool_w[0].astype(bf),
                             pool_scale[0][None, :], norm_ffn[1][None, :], wr1, br1)
    out = _moe_layer(x3, route1, norm_ffn[1][None, :], fgain, weights, 1, final_norm=True)
    return out.reshape(batch, seq, D_MODEL)
```

```python
import functools

import numpy as np
import jax
import jax.numpy as jnp
from jax import lax
from jax.experimental import pallas as pl
from jax.experimental.pallas import tpu as pltpu

D_MODEL = 1024
EPS = 1e-6
CHUNK = 64

RET_HEADS = 4
RET_DK = 256
RET_DV = 512
RET_QK_DIM = RET_HEADS * RET_DK
RET_V_DIM = RET_HEADS * RET_DV
ROPE_BASE = 10000.0

POOL_WINDOWS = (2, 4, 8, 16)
POOL_GROUP_DIM = 256
POOL_HIST = 16

MOE_GROUPS = 4
MOE_EPG = 8
MOE_EXPERTS = MOE_GROUPS * MOE_EPG
MOE_D_FF = 256
PAIRS_PER_GROUP = MOE_EPG * (MOE_EPG - 1) // 2
N_CLASSES = MOE_GROUPS * PAIRS_PER_GROUP

LANES = 128
SUBLANES = 8

RET_TILE = 256
RET_BLOCK = 256
POOL_TILE = 1024
MOE_ROWS = 128
MOE_STEP_BLOCKS = 4
MOE_WEIGHTS = ("gate", "up", "down")
DISPATCH_TILE = 2048
MOE_OBUFS = 2
DMA_THREADS = 2
VMEM_LIMIT = 56 * 1024 * 1024


def _rms(x, gain):
    ms = jnp.mean(x * x, axis=-1, keepdims=True)
    return x * lax.rsqrt(ms + EPS) * gain


def _silu(x):
    return x * jax.nn.sigmoid(x)


def _dot(a, b):
    return jnp.dot(a, b, preferred_element_type=jnp.float32)


ROUTE_EXPERT_COL = 8
TILES_PER_ROW = D_MODEL // LANES


def _load_rows(ref, rows):
    return jnp.concatenate(
        [ref[pl.ds(s, rows, stride=TILES_PER_ROW), :] for s in range(TILES_PER_ROW)], axis=1)


def _store_rows(ref, val):
    rows = val.shape[0]
    for s in range(TILES_PER_ROW):
        ref[pl.ds(s, rows, stride=TILES_PER_ROW), :] = val[:, s * LANES:(s + 1) * LANES]


def _route(x1, gain2, wr2, br, tri, cnt_ref):
    rows = x1.shape[0]
    h2 = _rms(x1, gain2)
    h_hi = h2.astype(jnp.bfloat16)
    h_lo = (h2 - h_hi.astype(jnp.float32)).astype(jnp.bfloat16)
    parts = _dot(jnp.concatenate([h_hi, h_lo], axis=0), wr2)
    logits = (parts[:rows, :LANES] + parts[rows:, :LANES]) + parts[:rows, LANES:] + br
    lt = logits.T
    sub = lax.broadcasted_iota(jnp.int32, (SUBLANES, rows), 0)
    neg = jnp.float32(-jnp.inf)
    gl = jnp.where(sub < MOE_GROUPS, lt[0:SUBLANES], neg)
    gmax = jnp.max(gl, axis=0, keepdims=True)
    grp = jnp.min(jnp.where(gl == gmax, sub, SUBLANES), axis=0, keepdims=True)
    g_gate = 1.0 / jnp.sum(jnp.exp(gl - gmax), axis=0, keepdims=True)
    el = lt[ROUTE_EXPERT_COL:ROUTE_EXPERT_COL + MOE_EPG]
    for g in range(1, MOE_GROUPS):
        first = ROUTE_EXPERT_COL + g * MOE_EPG
        el = jnp.where(grp == g, lt[first:first + MOE_EPG], el)
    m1 = jnp.max(el, axis=0, keepdims=True)
    i1 = jnp.min(jnp.where(el == m1, sub, SUBLANES), axis=0, keepdims=True)
    el2 = jnp.where(sub == i1, neg, el)
    m2 = jnp.max(el2, axis=0, keepdims=True)
    i2 = jnp.min(jnp.where(el2 == m2, sub, SUBLANES), axis=0, keepdims=True)
    t = jnp.exp(m2 - m1)
    w1 = g_gate / (1.0 + t)
    w2 = g_gate * t / (1.0 + t)
    first_lo = i1 < i2
    a = jnp.where(first_lo, i1, i2)
    bb = jnp.where(first_lo, i2, i1)
    base = grp * MOE_EPG
    lo = (base + a).astype(jnp.float32)
    hi = (base + bb).astype(jnp.float32)
    w_lo = jnp.where(first_lo, w1, w2)
    w_hi = jnp.where(first_lo, w2, w1)
    cls = grp * PAIRS_PER_GROUP + ((a * (2 * MOE_EPG - 1 - a)) >> 1) + (bb - a - 1)
    onehot = (lax.broadcasted_iota(jnp.int32, (LANES, rows), 0) == cls).astype(jnp.float32)
    earlier = _dot(onehot.astype(jnp.bfloat16), tri) + cnt_ref[...]
    rank = jnp.sum(onehot * earlier, axis=0, keepdims=True)
    cnt_ref[...] += jnp.sum(onehot, axis=1, keepdims=True)
    out = jnp.where(sub == 0, lo, jnp.where(sub == 1, hi, jnp.where(sub == 2, w_lo,
                    jnp.where(sub == 3, w_hi, 0.0))))
    return jnp.where(sub == 4, rank, jnp.where(sub == 5, cls.astype(jnp.float32), out))


def _ret_kernel(x_ref, gain_ref, win_ref, wout_ref, cos_ref, sin_ref, dmat_ref,
                qdec_ref, kdec_ref, cdec_ref, gain2_ref, wr_ref, br_ref, tri_ref,
                x1_ref, route_ref, state_ref, gated_ref, cnt_ref):
    @pl.when(pl.program_id(1) == 0)
    def _():
        state_ref[...] = jnp.zeros_like(state_ref)

    @pl.when(jnp.logical_and(pl.program_id(0) == 0, pl.program_id(1) == 0))
    def _():
        cnt_ref[...] = jnp.zeros_like(cnt_ref)

    x = x_ref[...]
    h = _rms(x, gain_ref[...]).astype(jnp.bfloat16)
    cos = cos_ref[...]
    sin = sin_ref[...]
    half = RET_DK // 2

    def rope(t):
        t1, t2 = t[:, :half], t[:, half:]
        return jnp.concatenate([t1 * cos - t2 * sin, t1 * sin + t2 * cos], axis=1)

    for hd in range(RET_HEADS):
        q = _dot(h, win_ref[:, hd * RET_DK:(hd + 1) * RET_DK])
        k = _dot(h, win_ref[:, RET_QK_DIM + hd * RET_DK:RET_QK_DIM + (hd + 1) * RET_DK])
        v0 = 2 * RET_QK_DIM + hd * RET_DV
        v = _dot(h, win_ref[:, v0:v0 + RET_DV])
        g0 = 2 * RET_QK_DIM + RET_V_DIM + hd * RET_DV
        g = _dot(h, win_ref[:, g0:g0 + RET_DV])
        q = rope(q)
        k = rope(k) * (RET_DK ** -0.5)
        for blk in range(x.shape[0] // RET_BLOCK):
            rs = slice(blk * RET_BLOCK, (blk + 1) * RET_BLOCK)
            qb = q[rs].astype(jnp.bfloat16)
            kb = k[rs].astype(jnp.bfloat16)
            vb = v[rs].astype(jnp.bfloat16)
            sc = lax.dot_general(qb, kb, (((1,), (1,)), ((), ())),
                                 preferred_element_type=jnp.float32)
            sc = sc * dmat_ref[hd]
            intra = _dot(sc.astype(jnp.bfloat16), vb)
            st = state_ref[hd]
            cross = _dot(qb, st.astype(jnp.bfloat16)) * qdec_ref[hd]
            o = intra + cross
            kd = (k[rs] * kdec_ref[hd]).astype(jnp.bfloat16)
            upd = lax.dot_general(kd, vb, (((0,), (0,)), ((), ())),
                                  preferred_element_type=jnp.float32)
            state_ref[hd] = st * cdec_ref[hd] + upd
            mu = jnp.mean(o, axis=-1, keepdims=True)
            oc = o - mu
            var = jnp.mean(oc * oc, axis=-1, keepdims=True)
            on = oc * lax.rsqrt(var + EPS)
            gated_ref[rs, hd * RET_DV:(hd + 1) * RET_DV] = (_silu(g[rs]) * on).astype(jnp.bfloat16)

    x1 = x + _dot(gated_ref[...], wout_ref[...])
    _store_rows(x1_ref, x1)
    route_ref[...] = _route(x1, gain2_ref[...], wr_ref[...], br_ref[...], tri_ref[...], cnt_ref)


def _tri(n):
    i = jnp.arange(n)
    return (i[:, None] < i[None, :]).astype(jnp.bfloat16)


def _const_spec(shape):
    nd = len(shape)
    return pl.BlockSpec(shape, lambda *_: (0,) * nd, pipeline_mode=pl.Buffered(1))


def _retention_layer(x2d, batch, seq, gain, w_in, w_out, gain2, wr, br):
    tb = RET_TILE
    n_s = seq // tb
    total = batch * seq
    rb = RET_BLOCK
    log_g = jnp.log(1.0 - jnp.exp2(-5.0 - jnp.arange(RET_HEADS, dtype=jnp.float32)))
    n = jnp.arange(rb, dtype=jnp.float32)
    diff = n[:, None] - n[None, :]
    cn = jnp.arange(rb)[:, None] // CHUNK
    cm = jnp.arange(rb)[None, :] // CHUNK
    expo = jnp.where(cn == cm, jnp.abs(diff), diff)
    dmat = jnp.where((cm <= cn)[None], jnp.exp(log_g[:, None, None] * expo[None]), 0.0)
    qdec = jnp.exp(log_g[:, None] * (n + 1.0)[None])[:, :, None]
    kdec = jnp.exp(log_g[:, None] * (rb - 1.0 - n)[None])[:, :, None]
    cdec = jnp.broadcast_to(jnp.exp(log_g * rb)[:, None, None], (RET_HEADS, 1, RET_DV))
    half = RET_DK // 2
    inv_freq = 1.0 / (ROPE_BASE ** (jnp.arange(half, dtype=jnp.float32) / half))
    ang = jnp.arange(seq, dtype=jnp.float32)[:, None] * inv_freq[None, :]
    cos, sin = jnp.cos(ang), jnp.sin(ang)

    tile = lambda b, s: (b * n_s + s, 0)
    return pl.pallas_call(
        _ret_kernel,
        grid=(batch, n_s),
        in_specs=[
            pl.BlockSpec((tb, D_MODEL), tile),
            _const_spec((1, D_MODEL)),
            _const_spec(w_in.shape),
            _const_spec(w_out.shape),
            pl.BlockSpec((tb, half), lambda b, s: (s, 0)),
            pl.BlockSpec((tb, half), lambda b, s: (s, 0)),
            _const_spec(dmat.shape),
            _const_spec(qdec.shape),
            _const_spec(kdec.shape),
            _const_spec(cdec.shape),
            _const_spec((1, D_MODEL)),
            _const_spec(wr.shape),
            _const_spec(br.shape),
            _const_spec((tb, tb)),
        ],
        out_specs=[
            pl.BlockSpec((tb * TILES_PER_ROW, LANES), tile),
            pl.BlockSpec((SUBLANES, tb), lambda b, s: (0, b * n_s + s)),
        ],
        out_shape=[
            jax.ShapeDtypeStruct((total * TILES_PER_ROW, LANES), jnp.float32),
            jax.ShapeDtypeStruct((SUBLANES, total), jnp.float32),
        ],
        scratch_shapes=[
            pltpu.VMEM((RET_HEADS, RET_DK, RET_DV), jnp.float32),
            pltpu.VMEM((tb, RET_V_DIM), jnp.bfloat16),
            pltpu.VMEM((LANES, 1), jnp.float32),
        ],
        compiler_params=pltpu.CompilerParams(
            dimension_semantics=("arbitrary", "arbitrary"),
            vmem_limit_bytes=VMEM_LIMIT),
        name="retention_layer",
    )(x2d, gain, w_in, w_out, cos, sin, dmat, qdec, kdec, cdec, gain2, wr, br, _tri(tb))


def _pool_kernel(x_ref, gain_ref, pw_ref, scale_ref, gain2_ref, wr_ref, br_ref, tri_ref,
                 x1_ref, route_ref, ext_ref, y_ref, cnt_ref):
    s = pl.program_id(1)
    tb = x_ref.shape[0] // TILES_PER_ROW

    @pl.when(s == 0)
    def _():
        ext_ref[0:POOL_HIST, :] = jnp.zeros((POOL_HIST, D_MODEL), jnp.float32)

    @pl.when(jnp.logical_and(pl.program_id(0) == 0, s == 0))
    def _():
        cnt_ref[...] = jnp.zeros_like(cnt_ref)

    x = _load_rows(x_ref, tb)
    ext_ref[POOL_HIST:, :] = _rms(x, gain_ref[...])
    pos = s * tb + lax.broadcasted_iota(jnp.int32, (tb, 1), 0)
    for g, w in enumerate(POOL_WINDOWS):
        cs = slice(g * POOL_GROUP_DIM, (g + 1) * POOL_GROUP_DIM)
        e = ext_ref[:, cs]
        acc = e
        k = 1
        while k < w:
            acc = acc + pltpu.roll(acc, k, axis=0)
            k *= 2
        inv_cnt = 1.0 / jnp.minimum(pos + 1, w).astype(jnp.float32)
        pooled = acc[POOL_HIST:, :] * inv_cnt - e[POOL_HIST:, :]
        y_ref[:, cs] = _dot(pooled.astype(jnp.bfloat16), pw_ref[g])
    ext_ref[0:POOL_HIST, :] = ext_ref[tb:tb + POOL_HIST, :]
    x1 = x + y_ref[...] * scale_ref[...]
    _store_rows(x1_ref, x1)
    route_ref[...] = _route(x1, gain2_ref[...], wr_ref[...], br_ref[...], tri_ref[...], cnt_ref)


def _pool_layer(x2d, batch, seq, gain, pw, scale, gain2, wr, br):
    tb = POOL_TILE
    n_s = seq // tb
    total = batch * seq
    tile = lambda b, s: (b * n_s + s, 0)
    return pl.pallas_call(
        _pool_kernel,
        grid=(batch, n_s),
        in_specs=[
            pl.BlockSpec((tb * TILES_PER_ROW, LANES), tile),
            _const_spec((1, D_MODEL)),
            _const_spec(pw.shape),
            _const_spec((1, D_MODEL)),
            _const_spec((1, D_MODEL)),
            _const_spec(wr.shape),
            _const_spec(br.shape),
            _const_spec((tb, tb)),
        ],
        out_specs=[
            pl.BlockSpec((tb * TILES_PER_ROW, LANES), tile),
            pl.BlockSpec((SUBLANES, tb), lambda b, s: (0, b * n_s + s)),
        ],
        out_shape=[
            jax.ShapeDtypeStruct((total * TILES_PER_ROW, LANES), jnp.float32),
            jax.ShapeDtypeStruct((SUBLANES, total), jnp.float32),
        ],
        scratch_shapes=[
            pltpu.VMEM((POOL_HIST + tb, D_MODEL), jnp.float32),
            pltpu.VMEM((tb, D_MODEL), jnp.float32),
            pltpu.VMEM((LANES, 1), jnp.float32),
        ],
        compiler_params=pltpu.CompilerParams(
            dimension_semantics=("arbitrary", "arbitrary"),
            vmem_limit_bytes=VMEM_LIMIT),
        name="pool_layer",
    )(x2d, gain, pw, scale, gain2, wr, br, _tri(tb))


def _tile_of(row):
    return pl.ds(pl.multiple_of(row * TILES_PER_ROW, TILES_PER_ROW), TILES_PER_ROW)


def _dispatch_kernel(slot_ref, run_start_ref, run_len_ref, n_used_ref,
                     x_ref, out_hbm, zero_ref, sem, sem_pad):
    i = pl.program_id(0)
    tile = x_ref.shape[0] // TILES_PER_ROW

    @pl.when(i == 0)
    def _():
        zero_ref[...] = jnp.zeros_like(zero_ref)

        def pad_copy(c):
            rows = pl.multiple_of(run_len_ref[c] * TILES_PER_ROW, TILES_PER_ROW)
            first = pl.multiple_of(run_start_ref[c] * TILES_PER_ROW, TILES_PER_ROW)
            return pltpu.make_async_copy(zero_ref.at[pl.ds(0, rows)], out_hbm.at[pl.ds(first, rows)],
                                         sem_pad.at[0])

        def block_copy(blk):
            first = pl.multiple_of(blk * zero_ref.shape[0], zero_ref.shape[0])
            return pltpu.make_async_copy(zero_ref, out_hbm.at[pl.ds(first, zero_ref.shape[0])],
                                         sem_pad.at[0])

        n_blocks = out_hbm.shape[0] // zero_ref.shape[0]
        for c in range(N_CLASSES):
            @pl.when(run_len_ref[c] > 0)
            def _():
                pad_copy(c).start()
        lax.fori_loop(n_used_ref[0], n_blocks, lambda blk, c: (block_copy(blk).start(), c)[1], 0)
        for c in range(N_CLASSES):
            @pl.when(run_len_ref[c] > 0)
            def _():
                pad_copy(c).wait()
        lax.fori_loop(n_used_ref[0], n_blocks, lambda blk, c: (block_copy(blk).wait(), c)[1], 0)

    for r in range(tile):
        slot = slot_ref[i * tile + r]
        pltpu.make_async_copy(x_ref.at[_tile_of(r)], out_hbm.at[_tile_of(slot)],
                              sem.at[0]).start(priority=r % DMA_THREADS)
    pltpu.make_async_copy(x_ref, out_hbm.at[pl.ds(0, x_ref.shape[0])], sem.at[0]).wait()


def _dispatch(x_tiled, slot_of_tok, run_start, run_len, n_used, n_slots):
    total = slot_of_tok.shape[0]
    tile = DISPATCH_TILE
    grid_spec = pltpu.PrefetchScalarGridSpec(
        num_scalar_prefetch=4,
        grid=(total // tile,),
        in_specs=[pl.BlockSpec((tile * TILES_PER_ROW, LANES), lambda i, *_: (i, 0))],
        out_specs=pl.BlockSpec(memory_space=pl.ANY),
        scratch_shapes=[
            pltpu.VMEM((MOE_ROWS * TILES_PER_ROW, LANES), jnp.float32),
            pltpu.SemaphoreType.DMA((1,)),
            pltpu.SemaphoreType.DMA((1,)),
        ],
    )
    return pl.pallas_call(
        _dispatch_kernel,
        grid_spec=grid_spec,
        out_shape=jax.ShapeDtypeStruct((n_slots * TILES_PER_ROW, LANES), jnp.float32),
        compiler_params=pltpu.CompilerParams(
            dimension_semantics=("arbitrary",),
            vmem_limit_bytes=VMEM_LIMIT),
        name="moe_dispatch",
    )(slot_of_tok, run_start, run_len, n_used, x_tiled)


def _moe_kernel(dst_ref, nv_ref, ea_ref, eb_ref, xstep_ref,
                x_ref, wts_ref, gain_ref, fgain_ref, *rest, final_norm, total):
    n_w = 2 * len(MOE_WEIGHTS) * MOE_STEP_BLOCKS
    w_refs = rest[:n_w]
    out_hbm, obuf, sem_out = rest[n_w:]
    i = pl.program_id(0)
    blk0 = i * MOE_STEP_BLOCKS
    n_first = nv_ref[blk0]
    n_first_prev = nv_ref[jnp.maximum(blk0 - MOE_STEP_BLOCKS, 0)]
    osl = i % MOE_OBUFS
    step_rows = MOE_STEP_BLOCKS * MOE_ROWS

    def counts(step):
        first = jnp.maximum(step, 0) * MOE_STEP_BLOCKS
        return [nv_ref[first + j] for j in range(MOE_STEP_BLOCKS)]

    def out_copy(sl, r, tok):
        if final_norm:
            return pltpu.make_async_copy(obuf.at[sl, pl.ds(r, 1)], out_hbm.at[pl.ds(tok, 1)],
                                         sem_out.at[sl])
        return pltpu.make_async_copy(obuf.at[sl, _tile_of(r)], out_hbm.at[_tile_of(tok)],
                                     sem_out.at[sl])

    def start_scatter(step, sl, rows):
        def unrolled(j):
            for r in range(j * MOE_ROWS, (j + 1) * MOE_ROWS):
                out_copy(sl, r, dst_ref[step * step_rows + r]).start(priority=r % DMA_THREADS)

        for j in range(MOE_STEP_BLOCKS):
            if not final_norm:
                unrolled(j)
                continue
            pl.when(rows[j] == MOE_ROWS)(functools.partial(unrolled, j))

            @pl.when(rows[j] < MOE_ROWS)
            def _():
                def body(r, c):
                    out_copy(sl, j * MOE_ROWS + r, dst_ref[step * step_rows + j * MOE_ROWS + r]).start()
                    return c
                lax.fori_loop(0, rows[j], body, 0)

    def wait_scatter(sl, rows):
        if not final_norm:
            pltpu.make_async_copy(obuf.at[sl], out_hbm.at[pl.ds(0, obuf.shape[1])],
                                  sem_out.at[sl]).wait()
            return
        for j in range(MOE_STEP_BLOCKS):
            tiled = pl.multiple_of((rows[j] // SUBLANES) * SUBLANES, SUBLANES)

            @pl.when(tiled > 0)
            def _():
                pltpu.make_async_copy(obuf.at[sl, pl.ds(0, tiled)], out_hbm.at[pl.ds(0, tiled)],
                                      sem_out.at[sl]).wait()

            def body(r, c):
                out_copy(sl, 0, 0).wait()
                return c
            lax.fori_loop(tiled, rows[j], body, 0)

    if not final_norm:
        @pl.when(i == 0)
        def _():
            obuf[...] = jnp.zeros(obuf.shape, jnp.float32)
            for r in range(step_rows):
                out_copy(0, r, total + r).start()

    @pl.when(n_first > 0)
    def _():
        tile_rows = MOE_ROWS * TILES_PER_ROW
        blocks = range(MOE_STEP_BLOCKS)
        if not final_norm:
            start_scatter(i, 1 - osl, None)
        xs = [_load_rows(x_ref.at[pl.ds(j * tile_rows, tile_rows)], MOE_ROWS) for j in blocks]
        hs = [_rms(x, gain_ref[...]).astype(jnp.bfloat16) for x in xs]
        wts = [wts_ref[j].T for j in blocks]
        def weight(j, e, kind):
            return w_refs[(2 * j + e) * len(MOE_WEIGHTS) + MOE_WEIGHTS.index(kind)][0, 0]

        experts = [(j, e) for j in blocks for e in range(2)]
        gates = {je: _dot(hs[je[0]], weight(*je, "gate")) for je in experts}
        ups = {je: _dot(hs[je[0]], weight(*je, "up")) for je in experts}
        hid = {je: (_silu(gates[je]) * ups[je]).astype(jnp.bfloat16) for je in experts}
        outs = [xs[j] + wts[j][:, 0:1] * _dot(hid[j, 0], weight(j, 0, "down"))
                + wts[j][:, 1:2] * _dot(hid[j, 1], weight(j, 1, "down")) for j in blocks]
        if final_norm:
            pl.when(i >= MOE_OBUFS)(lambda: wait_scatter(osl, counts(i - MOE_OBUFS)))
            for j in blocks:
                obuf[osl, j * MOE_ROWS:(j + 1) * MOE_ROWS] = _rms(outs[j], fgain_ref[...])
        else:
            wait_scatter(osl, None)
            for j in blocks:
                _store_rows(obuf.at[osl, pl.ds(j * tile_rows, tile_rows)], outs[j])
        if final_norm:
            start_scatter(i, osl, counts(i))

    @pl.when(jnp.logical_and(n_first == 0, n_first_prev > 0))
    def _():
        if final_norm:
            wait_scatter(1 - osl, counts(i - 1))
            pl.when(i >= MOE_OBUFS)(lambda: wait_scatter(osl, counts(i - MOE_OBUFS)))
        else:
            start_scatter(i, 1 - osl, None)
            for sl in range(MOE_OBUFS):
                wait_scatter(sl, None)


def _class_tables():
    ea = np.zeros((N_CLASSES,), np.int32)
    eb = np.zeros((N_CLASSES,), np.int32)
    for g in range(MOE_GROUPS):
        c = g * PAIRS_PER_GROUP
        for a in range(MOE_EPG):
            for b in range(a + 1, MOE_EPG):
                ea[c], eb[c] = g * MOE_EPG + a, g * MOE_EPG + b
                c += 1
    return ea, eb


def _moe_layer(x_tiled, route, gain, fgain, weights, layer, final_norm):
    total = route.shape[1]
    nb = (total // MOE_ROWS + N_CLASSES) // MOE_STEP_BLOCKS * MOE_STEP_BLOCKS + MOE_STEP_BLOCKS
    n_steps = nb // MOE_STEP_BLOCKS
    step_rows = MOE_STEP_BLOCKS * MOE_ROWS
    n_slots = nb * MOE_ROWS
    w_lo, w_hi = route[2], route[3]
    rank = route[4].astype(jnp.int32)
    cls = route[5].astype(jnp.int32)
    cls_ids = jnp.arange(N_CLASSES, dtype=jnp.int32)
    onehot_tok = (cls[:, None] == cls_ids[None, :]).astype(jnp.int32)
    counts = jnp.sum(onehot_tok, axis=0)
    nblk = (counts + MOE_ROWS - 1) // MOE_ROWS
    npad = nblk * MOE_ROWS - counts
    blk_end = jnp.cumsum(nblk)
    cls_start = (blk_end - nblk) * MOE_ROWS
    slot_of_tok = jnp.sum(onehot_tok * cls_start[None, :], axis=1) + rank
    x_sorted = _dispatch(x_tiled, slot_of_tok, cls_start + counts, npad, blk_end[-1:], n_slots)

    max_pad = MOE_ROWS - 1
    cand_cls = jnp.repeat(cls_ids, max_pad)
    cand_on = jnp.tile(jnp.arange(max_pad, dtype=jnp.int32), N_CLASSES) < jnp.repeat(npad, max_pad)
    n_fill = n_slots - total - N_CLASSES * max_pad
    tok_bits = 16
    assert total <= 1 << tok_bits
    keys = jnp.concatenate([
        (2 * cls << tok_bits) + jnp.arange(total, dtype=jnp.int32),
        jnp.where(cand_on, 2 * cand_cls + 1, 2 * N_CLASSES + 1) << tok_bits,
        jnp.full((n_fill,), (2 * N_CLASSES + 1) << tok_bits, jnp.int32)])
    zeros = jnp.zeros((n_slots - total,), jnp.float32)
    key_s, wlo_s, whi_s = lax.sort(
        (keys, jnp.concatenate([w_lo, zeros]), jnp.concatenate([w_hi, zeros])),
        num_keys=1, is_stable=False)
    is_pad = ((key_s >> tok_bits) & 1) == 1
    tok_s = key_s & ((1 << tok_bits) - 1)
    slot = jnp.arange(n_slots, dtype=jnp.int32)
    dump = total + ((slot // step_rows) % MOE_OBUFS) * step_rows + slot % step_rows
    dst = jnp.where(is_pad, dump, tok_s)
    if not final_norm:
        dst = jnp.concatenate([total + step_rows + jnp.arange(step_rows, dtype=jnp.int32),
                               dst[:-step_rows]])
    nvalid = jnp.sum((~is_pad).reshape(nb, MOE_ROWS).astype(jnp.int32), axis=1)
    wts = jnp.concatenate(
        [wlo_s.reshape(nb, 1, MOE_ROWS), whi_s.reshape(nb, 1, MOE_ROWS),
         jnp.zeros((nb, SUBLANES - 2, MOE_ROWS), jnp.float32)], axis=1)
    bidx = jnp.arange(nb, dtype=jnp.int32)
    xblk = jnp.minimum(bidx, blk_end[-1] - 1)
    c_blk = jnp.sum((blk_end[None, :] <= xblk[:, None]).astype(jnp.int32), axis=1)
    xstep = jnp.minimum(jnp.arange(n_steps, dtype=jnp.int32), (blk_end[-1] - 1) // MOE_STEP_BLOCKS)
    ea_tab, eb_tab = _class_tables()
    onehot_blk = (c_blk[:, None] == cls_ids[None, :]).astype(jnp.int32)
    ea = jnp.sum(onehot_blk * jnp.asarray(ea_tab)[None, :], axis=1)
    eb = jnp.sum(onehot_blk * jnp.asarray(eb_tab)[None, :], axis=1)

    tiled_block = (step_rows * TILES_PER_ROW, LANES)
    if final_norm:
        out_block = (step_rows, D_MODEL)
        out_shape = (total, D_MODEL)
    else:
        out_block = tiled_block
        out_shape = ((total + MOE_OBUFS * step_rows) * TILES_PER_ROW, LANES)

    def weight_specs(j):
        def spec(shape, tab):
            return pl.BlockSpec(shape, lambda i, dst, nv, ea, eb, xstep:
                                (layer, (ea, eb)[tab][i * MOE_STEP_BLOCKS + j], 0, 0))
        shapes = {"gate": (1, 1, D_MODEL, MOE_D_FF), "up": (1, 1, D_MODEL, MOE_D_FF),
                  "down": (1, 1, MOE_D_FF, D_MODEL)}
        return [spec(shapes[kind], tab) for tab in range(2) for kind in MOE_WEIGHTS]

    grid_spec = pltpu.PrefetchScalarGridSpec(
        num_scalar_prefetch=5,
        grid=(n_steps,),
        in_specs=[
            pl.BlockSpec(tiled_block, lambda i, dst, nv, ea, eb, xstep: (xstep[i], 0)),
            pl.BlockSpec((MOE_STEP_BLOCKS, SUBLANES, MOE_ROWS), lambda i, *_: (i, 0, 0)),
            pl.BlockSpec((1, D_MODEL), lambda i, *_: (0, 0)),
            pl.BlockSpec((1, D_MODEL), lambda i, *_: (0, 0)),
        ] + [s for j in range(MOE_STEP_BLOCKS) for s in weight_specs(j)],
        out_specs=pl.BlockSpec(memory_space=pl.ANY),
        scratch_shapes=[
            pltpu.VMEM((MOE_OBUFS,) + out_block, jnp.float32),
            pltpu.SemaphoreType.DMA((MOE_OBUFS,)),
        ],
    )
    return pl.pallas_call(
        functools.partial(_moe_kernel, final_norm=final_norm, total=total),
        grid_spec=grid_spec,
        out_shape=jax.ShapeDtypeStruct(out_shape, jnp.float32),
        compiler_params=pltpu.CompilerParams(
            dimension_semantics=("arbitrary",),
            vmem_limit_bytes=VMEM_LIMIT),
        name="moe_layer",
    )(dst, nvalid, ea, eb, xstep, x_sorted, wts, gain, fgain, *(list(weights) * (2 * MOE_STEP_BLOCKS)))


def _router_params(w_group, b_group, w_expert, b_expert):
    gap = ROUTE_EXPERT_COL - MOE_GROUPS
    pad = LANES - ROUTE_EXPERT_COL - MOE_EXPERTS
    wr = jnp.concatenate([w_group, jnp.zeros((D_MODEL, gap), jnp.float32), w_expert,
                          jnp.zeros((D_MODEL, pad), jnp.float32)], axis=1)
    br = jnp.concatenate([b_group, jnp.zeros((gap,), jnp.float32), b_expert.reshape(-1),
                          jnp.zeros((pad,), jnp.float32)])[None, :]
    wr_hi = wr.astype(jnp.bfloat16)
    wr_lo = (wr - wr_hi.astype(jnp.float32)).astype(jnp.bfloat16)
    return jnp.concatenate([wr_hi, wr_lo], axis=1), br


def kernel(x, norm_mix, norm_ffn, ret_w_in, ret_w_out, pool_w, pool_scale, moe_w_group, moe_b_group, moe_w_expert, moe_b_expert, moe_w_gate, moe_w_up, moe_w_down, final_norm):
    batch, seq, _ = x.shape
    x2d = x.reshape(batch * seq, D_MODEL)
    bf = jnp.bfloat16
    weights = (moe_w_gate.astype(bf), moe_w_up.astype(bf), moe_w_down.astype(bf))
    fgain = final_norm[None, :]

    wr0, br0 = _router_params(moe_w_group[0], moe_b_group[0], moe_w_expert[0], moe_b_expert[0])
    x1, route0 = _retention_layer(x2d, batch, seq, norm_mix[0][None, :], ret_w_in[0].astype(bf),
                                  ret_w_out[0].astype(bf), norm_ffn[0][None, :], wr0, br0)
    x2 = _moe_layer(x1, route0, norm_ffn[0][None, :], fgain, weights, 0, final_norm=False)

    wr1, br1 = _router_params(moe_w_group[1], moe_b_group[1], moe_w_expert[1], moe_b_expert[1])
    x3, route1 = _pool_layer(x2, batch, seq, norm_mix[1][None, :], pool_w[0].astype(bf),
                             pool_scale[0][None, :], norm_ffn[1][None, :], wr1, br1)
    out = _moe_layer(x3, route1, norm_ffn[1][None, :], fgain, weights, 1, final_norm=True)
    return out.reshape(batch, seq, D_MODEL)
```

```python
import functools

import numpy as np
import jax
import jax.numpy as jnp
from jax import lax
from jax.experimental import pallas as pl
from jax.experimental.pallas import tpu as pltpu

D_MODEL = 1024
EPS = 1e-6
CHUNK = 64

RET_HEADS = 4
RET_DK = 256
RET_DV = 512
RET_QK_DIM = RET_HEADS * RET_DK
RET_V_DIM = RET_HEADS * RET_DV
ROPE_BASE = 10000.0

POOL_WINDOWS = (2, 4, 8, 16)
POOL_GROUP_DIM = 256
POOL_HIST = 16

MOE_GROUPS = 4
MOE_EPG = 8
MOE_EXPERTS = MOE_GROUPS * MOE_EPG
MOE_D_FF = 256
PAIRS_PER_GROUP = MOE_EPG * (MOE_EPG - 1) // 2
N_CLASSES = MOE_GROUPS * PAIRS_PER_GROUP

LANES = 128
SUBLANES = 8

RET_TILE = 256
RET_BLOCK = 256
POOL_TILE = 1024
MOE_ROWS = 128
MOE_STEP_BLOCKS = 2
MOE_WEIGHTS = ("gate", "up", "down")
DISPATCH_TILE = 2048
MOE_OBUFS = 2
DMA_THREADS = 1
VMEM_LIMIT = 56 * 1024 * 1024


def _rms(x, gain):
    ms = jnp.mean(x * x, axis=-1, keepdims=True)
    return x * lax.rsqrt(ms + EPS) * gain


def _silu(x):
    return x * jax.nn.sigmoid(x)


def _dot(a, b):
    return jnp.dot(a, b, preferred_element_type=jnp.float32)


ROUTE_EXPERT_COL = 8
TILES_PER_ROW = D_MODEL // LANES


def _load_rows(ref, rows):
    return jnp.concatenate(
        [ref[pl.ds(s, rows, stride=TILES_PER_ROW), :] for s in range(TILES_PER_ROW)], axis=1)


def _store_rows(ref, val):
    rows = val.shape[0]
    for s in range(TILES_PER_ROW):
        ref[pl.ds(s, rows, stride=TILES_PER_ROW), :] = val[:, s * LANES:(s + 1) * LANES]


def _route(x1, gain2, wr2, br, tri, cnt_ref):
    rows = x1.shape[0]
    h2 = _rms(x1, gain2)
    h_hi = h2.astype(jnp.bfloat16)
    h_lo = (h2 - h_hi.astype(jnp.float32)).astype(jnp.bfloat16)
    parts = _dot(jnp.concatenate([h_hi, h_lo], axis=0), wr2)
    logits = (parts[:rows, :LANES] + parts[rows:, :LANES]) + parts[:rows, LANES:] + br
    lt = logits.T
    sub = lax.broadcasted_iota(jnp.int32, (SUBLANES, rows), 0)
    neg = jnp.float32(-jnp.inf)
    gl = jnp.where(sub < MOE_GROUPS, lt[0:SUBLANES], neg)
    gmax = jnp.max(gl, axis=0, keepdims=True)
    grp = jnp.min(jnp.where(gl == gmax, sub, SUBLANES), axis=0, keepdims=True)
    g_gate = 1.0 / jnp.sum(jnp.exp(gl - gmax), axis=0, keepdims=True)
    el = lt[ROUTE_EXPERT_COL:ROUTE_EXPERT_COL + MOE_EPG]
    for g in range(1, MOE_GROUPS):
        first = ROUTE_EXPERT_COL + g * MOE_EPG
        el = jnp.where(grp == g, lt[first:first + MOE_EPG], el)
    m1 = jnp.max(el, axis=0, keepdims=True)
    i1 = jnp.min(jnp.where(el == m1, sub, SUBLANES), axis=0, keepdims=True)
    el2 = jnp.where(sub == i1, neg, el)
    m2 = jnp.max(el2, axis=0, keepdims=True)
    i2 = jnp.min(jnp.where(el2 == m2, sub, SUBLANES), axis=0, keepdims=True)
    t = jnp.exp(m2 - m1)
    w1 = g_gate / (1.0 + t)
    w2 = g_gate * t / (1.0 + t)
    first_lo = i1 < i2
    a = jnp.where(first_lo, i1, i2)
    bb = jnp.where(first_lo, i2, i1)
    base = grp * MOE_EPG
    lo = (base + a).astype(jnp.float32)
    hi = (base + bb).astype(jnp.float32)
    w_lo = jnp.where(first_lo, w1, w2)
    w_hi = jnp.where(first_lo, w2, w1)
    cls = grp * PAIRS_PER_GROUP + ((a * (2 * MOE_EPG - 1 - a)) >> 1) + (bb - a - 1)
    onehot = (lax.broadcasted_iota(jnp.int32, (LANES, rows), 0) == cls).astype(jnp.float32)
    earlier = _dot(onehot.astype(jnp.bfloat16), tri) + cnt_ref[...]
    rank = jnp.sum(onehot * earlier, axis=0, keepdims=True)
    cnt_ref[...] += jnp.sum(onehot, axis=1, keepdims=True)
    out = jnp.where(sub == 0, lo, jnp.where(sub == 1, hi, jnp.where(sub == 2, w_lo,
                    jnp.where(sub == 3, w_hi, 0.0))))
    return jnp.where(sub == 4, rank, jnp.where(sub == 5, cls.astype(jnp.float32), out))


def _ret_kernel(x_ref, gain_ref, win_ref, wout_ref, cos_ref, sin_ref, dmat_ref,
                qdec_ref, kdec_ref, cdec_ref, gain2_ref, wr_ref, br_ref, tri_ref,
                x1_ref, route_ref, state_ref, gated_ref, cnt_ref):
    @pl.when(pl.program_id(1) == 0)
    def _():
        state_ref[...] = jnp.zeros_like(state_ref)

    @pl.when(jnp.logical_and(pl.program_id(0) == 0, pl.program_id(1) == 0))
    def _():
        cnt_ref[...] = jnp.zeros_like(cnt_ref)

    x = x_ref[...]
    h = _rms(x, gain_ref[...]).astype(jnp.bfloat16)
    cos = cos_ref[...]
    sin = sin_ref[...]
    half = RET_DK // 2

    def rope(t):
        t1, t2 = t[:, :half], t[:, half:]
        return jnp.concatenate([t1 * cos - t2 * sin, t1 * sin + t2 * cos], axis=1)

    for hd in range(RET_HEADS):
        q = _dot(h, win_ref[:, hd * RET_DK:(hd + 1) * RET_DK])
        k = _dot(h, win_ref[:, RET_QK_DIM + hd * RET_DK:RET_QK_DIM + (hd + 1) * RET_DK])
        v0 = 2 * RET_QK_DIM + hd * RET_DV
        v = _dot(h, win_ref[:, v0:v0 + RET_DV])
        g0 = 2 * RET_QK_DIM + RET_V_DIM + hd * RET_DV
        g = _dot(h, win_ref[:, g0:g0 + RET_DV])
        q = rope(q)
        k = rope(k) * (RET_DK ** -0.5)
        for blk in range(x.shape[0] // RET_BLOCK):
            rs = slice(blk * RET_BLOCK, (blk + 1) * RET_BLOCK)
            qb = q[rs].astype(jnp.bfloat16)
            kb = k[rs].astype(jnp.bfloat16)
            vb = v[rs].astype(jnp.bfloat16)
            sc = lax.dot_general(qb, kb, (((1,), (1,)), ((), ())),
                                 preferred_element_type=jnp.float32)
            sc = sc * dmat_ref[hd]
            intra = _dot(sc.astype(jnp.bfloat16), vb)
            st = state_ref[hd]
            cross = _dot(qb, st.astype(jnp.bfloat16)) * qdec_ref[hd]
            o = intra + cross
            kd = (k[rs] * kdec_ref[hd]).astype(jnp.bfloat16)
            upd = lax.dot_general(kd, vb, (((0,), (0,)), ((), ())),
                                  preferred_element_type=jnp.float32)
            state_ref[hd] = st * cdec_ref[hd] + upd
            mu = jnp.mean(o, axis=-1, keepdims=True)
            oc = o - mu
            var = jnp.mean(oc * oc, axis=-1, keepdims=True)
            on = oc * lax.rsqrt(var + EPS)
            gated_ref[rs, hd * RET_DV:(hd + 1) * RET_DV] = (_silu(g[rs]) * on).astype(jnp.bfloat16)

    x1 = x + _dot(gated_ref[...], wout_ref[...])
    _store_rows(x1_ref, x1)
    route_ref[...] = _route(x1, gain2_ref[...], wr_ref[...], br_ref[...], tri_ref[...], cnt_ref)


def _tri(n):
    i = jnp.arange(n)
    return (i[:, None] < i[None, :]).astype(jnp.bfloat16)


def _const_spec(shape):
    nd = len(shape)
    return pl.BlockSpec(shape, lambda *_: (0,) * nd, pipeline_mode=pl.Buffered(1))


def _retention_layer(x2d, batch, seq, gain, w_in, w_out, gain2, wr, br):
    tb = RET_TILE
    n_s = seq // tb
    total = batch * seq
    rb = RET_BLOCK
    log_g = jnp.log(1.0 - jnp.exp2(-5.0 - jnp.arange(RET_HEADS, dtype=jnp.float32)))
    n = jnp.arange(rb, dtype=jnp.float32)
    diff = n[:, None] - n[None, :]
    cn = jnp.arange(rb)[:, None] // CHUNK
    cm = jnp.arange(rb)[None, :] // CHUNK
    expo = jnp.where(cn == cm, jnp.abs(diff), diff)
    dmat = jnp.where((cm <= cn)[None], jnp.exp(log_g[:, None, None] * expo[None]), 0.0)
    qdec = jnp.exp(log_g[:, None] * (n + 1.0)[None])[:, :, None]
    kdec = jnp.exp(log_g[:, None] * (rb - 1.0 - n)[None])[:, :, None]
    cdec = jnp.broadcast_to(jnp.exp(log_g * rb)[:, None, None], (RET_HEADS, 1, RET_DV))
    half = RET_DK // 2
    inv_freq = 1.0 / (ROPE_BASE ** (jnp.arange(half, dtype=jnp.float32) / half))
    ang = jnp.arange(seq, dtype=jnp.float32)[:, None] * inv_freq[None, :]
    cos, sin = jnp.cos(ang), jnp.sin(ang)

    tile = lambda b, s: (b * n_s + s, 0)
    return pl.pallas_call(
        _ret_kernel,
        grid=(batch, n_s),
        in_specs=[
            pl.BlockSpec((tb, D_MODEL), tile),
            _const_spec((1, D_MODEL)),
            _const_spec(w_in.shape),
            _const_spec(w_out.shape),
            pl.BlockSpec((tb, half), lambda b, s: (s, 0)),
            pl.BlockSpec((tb, half), lambda b, s: (s, 0)),
            _const_spec(dmat.shape),
            _const_spec(qdec.shape),
            _const_spec(kdec.shape),
            _const_spec(cdec.shape),
            _const_spec((1, D_MODEL)),
            _const_spec(wr.shape),
            _const_spec(br.shape),
            _const_spec((tb, tb)),
        ],
        out_specs=[
            pl.BlockSpec((tb * TILES_PER_ROW, LANES), tile),
            pl.BlockSpec((SUBLANES, tb), lambda b, s: (0, b * n_s + s)),
        ],
        out_shape=[
            jax.ShapeDtypeStruct((total * TILES_PER_ROW, LANES), jnp.float32),
            jax.ShapeDtypeStruct((SUBLANES, total), jnp.float32),
        ],
        scratch_shapes=[
            pltpu.VMEM((RET_HEADS, RET_DK, RET_DV), jnp.float32),
            pltpu.VMEM((tb, RET_V_DIM), jnp.bfloat16),
            pltpu.VMEM((LANES, 1), jnp.float32),
        ],
        compiler_params=pltpu.CompilerParams(
            dimension_semantics=("arbitrary", "arbitrary"),
            vmem_limit_bytes=VMEM_LIMIT),
        name="retention_layer",
    )(x2d, gain, w_in, w_out, cos, sin, dmat, qdec, kdec, cdec, gain2, wr, br, _tri(tb))


def _pool_kernel(x_ref, gain_ref, pw_ref, scale_ref, gain2_ref, wr_ref, br_ref, tri_ref,
                 x1_ref, route_ref, ext_ref, y_ref, cnt_ref):
    s = pl.program_id(1)
    tb = x_ref.shape[0] // TILES_PER_ROW

    @pl.when(s == 0)
    def _():
        ext_ref[0:POOL_HIST, :] = jnp.zeros((POOL_HIST, D_MODEL), jnp.float32)

    @pl.when(jnp.logical_and(pl.program_id(0) == 0, s == 0))
    def _():
        cnt_ref[...] = jnp.zeros_like(cnt_ref)

    x = _load_rows(x_ref, tb)
    ext_ref[POOL_HIST:, :] = _rms(x, gain_ref[...])
    pos = s * tb + lax.broadcasted_iota(jnp.int32, (tb, 1), 0)
    for g, w in enumerate(POOL_WINDOWS):
        cs = slice(g * POOL_GROUP_DIM, (g + 1) * POOL_GROUP_DIM)
        e = ext_ref[:, cs]
        acc = e
        k = 1
        while k < w:
            acc = acc + pltpu.roll(acc, k, axis=0)
            k *= 2
        inv_cnt = 1.0 / jnp.minimum(pos + 1, w).astype(jnp.float32)
        pooled = acc[POOL_HIST:, :] * inv_cnt - e[POOL_HIST:, :]
        y_ref[:, cs] = _dot(pooled.astype(jnp.bfloat16), pw_ref[g])
    ext_ref[0:POOL_HIST, :] = ext_ref[tb:tb + POOL_HIST, :]
    x1 = x + y_ref[...] * scale_ref[...]
    _store_rows(x1_ref, x1)
    route_ref[...] = _route(x1, gain2_ref[...], wr_ref[...], br_ref[...], tri_ref[...], cnt_ref)


def _pool_layer(x2d, batch, seq, gain, pw, scale, gain2, wr, br):
    tb = POOL_TILE
    n_s = seq // tb
    total = batch * seq
    tile = lambda b, s: (b * n_s + s, 0)
    return pl.pallas_call(
        _pool_kernel,
        grid=(batch, n_s),
        in_specs=[
            pl.BlockSpec((tb * TILES_PER_ROW, LANES), tile),
            _const_spec((1, D_MODEL)),
            _const_spec(pw.shape),
            _const_spec((1, D_MODEL)),
            _const_spec((1, D_MODEL)),
            _const_spec(wr.shape),
            _const_spec(br.shape),
            _const_spec((tb, tb)),
        ],
        out_specs=[
            pl.BlockSpec((tb * TILES_PER_ROW, LANES), tile),
            pl.BlockSpec((SUBLANES, tb), lambda b, s: (0, b * n_s + s)),
        ],
        out_shape=[
            jax.ShapeDtypeStruct((total * TILES_PER_ROW, LANES), jnp.float32),
            jax.ShapeDtypeStruct((SUBLANES, total), jnp.float32),
        ],
        scratch_shapes=[
            pltpu.VMEM((POOL_HIST + tb, D_MODEL), jnp.float32),
            pltpu.VMEM((tb, D_MODEL), jnp.float32),
            pltpu.VMEM((LANES, 1), jnp.float32),
        ],
        compiler_params=pltpu.CompilerParams(
            dimension_semantics=("arbitrary", "arbitrary"),
            vmem_limit_bytes=VMEM_LIMIT),
        name="pool_layer",
    )(x2d, gain, pw, scale, gain2, wr, br, _tri(tb))


def _tile_of(row):
    return pl.ds(pl.multiple_of(row * TILES_PER_ROW, TILES_PER_ROW), TILES_PER_ROW)


def _dispatch_kernel(slot_ref, run_start_ref, run_len_ref, n_used_ref,
                     x_ref, out_hbm, zero_ref, sem, sem_pad):
    i = pl.program_id(0)
    tile = x_ref.shape[0] // TILES_PER_ROW

    @pl.when(i == 0)
    def _():
        zero_ref[...] = jnp.zeros_like(zero_ref)

        def pad_copy(c):
            rows = pl.multiple_of(run_len_ref[c] * TILES_PER_ROW, TILES_PER_ROW)
            first = pl.multiple_of(run_start_ref[c] * TILES_PER_ROW, TILES_PER_ROW)
            return pltpu.make_async_copy(zero_ref.at[pl.ds(0, rows)], out_hbm.at[pl.ds(first, rows)],
                                         sem_pad.at[0])

        def block_copy(blk):
            first = pl.multiple_of(blk * zero_ref.shape[0], zero_ref.shape[0])
            return pltpu.make_async_copy(zero_ref, out_hbm.at[pl.ds(first, zero_ref.shape[0])],
                                         sem_pad.at[0])

        n_blocks = out_hbm.shape[0] // zero_ref.shape[0]
        for c in range(N_CLASSES):
            @pl.when(run_len_ref[c] > 0)
            def _():
                pad_copy(c).start()
        lax.fori_loop(n_used_ref[0], n_blocks, lambda blk, c: (block_copy(blk).start(), c)[1], 0)
        for c in range(N_CLASSES):
            @pl.when(run_len_ref[c] > 0)
            def _():
                pad_copy(c).wait()
        lax.fori_loop(n_used_ref[0], n_blocks, lambda blk, c: (block_copy(blk).wait(), c)[1], 0)

    for r in range(tile):
        slot = slot_ref[i * tile + r]
        pltpu.make_async_copy(x_ref.at[_tile_of(r)], out_hbm.at[_tile_of(slot)],
                              sem.at[0]).start(priority=r % DMA_THREADS)
    pltpu.make_async_copy(x_ref, out_hbm.at[pl.ds(0, x_ref.shape[0])], sem.at[0]).wait()


def _dispatch(x_tiled, slot_of_tok, run_start, run_len, n_used, n_slots):
    total = slot_of_tok.shape[0]
    tile = DISPATCH_TILE
    grid_spec = pltpu.PrefetchScalarGridSpec(
        num_scalar_prefetch=4,
        grid=(total // tile,),
        in_specs=[pl.BlockSpec((tile * TILES_PER_ROW, LANES), lambda i, *_: (i, 0))],
        out_specs=pl.BlockSpec(memory_space=pl.ANY),
        scratch_shapes=[
            pltpu.VMEM((MOE_ROWS * TILES_PER_ROW, LANES), jnp.float32),
            pltpu.SemaphoreType.DMA((1,)),
            pltpu.SemaphoreType.DMA((1,)),
        ],
    )
    return pl.pallas_call(
        _dispatch_kernel,
        grid_spec=grid_spec,
        out_shape=jax.ShapeDtypeStruct((n_slots * TILES_PER_ROW, LANES), jnp.float32),
        compiler_params=pltpu.CompilerParams(
            dimension_semantics=("arbitrary",),
            vmem_limit_bytes=VMEM_LIMIT),
        name="moe_dispatch",
    )(slot_of_tok, run_start, run_len, n_used, x_tiled)


def _moe_kernel(dst_ref, nv_ref, ea_ref, eb_ref, xstep_ref,
                x_ref, wts_ref, gain_ref, fgain_ref, *rest, final_norm, total):
    n_w = 2 * len(MOE_WEIGHTS) * MOE_STEP_BLOCKS
    w_refs = rest[:n_w]
    out_hbm, obuf, sem_out = rest[n_w:]
    i = pl.program_id(0)
    blk0 = i * MOE_STEP_BLOCKS
    n_first = nv_ref[blk0]
    n_first_prev = nv_ref[jnp.maximum(blk0 - MOE_STEP_BLOCKS, 0)]
    osl = i % MOE_OBUFS
    step_rows = MOE_STEP_BLOCKS * MOE_ROWS

    def counts(step):
        first = jnp.maximum(step, 0) * MOE_STEP_BLOCKS
        return [nv_ref[first + j] for j in range(MOE_STEP_BLOCKS)]

    def out_copy(sl, r, tok):
        if final_norm:
            return pltpu.make_async_copy(obuf.at[sl, pl.ds(r, 1)], out_hbm.at[pl.ds(tok, 1)],
                                         sem_out.at[sl])
        return pltpu.make_async_copy(obuf.at[sl, _tile_of(r)], out_hbm.at[_tile_of(tok)],
                                     sem_out.at[sl])

    def start_scatter(step, sl, rows):
        def unrolled(j):
            for r in range(j * MOE_ROWS, (j + 1) * MOE_ROWS):
                out_copy(sl, r, dst_ref[step * step_rows + r]).start(priority=r % DMA_THREADS)

        for j in range(MOE_STEP_BLOCKS):
            if not final_norm:
                unrolled(j)
                continue
            pl.when(rows[j] == MOE_ROWS)(functools.partial(unrolled, j))

            @pl.when(rows[j] < MOE_ROWS)
            def _():
                def body(r, c):
                    out_copy(sl, j * MOE_ROWS + r, dst_ref[step * step_rows + j * MOE_ROWS + r]).start()
                    return c
                lax.fori_loop(0, rows[j], body, 0)

    def wait_scatter(sl, rows):
        if not final_norm:
            pltpu.make_async_copy(obuf.at[sl], out_hbm.at[pl.ds(0, obuf.shape[1])],
                                  sem_out.at[sl]).wait()
            return
        for j in range(MOE_STEP_BLOCKS):
            tiled = pl.multiple_of((rows[j] // SUBLANES) * SUBLANES, SUBLANES)

            @pl.when(tiled > 0)
            def _():
                pltpu.make_async_copy(obuf.at[sl, pl.ds(0, tiled)], out_hbm.at[pl.ds(0, tiled)],
                                      sem_out.at[sl]).wait()

            def body(r, c):
                out_copy(sl, 0, 0).wait()
                return c
            lax.fori_loop(tiled, rows[j], body, 0)

    if not final_norm:
        @pl.when(i == 0)
        def _():
            obuf[...] = jnp.zeros(obuf.shape, jnp.float32)
            for r in range(step_rows):
                out_copy(0, r, total + r).start()

    @pl.when(n_first > 0)
    def _():
        tile_rows = MOE_ROWS * TILES_PER_ROW
        blocks = range(MOE_STEP_BLOCKS)
        if not final_norm:
            start_scatter(i, 1 - osl, None)
        xs = [_load_rows(x_ref.at[pl.ds(j * tile_rows, tile_rows)], MOE_ROWS) for j in blocks]
        hs = [_rms(x, gain_ref[...]).astype(jnp.bfloat16) for x in xs]
        wts = [wts_ref[j].T for j in blocks]
        def weight(j, e, kind):
            return w_refs[(2 * j + e) * len(MOE_WEIGHTS) + MOE_WEIGHTS.index(kind)][0, 0]

        experts = [(j, e) for j in blocks for e in range(2)]
        gates = {je: _dot(hs[je[0]], weight(*je, "gate")) for je in experts}
        ups = {je: _dot(hs[je[0]], weight(*je, "up")) for je in experts}
        hid = {je: (_silu(gates[je]) * ups[je]).astype(jnp.bfloat16) for je in experts}
        outs = [xs[j] + wts[j][:, 0:1] * _dot(hid[j, 0], weight(j, 0, "down"))
                + wts[j][:, 1:2] * _dot(hid[j, 1], weight(j, 1, "down")) for j in blocks]
        if final_norm:
            pl.when(i >= MOE_OBUFS)(lambda: wait_scatter(osl, counts(i - MOE_OBUFS)))
            for j in blocks:
                obuf[osl, j * MOE_ROWS:(j + 1) * MOE_ROWS] = _rms(outs[j], fgain_ref[...])
        else:
            wait_scatter(osl, None)
            for j in blocks:
                _store_rows(obuf.at[osl, pl.ds(j * tile_rows, tile_rows)], outs[j])
        if final_norm:
            start_scatter(i, osl, counts(i))

    @pl.when(jnp.logical_and(n_first == 0, n_first_prev > 0))
    def _():
        if final_norm:
            wait_scatter(1 - osl, counts(i - 1))
            pl.when(i >= MOE_OBUFS)(lambda: wait_scatter(osl, counts(i - MOE_OBUFS)))
        else:
            start_scatter(i, 1 - osl, None)
            for sl in range(MOE_OBUFS):
                wait_scatter(sl, None)


def _class_tables():
    ea = np.zeros((N_CLASSES,), np.int32)
    eb = np.zeros((N_CLASSES,), np.int32)
    for g in range(MOE_GROUPS):
        c = g * PAIRS_PER_GROUP
        for a in range(MOE_EPG):
            for b in range(a + 1, MOE_EPG):
                ea[c], eb[c] = g * MOE_EPG + a, g * MOE_EPG + b
                c += 1
    return ea, eb


def _moe_layer(x_tiled, route, gain, fgain, weights, layer, final_norm):
    total = route.shape[1]
    nb = (total // MOE_ROWS + N_CLASSES) // MOE_STEP_BLOCKS * MOE_STEP_BLOCKS + MOE_STEP_BLOCKS
    n_steps = nb // MOE_STEP_BLOCKS
    step_rows = MOE_STEP_BLOCKS * MOE_ROWS
    n_slots = nb * MOE_ROWS
    w_lo, w_hi = route[2], route[3]
    rank = route[4].astype(jnp.int32)
    cls = route[5].astype(jnp.int32)
    cls_ids = jnp.arange(N_CLASSES, dtype=jnp.int32)
    onehot_tok = (cls[:, None] == cls_ids[None, :]).astype(jnp.int32)
    counts = jnp.sum(onehot_tok, axis=0)
    nblk = (counts + MOE_ROWS - 1) // MOE_ROWS
    npad = nblk * MOE_ROWS - counts
    blk_end = jnp.cumsum(nblk)
    cls_start = (blk_end - nblk) * MOE_ROWS
    slot_of_tok = jnp.sum(onehot_tok * cls_start[None, :], axis=1) + rank
    x_sorted = _dispatch(x_tiled, slot_of_tok, cls_start + counts, npad, blk_end[-1:], n_slots)

    max_pad = MOE_ROWS - 1
    cand_cls = jnp.repeat(cls_ids, max_pad)
    cand_on = jnp.tile(jnp.arange(max_pad, dtype=jnp.int32), N_CLASSES) < jnp.repeat(npad, max_pad)
    n_fill = n_slots - total - N_CLASSES * max_pad
    tok_bits = 16
    assert total <= 1 << tok_bits
    keys = jnp.concatenate([
        (2 * cls << tok_bits) + jnp.arange(total, dtype=jnp.int32),
        jnp.where(cand_on, 2 * cand_cls + 1, 2 * N_CLASSES + 1) << tok_bits,
        jnp.full((n_fill,), (2 * N_CLASSES + 1) << tok_bits, jnp.int32)])
    zeros = jnp.zeros((n_slots - total,), jnp.float32)
    key_s, wlo_s, whi_s = lax.sort(
        (keys, jnp.concatenate([w_lo, zeros]), jnp.concatenate([w_hi, zeros])),
        num_keys=1, is_stable=False)
    is_pad = ((key_s >> tok_bits) & 1) == 1
    tok_s = key_s & ((1 << tok_bits) - 1)
    slot = jnp.arange(n_slots, dtype=jnp.int32)
    dump = total + ((slot // step_rows) % MOE_OBUFS) * step_rows + slot % step_rows
    dst = jnp.where(is_pad, dump, tok_s)
    if not final_norm:
        dst = jnp.concatenate([total + step_rows + jnp.arange(step_rows, dtype=jnp.int32),
                               dst[:-step_rows]])
    nvalid = jnp.sum((~is_pad).reshape(nb, MOE_ROWS).astype(jnp.int32), axis=1)
    wts = jnp.concatenate(
        [wlo_s.reshape(nb, 1, MOE_ROWS), whi_s.reshape(nb, 1, MOE_ROWS),
         jnp.zeros((nb, SUBLANES - 2, MOE_ROWS), jnp.float32)], axis=1)
    bidx = jnp.arange(nb, dtype=jnp.int32)
    xblk = jnp.minimum(bidx, blk_end[-1] - 1)
    c_blk = jnp.sum((blk_end[None, :] <= xblk[:, None]).astype(jnp.int32), axis=1)
    xstep = jnp.minimum(jnp.arange(n_steps, dtype=jnp.int32), (blk_end[-1] - 1) // MOE_STEP_BLOCKS)
    ea_tab, eb_tab = _class_tables()
    onehot_blk = (c_blk[:, None] == cls_ids[None, :]).astype(jnp.int32)
    ea = jnp.sum(onehot_blk * jnp.asarray(ea_tab)[None, :], axis=1)
    eb = jnp.sum(onehot_blk * jnp.asarray(eb_tab)[None, :], axis=1)

    tiled_block = (step_rows * TILES_PER_ROW, LANES)
    if final_norm:
        out_block = (step_rows, D_MODEL)
        out_shape = (total, D_MODEL)
    else:
        out_block = tiled_block
        out_shape = ((total + MOE_OBUFS * step_rows) * TILES_PER_ROW, LANES)

    def weight_specs(j):
        def spec(shape, tab):
            return pl.BlockSpec(shape, lambda i, dst, nv, ea, eb, xstep:
                                (layer, (ea, eb)[tab][i * MOE_STEP_BLOCKS + j], 0, 0))
        shapes = {"gate": (1, 1, D_MODEL, MOE_D_FF), "up": (1, 1, D_MODEL, MOE_D_FF),
                  "down": (1, 1, MOE_D_FF, D_MODEL)}
        return [spec(shapes[kind], tab) for tab in range(2) for kind in MOE_WEIGHTS]

    grid_spec = pltpu.PrefetchScalarGridSpec(
        num_scalar_prefetch=5,
        grid=(n_steps,),
        in_specs=[
            pl.BlockSpec(tiled_block, lambda i, dst, nv, ea, eb, xstep: (xstep[i], 0)),
            pl.BlockSpec((MOE_STEP_BLOCKS, SUBLANES, MOE_ROWS), lambda i, *_: (i, 0, 0)),
            pl.BlockSpec((1, D_MODEL), lambda i, *_: (0, 0)),
            pl.BlockSpec((1, D_MODEL), lambda i, *_: (0, 0)),
        ] + [s for j in range(MOE_STEP_BLOCKS) for s in weight_specs(j)],
        out_specs=pl.BlockSpec(memory_space=pl.ANY),
        scratch_shapes=[
            pltpu.VMEM((MOE_OBUFS,) + out_block, jnp.float32),
            pltpu.SemaphoreType.DMA((MOE_OBUFS,)),
        ],
    )
    return pl.pallas_call(
        functools.partial(_moe_kernel, final_norm=final_norm, total=total),
        grid_spec=grid_spec,
        out_shape=jax.ShapeDtypeStruct(out_shape, jnp.float32),
        compiler_params=pltpu.CompilerParams(
            dimension_semantics=("arbitrary",),
            vmem_limit_bytes=VMEM_LIMIT),
        name="moe_layer",
    )(dst, nvalid, ea, eb, xstep, x_sorted, wts, gain, fgain, *(list(weights) * (2 * MOE_STEP_BLOCKS)))


def _router_params(w_group, b_group, w_expert, b_expert):
    gap = ROUTE_EXPERT_COL - MOE_GROUPS
    pad = LANES - ROUTE_EXPERT_COL - MOE_EXPERTS
    wr = jnp.concatenate([w_group, jnp.zeros((D_MODEL, gap), jnp.float32), w_expert,
                          jnp.zeros((D_MODEL, pad), jnp.float32)], axis=1)
    br = jnp.concatenate([b_group, jnp.zeros((gap,), jnp.float32), b_expert.reshape(-1),
                          jnp.zeros((pad,), jnp.float32)])[None, :]
    wr_hi = wr.astype(jnp.bfloat16)
    wr_lo = (wr - wr_hi.astype(jnp.float32)).astype(jnp.bfloat16)
    return jnp.concatenate([wr_hi, wr_lo], axis=1), br


def kernel(x, norm_mix, norm_ffn, ret_w_in, ret_w_out, pool_w, pool_scale, moe_w_group, moe_b_group, moe_w_expert, moe_b_expert, moe_w_gate, moe_w_up, moe_w_down, final_norm):
    batch, seq, _ = x.shape
    x2d = x.reshape(batch * seq, D_MODEL)
    bf = jnp.bfloat16
    weights = (moe_w_gate.astype(bf), moe_w_up.astype(bf), moe_w_down.astype(bf))
    fgain = final_norm[None, :]

    wr0, br0 = _router_params(moe_w_group[0], moe_b_group[0], moe_w_expert[0], moe_b_expert[0])
    x1, route0 = _retention_layer(x2d, batch, seq, norm_mix[0][None, :], ret_w_in[0].astype(bf),
                                  ret_w_out[0].astype(bf), norm_ffn[0][None, :], wr0, br0)
    x2 = _moe_layer(x1, route0, norm_ffn[0][None, :], fgain, weights, 0, final_norm=False)

    wr1, br1 = _router_params(moe_w_group[1], moe_b_group[1], moe_w_expert[1], moe_b_expert[1])
    x3, route1 = _pool_layer(x2, batch, seq, norm_mix[1][None, :], pool_w[0].astype(bf),
                             pool_scale[0][None, :], norm_ffn[1][None, :], wr1, br1)
    out = _moe_layer(x3, route1, norm_ffn[1][None, :], fgain, weights, 1, final_norm=True)
    return out.reshape(batch, seq, D_MODEL)
```

```python
import functools

import numpy as np
import jax
import jax.numpy as jnp
from jax import lax
from jax.experimental import pallas as pl
from jax.experimental.pallas import tpu as pltpu

D_MODEL = 1024
EPS = 1e-6
CHUNK = 64

RET_HEADS = 4
RET_DK = 256
RET_DV = 512
RET_QK_DIM = RET_HEADS * RET_DK
RET_V_DIM = RET_HEADS * RET_DV
ROPE_BASE = 10000.0

POOL_WINDOWS = (2, 4, 8, 16)
POOL_GROUP_DIM = 256
POOL_HIST = 16

MOE_GROUPS = 4
MOE_EPG = 8
MOE_EXPERTS = MOE_GROUPS * MOE_EPG
MOE_D_FF = 256
PAIRS_PER_GROUP = MOE_EPG * (MOE_EPG - 1) // 2
N_CLASSES = MOE_GROUPS * PAIRS_PER_GROUP

LANES = 128
SUBLANES = 8

RET_TILE = 256
RET_BLOCK = 256
POOL_TILE = 1024
MOE_ROWS = 128
MOE_STEP_BLOCKS = 2
MOE_WEIGHTS = ("gate", "up", "down")
DISPATCH_TILE = 4096
MOE_OBUFS = 2
DMA_THREADS = 2
VMEM_LIMIT = 56 * 1024 * 1024


def _rms(x, gain):
    ms = jnp.mean(x * x, axis=-1, keepdims=True)
    return x * lax.rsqrt(ms + EPS) * gain


def _silu(x):
    return x * jax.nn.sigmoid(x)


def _dot(a, b):
    return jnp.dot(a, b, preferred_element_type=jnp.float32)


ROUTE_EXPERT_COL = 8
TILES_PER_ROW = D_MODEL // LANES


def _load_rows(ref, rows):
    return jnp.concatenate(
        [ref[pl.ds(s, rows, stride=TILES_PER_ROW), :] for s in range(TILES_PER_ROW)], axis=1)


def _store_rows(ref, val):
    rows = val.shape[0]
    for s in range(TILES_PER_ROW):
        ref[pl.ds(s, rows, stride=TILES_PER_ROW), :] = val[:, s * LANES:(s + 1) * LANES]


def _route(x1, gain2, wr2, br, tri, cnt_ref):
    rows = x1.shape[0]
    h2 = _rms(x1, gain2)
    h_hi = h2.astype(jnp.bfloat16)
    h_lo = (h2 - h_hi.astype(jnp.float32)).astype(jnp.bfloat16)
    parts = _dot(jnp.concatenate([h_hi, h_lo], axis=0), wr2)
    logits = (parts[:rows, :LANES] + parts[rows:, :LANES]) + parts[:rows, LANES:] + br
    lt = logits.T
    sub = lax.broadcasted_iota(jnp.int32, (SUBLANES, rows), 0)
    neg = jnp.float32(-jnp.inf)
    gl = jnp.where(sub < MOE_GROUPS, lt[0:SUBLANES], neg)
    gmax = jnp.max(gl, axis=0, keepdims=True)
    grp = jnp.min(jnp.where(gl == gmax, sub, SUBLANES), axis=0, keepdims=True)
    g_gate = 1.0 / jnp.sum(jnp.exp(gl - gmax), axis=0, keepdims=True)
    el = lt[ROUTE_EXPERT_COL:ROUTE_EXPERT_COL + MOE_EPG]
    for g in range(1, MOE_GROUPS):
        first = ROUTE_EXPERT_COL + g * MOE_EPG
        el = jnp.where(grp == g, lt[first:first + MOE_EPG], el)
    m1 = jnp.max(el, axis=0, keepdims=True)
    i1 = jnp.min(jnp.where(el == m1, sub, SUBLANES), axis=0, keepdims=True)
    el2 = jnp.where(sub == i1, neg, el)
    m2 = jnp.max(el2, axis=0, keepdims=True)
    i2 = jnp.min(jnp.where(el2 == m2, sub, SUBLANES), axis=0, keepdims=True)
    t = jnp.exp(m2 - m1)
    w1 = g_gate / (1.0 + t)
    w2 = g_gate * t / (1.0 + t)
    first_lo = i1 < i2
    a = jnp.where(first_lo, i1, i2)
    bb = jnp.where(first_lo, i2, i1)
    base = grp * MOE_EPG
    lo = (base + a).astype(jnp.float32)
    hi = (base + bb).astype(jnp.float32)
    w_lo = jnp.where(first_lo, w1, w2)
    w_hi = jnp.where(first_lo, w2, w1)
    cls = grp * PAIRS_PER_GROUP + ((a * (2 * MOE_EPG - 1 - a)) >> 1) + (bb - a - 1)
    onehot = (lax.broadcasted_iota(jnp.int32, (LANES, rows), 0) == cls).astype(jnp.float32)
    earlier = _dot(onehot.astype(jnp.bfloat16), tri) + cnt_ref[...]
    rank = jnp.sum(onehot * earlier, axis=0, keepdims=True)
    cnt_ref[...] += jnp.sum(onehot, axis=1, keepdims=True)
    out = jnp.where(sub == 0, lo, jnp.where(sub == 1, hi, jnp.where(sub == 2, w_lo,
                    jnp.where(sub == 3, w_hi, 0.0))))
    return jnp.where(sub == 4, rank, jnp.where(sub == 5, cls.astype(jnp.float32), out))


def _ret_kernel(x_ref, gain_ref, win_ref, wout_ref, cos_ref, sin_ref, dmat_ref,
                qdec_ref, kdec_ref, cdec_ref, gain2_ref, wr_ref, br_ref, tri_ref,
                x1_ref, route_ref, state_ref, gated_ref, cnt_ref):
    @pl.when(pl.program_id(1) == 0)
    def _():
        state_ref[...] = jnp.zeros_like(state_ref)

    @pl.when(jnp.logical_and(pl.program_id(0) == 0, pl.program_id(1) == 0))
    def _():
        cnt_ref[...] = jnp.zeros_like(cnt_ref)

    x = x_ref[...]
    h = _rms(x, gain_ref[...]).astype(jnp.bfloat16)
    cos = cos_ref[...]
    sin = sin_ref[...]
    half = RET_DK // 2

    def rope(t):
        t1, t2 = t[:, :half], t[:, half:]
        return jnp.concatenate([t1 * cos - t2 * sin, t1 * sin + t2 * cos], axis=1)

    for hd in range(RET_HEADS):
        q = _dot(h, win_ref[:, hd * RET_DK:(hd + 1) * RET_DK])
        k = _dot(h, win_ref[:, RET_QK_DIM + hd * RET_DK:RET_QK_DIM + (hd + 1) * RET_DK])
        v0 = 2 * RET_QK_DIM + hd * RET_DV
        v = _dot(h, win_ref[:, v0:v0 + RET_DV])
        g0 = 2 * RET_QK_DIM + RET_V_DIM + hd * RET_DV
        g = _dot(h, win_ref[:, g0:g0 + RET_DV])
        q = rope(q)
        k = rope(k) * (RET_DK ** -0.5)
        for blk in range(x.shape[0] // RET_BLOCK):
            rs = slice(blk * RET_BLOCK, (blk + 1) * RET_BLOCK)
            qb = q[rs].astype(jnp.bfloat16)
            kb = k[rs].astype(jnp.bfloat16)
            vb = v[rs].astype(jnp.bfloat16)
            sc = lax.dot_general(qb, kb, (((1,), (1,)), ((), ())),
                                 preferred_element_type=jnp.float32)
            sc = sc * dmat_ref[hd]
            intra = _dot(sc.astype(jnp.bfloat16), vb)
            st = state_ref[hd]
            cross = _dot(qb, st.astype(jnp.bfloat16)) * qdec_ref[hd]
            o = intra + cross
            kd = (k[rs] * kdec_ref[hd]).astype(jnp.bfloat16)
            upd = lax.dot_general(kd, vb, (((0,), (0,)), ((), ())),
                                  preferred_element_type=jnp.float32)
            state_ref[hd] = st * cdec_ref[hd] + upd
            mu = jnp.mean(o, axis=-1, keepdims=True)
            oc = o - mu
            var = jnp.mean(oc * oc, axis=-1, keepdims=True)
            on = oc * lax.rsqrt(var + EPS)
            gated_ref[rs, hd * RET_DV:(hd + 1) * RET_DV] = (_silu(g[rs]) * on).astype(jnp.bfloat16)

    x1 = x + _dot(gated_ref[...], wout_ref[...])
    _store_rows(x1_ref, x1)
    route_ref[...] = _route(x1, gain2_ref[...], wr_ref[...], br_ref[...], tri_ref[...], cnt_ref)


def _tri(n):
    i = jnp.arange(n)
    return (i[:, None] < i[None, :]).astype(jnp.bfloat16)


def _const_spec(shape):
    nd = len(shape)
    return pl.BlockSpec(shape, lambda *_: (0,) * nd, pipeline_mode=pl.Buffered(1))


def _retention_layer(x2d, batch, seq, gain, w_in, w_out, gain2, wr, br):
    tb = RET_TILE
    n_s = seq // tb
    total = batch * seq
    rb = RET_BLOCK
    log_g = jnp.log(1.0 - jnp.exp2(-5.0 - jnp.arange(RET_HEADS, dtype=jnp.float32)))
    n = jnp.arange(rb, dtype=jnp.float32)
    diff = n[:, None] - n[None, :]
    cn = jnp.arange(rb)[:, None] // CHUNK
    cm = jnp.arange(rb)[None, :] // CHUNK
    expo = jnp.where(cn == cm, jnp.abs(diff), diff)
    dmat = jnp.where((cm <= cn)[None], jnp.exp(log_g[:, None, None] * expo[None]), 0.0)
    qdec = jnp.exp(log_g[:, None] * (n + 1.0)[None])[:, :, None]
    kdec = jnp.exp(log_g[:, None] * (rb - 1.0 - n)[None])[:, :, None]
    cdec = jnp.broadcast_to(jnp.exp(log_g * rb)[:, None, None], (RET_HEADS, 1, RET_DV))
    half = RET_DK // 2
    inv_freq = 1.0 / (ROPE_BASE ** (jnp.arange(half, dtype=jnp.float32) / half))
    ang = jnp.arange(seq, dtype=jnp.float32)[:, None] * inv_freq[None, :]
    cos, sin = jnp.cos(ang), jnp.sin(ang)

    tile = lambda b, s: (b * n_s + s, 0)
    return pl.pallas_call(
        _ret_kernel,
        grid=(batch, n_s),
        in_specs=[
            pl.BlockSpec((tb, D_MODEL), tile),
            _const_spec((1, D_MODEL)),
            _const_spec(w_in.shape),
            _const_spec(w_out.shape),
            pl.BlockSpec((tb, half), lambda b, s: (s, 0)),
            pl.BlockSpec((tb, half), lambda b, s: (s, 0)),
            _const_spec(dmat.shape),
            _const_spec(qdec.shape),
            _const_spec(kdec.shape),
            _const_spec(cdec.shape),
            _const_spec((1, D_MODEL)),
            _const_spec(wr.shape),
            _const_spec(br.shape),
            _const_spec((tb, tb)),
        ],
        out_specs=[
            pl.BlockSpec((tb * TILES_PER_ROW, LANES), tile),
            pl.BlockSpec((SUBLANES, tb), lambda b, s: (0, b * n_s + s)),
        ],
        out_shape=[
            jax.ShapeDtypeStruct((total * TILES_PER_ROW, LANES), jnp.float32),
            jax.ShapeDtypeStruct((SUBLANES, total), jnp.float32),
        ],
        scratch_shapes=[
            pltpu.VMEM((RET_HEADS, RET_DK, RET_DV), jnp.float32),
            pltpu.VMEM((tb, RET_V_DIM), jnp.bfloat16),
            pltpu.VMEM((LANES, 1), jnp.float32),
        ],
        compiler_params=pltpu.CompilerParams(
            dimension_semantics=("arbitrary", "arbitrary"),
            vmem_limit_bytes=VMEM_LIMIT),
        name="retention_layer",
    )(x2d, gain, w_in, w_out, cos, sin, dmat, qdec, kdec, cdec, gain2, wr, br, _tri(tb))


def _pool_kernel(x_ref, gain_ref, pw_ref, scale_ref, gain2_ref, wr_ref, br_ref, tri_ref,
                 x1_ref, route_ref, ext_ref, y_ref, cnt_ref):
    s = pl.program_id(1)
    tb = x_ref.shape[0] // TILES_PER_ROW

    @pl.when(s == 0)
    def _():
        ext_ref[0:POOL_HIST, :] = jnp.zeros((POOL_HIST, D_MODEL), jnp.float32)

    @pl.when(jnp.logical_and(pl.program_id(0) == 0, s == 0))
    def _():
        cnt_ref[...] = jnp.zeros_like(cnt_ref)

    x = _load_rows(x_ref, tb)
    ext_ref[POOL_HIST:, :] = _rms(x, gain_ref[...])
    pos = s * tb + lax.broadcasted_iota(jnp.int32, (POOL_HIST, 1), 0)
    for g, w in enumerate(POOL_WINDOWS):
        cs = slice(g * POOL_GROUP_DIM, (g + 1) * POOL_GROUP_DIM)
        e = ext_ref[:, cs]
        acc = e
        k = 1
        while k < w:
            acc = acc + pltpu.roll(acc, k, axis=0)
            k *= 2
        inv_cnt = 1.0 / jnp.minimum(pos + 1, w).astype(jnp.float32)
        head = slice(POOL_HIST, 2 * POOL_HIST)
        pooled = jnp.concatenate(
            [acc[head, :] * inv_cnt - e[head, :],
             acc[2 * POOL_HIST:, :] * (1.0 / w) - e[2 * POOL_HIST:, :]], axis=0)
        y_ref[:, cs] = _dot(pooled.astype(jnp.bfloat16), pw_ref[g])
    ext_ref[0:POOL_HIST, :] = ext_ref[tb:tb + POOL_HIST, :]
    x1 = x + y_ref[...] * scale_ref[...]
    _store_rows(x1_ref, x1)
    route_ref[...] = _route(x1, gain2_ref[...], wr_ref[...], br_ref[...], tri_ref[...], cnt_ref)


def _pool_layer(x2d, batch, seq, gain, pw, scale, gain2, wr, br):
    tb = POOL_TILE
    n_s = seq // tb
    total = batch * seq
    tile = lambda b, s: (b * n_s + s, 0)
    return pl.pallas_call(
        _pool_kernel,
        grid=(batch, n_s),
        in_specs=[
            pl.BlockSpec((tb * TILES_PER_ROW, LANES), tile),
            _const_spec((1, D_MODEL)),
            _const_spec(pw.shape),
            _const_spec((1, D_MODEL)),
            _const_spec((1, D_MODEL)),
            _const_spec(wr.shape),
            _const_spec(br.shape),
            _const_spec((tb, tb)),
        ],
        out_specs=[
            pl.BlockSpec((tb * TILES_PER_ROW, LANES), tile),
            pl.BlockSpec((SUBLANES, tb), lambda b, s: (0, b * n_s + s)),
        ],
        out_shape=[
            jax.ShapeDtypeStruct((total * TILES_PER_ROW, LANES), jnp.float32),
            jax.ShapeDtypeStruct((SUBLANES, total), jnp.float32),
        ],
        scratch_shapes=[
            pltpu.VMEM((POOL_HIST + tb, D_MODEL), jnp.float32),
            pltpu.VMEM((tb, D_MODEL), jnp.float32),
            pltpu.VMEM((LANES, 1), jnp.float32),
        ],
        compiler_params=pltpu.CompilerParams(
            dimension_semantics=("arbitrary", "arbitrary"),
            vmem_limit_bytes=VMEM_LIMIT),
        name="pool_layer",
    )(x2d, gain, pw, scale, gain2, wr, br, _tri(tb))


def _tile_of(row):
    return pl.ds(pl.multiple_of(row * TILES_PER_ROW, TILES_PER_ROW), TILES_PER_ROW)


def _dispatch_kernel(slot_ref, run_start_ref, run_len_ref, n_used_ref,
                     x_ref, out_hbm, zero_ref, sem, sem_pad):
    i = pl.program_id(0)
    tile = x_ref.shape[0] // TILES_PER_ROW

    @pl.when(i == 0)
    def _():
        zero_ref[...] = jnp.zeros_like(zero_ref)

        def pad_copy(c):
            rows = pl.multiple_of(run_len_ref[c] * TILES_PER_ROW, TILES_PER_ROW)
            first = pl.multiple_of(run_start_ref[c] * TILES_PER_ROW, TILES_PER_ROW)
            return pltpu.make_async_copy(zero_ref.at[pl.ds(0, rows)], out_hbm.at[pl.ds(first, rows)],
                                         sem_pad.at[0])

        def block_copy(blk):
            first = pl.multiple_of(blk * zero_ref.shape[0], zero_ref.shape[0])
            return pltpu.make_async_copy(zero_ref, out_hbm.at[pl.ds(first, zero_ref.shape[0])],
                                         sem_pad.at[0])

        n_blocks = out_hbm.shape[0] // zero_ref.shape[0]
        for c in range(N_CLASSES):
            @pl.when(run_len_ref[c] > 0)
            def _():
                pad_copy(c).start()
        lax.fori_loop(n_used_ref[0], n_blocks, lambda blk, c: (block_copy(blk).start(), c)[1], 0)
        for c in range(N_CLASSES):
            @pl.when(run_len_ref[c] > 0)
            def _():
                pad_copy(c).wait()
        lax.fori_loop(n_used_ref[0], n_blocks, lambda blk, c: (block_copy(blk).wait(), c)[1], 0)

    for r in range(tile):
        slot = slot_ref[i * tile + r]
        pltpu.make_async_copy(x_ref.at[_tile_of(r)], out_hbm.at[_tile_of(slot)],
                              sem.at[0]).start(priority=r % DMA_THREADS)
    pltpu.make_async_copy(x_ref, out_hbm.at[pl.ds(0, x_ref.shape[0])], sem.at[0]).wait()


def _dispatch(x_tiled, slot_of_tok, run_start, run_len, n_used, n_slots):
    total = slot_of_tok.shape[0]
    tile = DISPATCH_TILE
    grid_spec = pltpu.PrefetchScalarGridSpec(
        num_scalar_prefetch=4,
        grid=(total // tile,),
        in_specs=[pl.BlockSpec((tile * TILES_PER_ROW, LANES), lambda i, *_: (i, 0))],
        out_specs=pl.BlockSpec(memory_space=pl.ANY),
        scratch_shapes=[
            pltpu.VMEM((MOE_ROWS * TILES_PER_ROW, LANES), jnp.float32),
            pltpu.SemaphoreType.DMA((1,)),
            pltpu.SemaphoreType.DMA((1,)),
        ],
    )
    return pl.pallas_call(
        _dispatch_kernel,
        grid_spec=grid_spec,
        out_shape=jax.ShapeDtypeStruct((n_slots * TILES_PER_ROW, LANES), jnp.float32),
        compiler_params=pltpu.CompilerParams(
            dimension_semantics=("arbitrary",),
            vmem_limit_bytes=VMEM_LIMIT),
        name="moe_dispatch",
    )(slot_of_tok, run_start, run_len, n_used, x_tiled)


def _moe_kernel(dst_ref, nv_ref, ea_ref, eb_ref, xstep_ref,
                x_ref, wts_ref, gain_ref, fgain_ref, *rest, final_norm, total):
    n_w = 2 * len(MOE_WEIGHTS) * MOE_STEP_BLOCKS
    w_refs = rest[:n_w]
    out_hbm, obuf, sem_out = rest[n_w:]
    i = pl.program_id(0)
    blk0 = i * MOE_STEP_BLOCKS
    n_first = nv_ref[blk0]
    n_first_prev = nv_ref[jnp.maximum(blk0 - MOE_STEP_BLOCKS, 0)]
    osl = i % MOE_OBUFS
    step_rows = MOE_STEP_BLOCKS * MOE_ROWS

    def counts(step):
        first = jnp.maximum(step, 0) * MOE_STEP_BLOCKS
        return [nv_ref[first + j] for j in range(MOE_STEP_BLOCKS)]

    def out_copy(sl, r, tok):
        if final_norm:
            return pltpu.make_async_copy(obuf.at[sl, pl.ds(r, 1)], out_hbm.at[pl.ds(tok, 1)],
                                         sem_out.at[sl])
        return pltpu.make_async_copy(obuf.at[sl, _tile_of(r)], out_hbm.at[_tile_of(tok)],
                                     sem_out.at[sl])

    def start_scatter(step, sl, rows):
        def unrolled(j):
            for r in range(j * MOE_ROWS, (j + 1) * MOE_ROWS):
                out_copy(sl, r, dst_ref[step * step_rows + r]).start(priority=r % DMA_THREADS)

        for j in range(MOE_STEP_BLOCKS):
            if not final_norm:
                unrolled(j)
                continue
            pl.when(rows[j] == MOE_ROWS)(functools.partial(unrolled, j))

            @pl.when(rows[j] < MOE_ROWS)
            def _():
                def body(r, c):
                    out_copy(sl, j * MOE_ROWS + r, dst_ref[step * step_rows + j * MOE_ROWS + r]).start()
                    return c
                lax.fori_loop(0, rows[j], body, 0)

    def wait_scatter(sl, rows):
        if not final_norm:
            pltpu.make_async_copy(obuf.at[sl], out_hbm.at[pl.ds(0, obuf.shape[1])],
                                  sem_out.at[sl]).wait()
            return
        for j in range(MOE_STEP_BLOCKS):
            tiled = pl.multiple_of((rows[j] // SUBLANES) * SUBLANES, SUBLANES)

            @pl.when(tiled > 0)
            def _():
                pltpu.make_async_copy(obuf.at[sl, pl.ds(0, tiled)], out_hbm.at[pl.ds(0, tiled)],
                                      sem_out.at[sl]).wait()

            def body(r, c):
                out_copy(sl, 0, 0).wait()
                return c
            lax.fori_loop(tiled, rows[j], body, 0)

    if not final_norm:
        @pl.when(i == 0)
        def _():
            obuf[...] = jnp.zeros(obuf.shape, jnp.float32)
            for r in range(step_rows):
                out_copy(0, r, total + r).start()

    @pl.when(n_first > 0)
    def _():
        tile_rows = MOE_ROWS * TILES_PER_ROW
        blocks = range(MOE_STEP_BLOCKS)
        if not final_norm:
            start_scatter(i, 1 - osl, None)
        xs = [_load_rows(x_ref.at[pl.ds(j * tile_rows, tile_rows)], MOE_ROWS) for j in blocks]
        hs = [_rms(x, gain_ref[...]).astype(jnp.bfloat16) for x in xs]
        wts = [wts_ref[j].T for j in blocks]
        def weight(j, e, kind):
            return w_refs[(2 * j + e) * len(MOE_WEIGHTS) + MOE_WEIGHTS.index(kind)][0, 0]

        experts = [(j, e) for j in blocks for e in range(2)]
        gates = {je: _dot(hs[je[0]], weight(*je, "gate")) for je in experts}
        ups = {je: _dot(hs[je[0]], weight(*je, "up")) for je in experts}
        hid = {je: (_silu(gates[je]) * ups[je]).astype(jnp.bfloat16) for je in experts}
        outs = [xs[j] + wts[j][:, 0:1] * _dot(hid[j, 0], weight(j, 0, "down"))
                + wts[j][:, 1:2] * _dot(hid[j, 1], weight(j, 1, "down")) for j in blocks]
        if final_norm:
            pl.when(i >= MOE_OBUFS)(lambda: wait_scatter(osl, counts(i - MOE_OBUFS)))
            for j in blocks:
                obuf[osl, j * MOE_ROWS:(j + 1) * MOE_ROWS] = _rms(outs[j], fgain_ref[...])
        else:
            wait_scatter(osl, None)
            for j in blocks:
                _store_rows(obuf.at[osl, pl.ds(j * tile_rows, tile_rows)], outs[j])
        if final_norm:
            start_scatter(i, osl, counts(i))

    @pl.when(jnp.logical_and(n_first == 0, n_first_prev > 0))
    def _():
        if final_norm:
            wait_scatter(1 - osl, counts(i - 1))
            pl.when(i >= MOE_OBUFS)(lambda: wait_scatter(osl, counts(i - MOE_OBUFS)))
        else:
            start_scatter(i, 1 - osl, None)
            for sl in range(MOE_OBUFS):
                wait_scatter(sl, None)


def _class_tables():
    ea = np.zeros((N_CLASSES,), np.int32)
    eb = np.zeros((N_CLASSES,), np.int32)
    for g in range(MOE_GROUPS):
        c = g * PAIRS_PER_GROUP
        for a in range(MOE_EPG):
            for b in range(a + 1, MOE_EPG):
                ea[c], eb[c] = g * MOE_EPG + a, g * MOE_EPG + b
                c += 1
    return ea, eb


def _moe_layer(x_tiled, route, gain, fgain, weights, layer, final_norm):
    total = route.shape[1]
    nb = (total // MOE_ROWS + N_CLASSES) // MOE_STEP_BLOCKS * MOE_STEP_BLOCKS + MOE_STEP_BLOCKS
    n_steps = nb // MOE_STEP_BLOCKS
    step_rows = MOE_STEP_BLOCKS * MOE_ROWS
    n_slots = nb * MOE_ROWS
    w_lo, w_hi = route[2], route[3]
    rank = route[4].astype(jnp.int32)
    cls = route[5].astype(jnp.int32)
    cls_ids = jnp.arange(N_CLASSES, dtype=jnp.int32)
    onehot_tok = (cls[:, None] == cls_ids[None, :]).astype(jnp.int32)
    counts = jnp.sum(onehot_tok, axis=0)
    nblk = (counts + MOE_ROWS - 1) // MOE_ROWS
    npad = nblk * MOE_ROWS - counts
    blk_end = jnp.cumsum(nblk)
    cls_start = (blk_end - nblk) * MOE_ROWS
    slot_of_tok = jnp.sum(onehot_tok * cls_start[None, :], axis=1) + rank
    x_sorted = _dispatch(x_tiled, slot_of_tok, cls_start + counts, npad, blk_end[-1:], n_slots)

    max_pad = MOE_ROWS - 1
    cand_cls = jnp.repeat(cls_ids, max_pad)
    cand_on = jnp.tile(jnp.arange(max_pad, dtype=jnp.int32), N_CLASSES) < jnp.repeat(npad, max_pad)
    n_fill = n_slots - total - N_CLASSES * max_pad
    tok_bits = 16
    assert total <= 1 << tok_bits
    keys = jnp.concatenate([
        (2 * cls << tok_bits) + jnp.arange(total, dtype=jnp.int32),
        jnp.where(cand_on, 2 * cand_cls + 1, 2 * N_CLASSES + 1) << tok_bits,
        jnp.full((n_fill,), (2 * N_CLASSES + 1) << tok_bits, jnp.int32)])
    zeros = jnp.zeros((n_slots - total,), jnp.float32)
    key_s, wlo_s, whi_s = lax.sort(
        (keys, jnp.concatenate([w_lo, zeros]), jnp.concatenate([w_hi, zeros])),
        num_keys=1, is_stable=False)
    is_pad = ((key_s >> tok_bits) & 1) == 1
    tok_s = key_s & ((1 << tok_bits) - 1)
    slot = jnp.arange(n_slots, dtype=jnp.int32)
    dump = total + ((slot // step_rows) % MOE_OBUFS) * step_rows + slot % step_rows
    dst = jnp.where(is_pad, dump, tok_s)
    if not final_norm:
        dst = jnp.concatenate([total + step_rows + jnp.arange(step_rows, dtype=jnp.int32),
                               dst[:-step_rows]])
    nvalid = jnp.sum((~is_pad).reshape(nb, MOE_ROWS).astype(jnp.int32), axis=1)
    wts = jnp.concatenate(
        [wlo_s.reshape(nb, 1, MOE_ROWS), whi_s.reshape(nb, 1, MOE_ROWS),
         jnp.zeros((nb, SUBLANES - 2, MOE_ROWS), jnp.float32)], axis=1)
    bidx = jnp.arange(nb, dtype=jnp.int32)
    xblk = jnp.minimum(bidx, blk_end[-1] - 1)
    c_blk = jnp.sum((blk_end[None, :] <= xblk[:, None]).astype(jnp.int32), axis=1)
    xstep = jnp.minimum(jnp.arange(n_steps, dtype=jnp.int32), (blk_end[-1] - 1) // MOE_STEP_BLOCKS)
    ea_tab, eb_tab = _class_tables()
    onehot_blk = (c_blk[:, None] == cls_ids[None, :]).astype(jnp.int32)
    ea = jnp.sum(onehot_blk * jnp.asarray(ea_tab)[None, :], axis=1)
    eb = jnp.sum(onehot_blk * jnp.asarray(eb_tab)[None, :], axis=1)

    tiled_block = (step_rows * TILES_PER_ROW, LANES)
    if final_norm:
        out_block = (step_rows, D_MODEL)
        out_shape = (total, D_MODEL)
    else:
        out_block = tiled_block
        out_shape = ((total + MOE_OBUFS * step_rows) * TILES_PER_ROW, LANES)

    def weight_specs(j):
        def spec(shape, tab):
            return pl.BlockSpec(shape, lambda i, dst, nv, ea, eb, xstep:
                                (layer, (ea, eb)[tab][i * MOE_STEP_BLOCKS + j], 0, 0))
        shapes = {"gate": (1, 1, D_MODEL, MOE_D_FF), "up": (1, 1, D_MODEL, MOE_D_FF),
                  "down": (1, 1, MOE_D_FF, D_MODEL)}
        return [spec(shapes[kind], tab) for tab in range(2) for kind in MOE_WEIGHTS]

    grid_spec = pltpu.PrefetchScalarGridSpec(
        num_scalar_prefetch=5,
        grid=(n_steps,),
        in_specs=[
            pl.BlockSpec(tiled_block, lambda i, dst, nv, ea, eb, xstep: (xstep[i], 0)),
            pl.BlockSpec((MOE_STEP_BLOCKS, SUBLANES, MOE_ROWS), lambda i, *_: (i, 0, 0)),
            pl.BlockSpec((1, D_MODEL), lambda i, *_: (0, 0)),
            pl.BlockSpec((1, D_MODEL), lambda i, *_: (0, 0)),
        ] + [s for j in range(MOE_STEP_BLOCKS) for s in weight_specs(j)],
        out_specs=pl.BlockSpec(memory_space=pl.ANY),
        scratch_shapes=[
            pltpu.VMEM((MOE_OBUFS,) + out_block, jnp.float32),
            pltpu.SemaphoreType.DMA((MOE_OBUFS,)),
        ],
    )
    return pl.pallas_call(
        functools.partial(_moe_kernel, final_norm=final_norm, total=total),
        grid_spec=grid_spec,
        out_shape=jax.ShapeDtypeStruct(out_shape, jnp.float32),
        compiler_params=pltpu.CompilerParams(
            dimension_semantics=("arbitrary",),
            vmem_limit_bytes=VMEM_LIMIT),
        name="moe_layer",
    )(dst, nvalid, ea, eb, xstep, x_sorted, wts, gain, fgain, *(list(weights) * (2 * MOE_STEP_BLOCKS)))


def _router_params(w_group, b_group, w_expert, b_expert):
    gap = ROUTE_EXPERT_COL - MOE_GROUPS
    pad = LANES - ROUTE_EXPERT_COL - MOE_EXPERTS
    wr = jnp.concatenate([w_group, jnp.zeros((D_MODEL, gap), jnp.float32), w_expert,
                          jnp.zeros((D_MODEL, pad), jnp.float32)], axis=1)
    br = jnp.concatenate([b_group, jnp.zeros((gap,), jnp.float32), b_expert.reshape(-1),
                          jnp.zeros((pad,), jnp.float32)])[None, :]
    wr_hi = wr.astype(jnp.bfloat16)
    wr_lo = (wr - wr_hi.astype(jnp.float32)).astype(jnp.bfloat16)
    return jnp.concatenate([wr_hi, wr_lo], axis=1), br


def kernel(x, norm_mix, norm_ffn, ret_w_in, ret_w_out, pool_w, pool_scale, moe_w_group, moe_b_group, moe_w_expert, moe_b_expert, moe_w_gate, moe_w_up, moe_w_down, final_norm):
    batch, seq, _ = x.shape
    x2d = x.reshape(batch * seq, D_MODEL)
    bf = jnp.bfloat16
    weights = (moe_w_gate.astype(bf), moe_w_up.astype(bf), moe_w_down.astype(bf))
    fgain = final_norm[None, :]

    wr0, br0 = _router_params(moe_w_group[0], moe_b_group[0], moe_w_expert[0], moe_b_expert[0])
    x1, route0 = _retention_layer(x2d, batch, seq, norm_mix[0][None, :], ret_w_in[0].astype(bf),
                                  ret_w_out[0].astype(bf), norm_ffn[0][None, :], wr0, br0)
    x2 = _moe_layer(x1, route0, norm_ffn[0][None, :], fgain, weights, 0, final_norm=False)

    wr1, br1 = _router_params(moe_w_group[1], moe_b_group[1], moe_w_expert[1], moe_b_expert[1])
    x3, route1 = _pool_layer(x2, batch, seq, norm_mix[1][None, :], pool_w[0].astype(bf),
                             pool_scale[0][None, :], norm_ffn[1][None, :], wr1, br1)
    out = _moe_layer(x3, route1, norm_ffn[1][None, :], fgain, weights, 1, final_norm=True)
    return out.reshape(batch, seq, D_MODEL)
```

```python
import functools

import numpy as np
import jax
import jax.numpy as jnp
from jax import lax
from jax.experimental import pallas as pl
from jax.experimental.pallas import tpu as pltpu

D_MODEL = 1024
EPS = 1e-6
CHUNK = 64

RET_HEADS = 4
RET_DK = 256
RET_DV = 512
RET_QK_DIM = RET_HEADS * RET_DK
RET_V_DIM = RET_HEADS * RET_DV
ROPE_BASE = 10000.0

POOL_WINDOWS = (2, 4, 8, 16)
POOL_GROUP_DIM = 256
POOL_HIST = 16

MOE_GROUPS = 4
MOE_EPG = 8
MOE_EXPERTS = MOE_GROUPS * MOE_EPG
MOE_D_FF = 256
PAIRS_PER_GROUP = MOE_EPG * (MOE_EPG - 1) // 2
N_CLASSES = MOE_GROUPS * PAIRS_PER_GROUP

LANES = 128
SUBLANES = 8

RET_TILE = 256
RET_BLOCK = 256
RET_SEQS = 2
POOL_TILE = 1024
MOE_ROWS = 128
MOE_STEP_BLOCKS = 2
MOE_WEIGHTS = ("gate", "up", "down")
DISPATCH_TILE = 4096
MOE_OBUFS = 2
DMA_THREADS = 2
VMEM_LIMIT = 56 * 1024 * 1024


def _rms(x, gain):
    ms = jnp.mean(x * x, axis=-1, keepdims=True)
    return x * lax.rsqrt(ms + EPS) * gain


def _silu(x):
    return x * jax.nn.sigmoid(x)


def _dot(a, b):
    return jnp.dot(a, b, preferred_element_type=jnp.float32)


ROUTE_EXPERT_COL = 8
TILES_PER_ROW = D_MODEL // LANES


def _load_rows(ref, rows):
    return jnp.concatenate(
        [ref[pl.ds(s, rows, stride=TILES_PER_ROW), :] for s in range(TILES_PER_ROW)], axis=1)


def _store_rows(ref, val):
    rows = val.shape[0]
    for s in range(TILES_PER_ROW):
        ref[pl.ds(s, rows, stride=TILES_PER_ROW), :] = val[:, s * LANES:(s + 1) * LANES]


def _route(x1, gain2, wr2, br, tri, cnt_ref):
    rows = x1.shape[0]
    h2 = _rms(x1, gain2)
    h_hi = h2.astype(jnp.bfloat16)
    h_lo = (h2 - h_hi.astype(jnp.float32)).astype(jnp.bfloat16)
    parts = _dot(jnp.concatenate([h_hi, h_lo], axis=0), wr2)
    logits = (parts[:rows, :LANES] + parts[rows:, :LANES]) + parts[:rows, LANES:] + br
    lt = logits.T
    sub = lax.broadcasted_iota(jnp.int32, (SUBLANES, rows), 0)
    neg = jnp.float32(-jnp.inf)
    gl = jnp.where(sub < MOE_GROUPS, lt[0:SUBLANES], neg)
    gmax = jnp.max(gl, axis=0, keepdims=True)
    grp = jnp.min(jnp.where(gl == gmax, sub, SUBLANES), axis=0, keepdims=True)
    g_gate = 1.0 / jnp.sum(jnp.exp(gl - gmax), axis=0, keepdims=True)
    el = lt[ROUTE_EXPERT_COL:ROUTE_EXPERT_COL + MOE_EPG]
    for g in range(1, MOE_GROUPS):
        first = ROUTE_EXPERT_COL + g * MOE_EPG
        el = jnp.where(grp == g, lt[first:first + MOE_EPG], el)
    m1 = jnp.max(el, axis=0, keepdims=True)
    i1 = jnp.min(jnp.where(el == m1, sub, SUBLANES), axis=0, keepdims=True)
    el2 = jnp.where(sub == i1, neg, el)
    m2 = jnp.max(el2, axis=0, keepdims=True)
    i2 = jnp.min(jnp.where(el2 == m2, sub, SUBLANES), axis=0, keepdims=True)
    t = jnp.exp(m2 - m1)
    w1 = g_gate / (1.0 + t)
    w2 = g_gate * t / (1.0 + t)
    first_lo = i1 < i2
    a = jnp.where(first_lo, i1, i2)
    bb = jnp.where(first_lo, i2, i1)
    base = grp * MOE_EPG
    lo = (base + a).astype(jnp.float32)
    hi = (base + bb).astype(jnp.float32)
    w_lo = jnp.where(first_lo, w1, w2)
    w_hi = jnp.where(first_lo, w2, w1)
    cls = grp * PAIRS_PER_GROUP + ((a * (2 * MOE_EPG - 1 - a)) >> 1) + (bb - a - 1)
    onehot = (lax.broadcasted_iota(jnp.int32, (LANES, rows), 0) == cls).astype(jnp.float32)
    earlier = _dot(onehot.astype(jnp.bfloat16), tri) + cnt_ref[...]
    rank = jnp.sum(onehot * earlier, axis=0, keepdims=True)
    cnt_ref[...] += jnp.sum(onehot, axis=1, keepdims=True)
    out = jnp.where(sub == 0, lo, jnp.where(sub == 1, hi, jnp.where(sub == 2, w_lo,
                    jnp.where(sub == 3, w_hi, 0.0))))
    return jnp.where(sub == 4, rank, jnp.where(sub == 5, cls.astype(jnp.float32), out))


def _ret_kernel(x_ref, gain_ref, win_ref, wout_ref, cos_ref, sin_ref, dmat_ref,
                qdec_ref, kdec_ref, cdec_ref, gain2_ref, wr_ref, br_ref, tri_ref,
                x1_ref, route_ref, state_ref, gated_ref, cnt_ref):
    @pl.when(pl.program_id(1) == 0)
    def _():
        state_ref[...] = jnp.zeros_like(state_ref)

    @pl.when(jnp.logical_and(pl.program_id(0) == 0, pl.program_id(1) == 0))
    def _():
        cnt_ref[...] = jnp.zeros_like(cnt_ref)

    tb = x_ref.shape[1]
    x = jnp.concatenate([x_ref[sq] for sq in range(RET_SEQS)], axis=0)
    h = _rms(x, gain_ref[...]).astype(jnp.bfloat16)
    cos = cos_ref[...]
    sin = sin_ref[...]
    half = RET_DK // 2

    def rope(t):
        t1, t2 = t[:, :half], t[:, half:]
        return jnp.concatenate([t1 * cos - t2 * sin, t1 * sin + t2 * cos], axis=1)

    for hd in range(RET_HEADS):
        proj = []
        for sq in range(RET_SEQS):
            hs = h[sq * tb:(sq + 1) * tb]
            q = _dot(hs, win_ref[:, hd * RET_DK:(hd + 1) * RET_DK])
            k = _dot(hs, win_ref[:, RET_QK_DIM + hd * RET_DK:RET_QK_DIM + (hd + 1) * RET_DK])
            v0 = 2 * RET_QK_DIM + hd * RET_DV
            v = _dot(hs, win_ref[:, v0:v0 + RET_DV])
            g0 = 2 * RET_QK_DIM + RET_V_DIM + hd * RET_DV
            g = _dot(hs, win_ref[:, g0:g0 + RET_DV])
            proj.append((q, k, v, g))
        for sq in range(RET_SEQS):
            q, k, v, g = proj[sq]
            q = rope(q)
            k = rope(k) * (RET_DK ** -0.5)
            for blk in range(tb // RET_BLOCK):
                rs = slice(blk * RET_BLOCK, (blk + 1) * RET_BLOCK)
                qb = q[rs].astype(jnp.bfloat16)
                kb = k[rs].astype(jnp.bfloat16)
                vb = v[rs].astype(jnp.bfloat16)
                sc = lax.dot_general(qb, kb, (((1,), (1,)), ((), ())),
                                     preferred_element_type=jnp.float32)
                sc = sc * dmat_ref[hd]
                intra = _dot(sc.astype(jnp.bfloat16), vb)
                st = state_ref[sq, hd]
                cross = _dot(qb, st.astype(jnp.bfloat16)) * qdec_ref[hd]
                o = intra + cross
                kd = (k[rs] * kdec_ref[hd]).astype(jnp.bfloat16)
                upd = lax.dot_general(kd, vb, (((0,), (0,)), ((), ())),
                                      preferred_element_type=jnp.float32)
                state_ref[sq, hd] = st * cdec_ref[hd] + upd
                mu = jnp.mean(o, axis=-1, keepdims=True)
                oc = o - mu
                var = jnp.mean(oc * oc, axis=-1, keepdims=True)
                on = oc * lax.rsqrt(var + EPS)
                rows = slice(sq * tb + blk * RET_BLOCK, sq * tb + (blk + 1) * RET_BLOCK)
                gated_ref[rows, hd * RET_DV:(hd + 1) * RET_DV] = (_silu(g[rs]) * on).astype(jnp.bfloat16)

    x1 = x + _dot(gated_ref[...], wout_ref[...])
    for sq in range(RET_SEQS):
        _store_rows(x1_ref.at[sq], x1[sq * tb:(sq + 1) * tb])
    route_ref[...] = _route(x1, gain2_ref[...], wr_ref[...], br_ref[...], tri_ref[...], cnt_ref)


def _tri(n):
    i = jnp.arange(n)
    return (i[:, None] < i[None, :]).astype(jnp.bfloat16)


def _const_spec(shape):
    nd = len(shape)
    return pl.BlockSpec(shape, lambda *_: (0,) * nd, pipeline_mode=pl.Buffered(1))


def _retention_layer(x, gain, w_in, w_out, gain2, wr, br):
    batch, seq, _ = x.shape
    tb = RET_TILE
    n_s = seq // tb
    total = batch * seq
    rb = RET_BLOCK
    log_g = jnp.log(1.0 - jnp.exp2(-5.0 - jnp.arange(RET_HEADS, dtype=jnp.float32)))
    n = jnp.arange(rb, dtype=jnp.float32)
    diff = n[:, None] - n[None, :]
    cn = jnp.arange(rb)[:, None] // CHUNK
    cm = jnp.arange(rb)[None, :] // CHUNK
    expo = jnp.where(cn == cm, jnp.abs(diff), diff)
    dmat = jnp.where((cm <= cn)[None], jnp.exp(log_g[:, None, None] * expo[None]), 0.0)
    qdec = jnp.exp(log_g[:, None] * (n + 1.0)[None])[:, :, None]
    kdec = jnp.exp(log_g[:, None] * (rb - 1.0 - n)[None])[:, :, None]
    cdec = jnp.broadcast_to(jnp.exp(log_g * rb)[:, None, None], (RET_HEADS, 1, RET_DV))
    half = RET_DK // 2
    inv_freq = 1.0 / (ROPE_BASE ** (jnp.arange(half, dtype=jnp.float32) / half))
    ang = jnp.arange(seq, dtype=jnp.float32)[:, None] * inv_freq[None, :]
    cos, sin = jnp.cos(ang), jnp.sin(ang)

    tile = lambda b, s: (b, s, 0)
    return pl.pallas_call(
        _ret_kernel,
        grid=(batch // RET_SEQS, n_s),
        in_specs=[
            pl.BlockSpec((RET_SEQS, tb, D_MODEL), tile),
            _const_spec((1, D_MODEL)),
            _const_spec(w_in.shape),
            _const_spec(w_out.shape),
            pl.BlockSpec((tb, half), lambda b, s: (s, 0)),
            pl.BlockSpec((tb, half), lambda b, s: (s, 0)),
            _const_spec(dmat.shape),
            _const_spec(qdec.shape),
            _const_spec(kdec.shape),
            _const_spec(cdec.shape),
            _const_spec((1, D_MODEL)),
            _const_spec(wr.shape),
            _const_spec(br.shape),
            _const_spec((RET_SEQS * tb, RET_SEQS * tb)),
        ],
        out_specs=[
            pl.BlockSpec((RET_SEQS, tb * TILES_PER_ROW, LANES), tile),
            pl.BlockSpec((SUBLANES, RET_SEQS * tb), lambda b, s: (0, b * n_s + s)),
        ],
        out_shape=[
            jax.ShapeDtypeStruct((batch, seq * TILES_PER_ROW, LANES), jnp.float32),
            jax.ShapeDtypeStruct((SUBLANES, total), jnp.float32),
        ],
        scratch_shapes=[
            pltpu.VMEM((RET_SEQS, RET_HEADS, RET_DK, RET_DV), jnp.float32),
            pltpu.VMEM((RET_SEQS * tb, RET_V_DIM), jnp.bfloat16),
            pltpu.VMEM((LANES, 1), jnp.float32),
        ],
        compiler_params=pltpu.CompilerParams(
            dimension_semantics=("arbitrary", "arbitrary"),
            vmem_limit_bytes=VMEM_LIMIT),
        name="retention_layer",
    )(x, gain, w_in, w_out, cos, sin, dmat, qdec, kdec, cdec, gain2, wr, br, _tri(RET_SEQS * tb))


def _pool_kernel(x_ref, gain_ref, pw_ref, scale_ref, gain2_ref, wr_ref, br_ref, tri_ref,
                 x1_ref, route_ref, ext_ref, y_ref, cnt_ref):
    s = pl.program_id(1)
    tb = x_ref.shape[0] // TILES_PER_ROW

    @pl.when(s == 0)
    def _():
        ext_ref[0:POOL_HIST, :] = jnp.zeros((POOL_HIST, D_MODEL), jnp.float32)

    @pl.when(jnp.logical_and(pl.program_id(0) == 0, s == 0))
    def _():
        cnt_ref[...] = jnp.zeros_like(cnt_ref)

    x = _load_rows(x_ref, tb)
    ext_ref[POOL_HIST:, :] = _rms(x, gain_ref[...])
    pos = s * tb + lax.broadcasted_iota(jnp.int32, (POOL_HIST, 1), 0)
    for g, w in enumerate(POOL_WINDOWS):
        cs = slice(g * POOL_GROUP_DIM, (g + 1) * POOL_GROUP_DIM)
        e = ext_ref[:, cs]
        acc = e
        k = 1
        while k < w:
            acc = acc + pltpu.roll(acc, k, axis=0)
            k *= 2
        inv_cnt = 1.0 / jnp.minimum(pos + 1, w).astype(jnp.float32)
        head = slice(POOL_HIST, 2 * POOL_HIST)
        pooled = jnp.concatenate(
            [acc[head, :] * inv_cnt - e[head, :],
             acc[2 * POOL_HIST:, :] * (1.0 / w) - e[2 * POOL_HIST:, :]], axis=0)
        y_ref[:, cs] = _dot(pooled.astype(jnp.bfloat16), pw_ref[g])
    ext_ref[0:POOL_HIST, :] = ext_ref[tb:tb + POOL_HIST, :]
    x1 = x + y_ref[...] * scale_ref[...]
    _store_rows(x1_ref, x1)
    route_ref[...] = _route(x1, gain2_ref[...], wr_ref[...], br_ref[...], tri_ref[...], cnt_ref)


def _pool_layer(x2d, batch, seq, gain, pw, scale, gain2, wr, br):
    tb = POOL_TILE
    n_s = seq // tb
    total = batch * seq
    tile = lambda b, s: (b * n_s + s, 0)
    return pl.pallas_call(
        _pool_kernel,
        grid=(batch, n_s),
        in_specs=[
            pl.BlockSpec((tb * TILES_PER_ROW, LANES), tile),
            _const_spec((1, D_MODEL)),
            _const_spec(pw.shape),
            _const_spec((1, D_MODEL)),
            _const_spec((1, D_MODEL)),
            _const_spec(wr.shape),
            _const_spec(br.shape),
            _const_spec((tb, tb)),
        ],
        out_specs=[
            pl.BlockSpec((tb * TILES_PER_ROW, LANES), tile),
            pl.BlockSpec((SUBLANES, tb), lambda b, s: (0, b * n_s + s)),
        ],
        out_shape=[
            jax.ShapeDtypeStruct((total * TILES_PER_ROW, LANES), jnp.float32),
            jax.ShapeDtypeStruct((SUBLANES, total), jnp.float32),
        ],
        scratch_shapes=[
            pltpu.VMEM((POOL_HIST + tb, D_MODEL), jnp.float32),
            pltpu.VMEM((tb, D_MODEL), jnp.float32),
            pltpu.VMEM((LANES, 1), jnp.float32),
        ],
        compiler_params=pltpu.CompilerParams(
            dimension_semantics=("arbitrary", "arbitrary"),
            vmem_limit_bytes=VMEM_LIMIT),
        name="pool_layer",
    )(x2d, gain, pw, scale, gain2, wr, br, _tri(tb))


def _tile_of(row):
    return pl.ds(pl.multiple_of(row * TILES_PER_ROW, TILES_PER_ROW), TILES_PER_ROW)


def _dispatch_kernel(slot_ref, run_start_ref, run_len_ref, n_used_ref,
                     x_ref, out_hbm, zero_ref, sem, sem_pad):
    i = pl.program_id(0)
    tile = x_ref.shape[0] // TILES_PER_ROW

    @pl.when(i == 0)
    def _():
        zero_ref[...] = jnp.zeros_like(zero_ref)

        def pad_copy(c):
            rows = pl.multiple_of(run_len_ref[c] * TILES_PER_ROW, TILES_PER_ROW)
            first = pl.multiple_of(run_start_ref[c] * TILES_PER_ROW, TILES_PER_ROW)
            return pltpu.make_async_copy(zero_ref.at[pl.ds(0, rows)], out_hbm.at[pl.ds(first, rows)],
                                         sem_pad.at[0])

        def block_copy(blk):
            first = pl.multiple_of(blk * zero_ref.shape[0], zero_ref.shape[0])
            return pltpu.make_async_copy(zero_ref, out_hbm.at[pl.ds(first, zero_ref.shape[0])],
                                         sem_pad.at[0])

        n_blocks = out_hbm.shape[0] // zero_ref.shape[0]
        for c in range(N_CLASSES):
            @pl.when(run_len_ref[c] > 0)
            def _():
                pad_copy(c).start()
        lax.fori_loop(n_used_ref[0], n_blocks, lambda blk, c: (block_copy(blk).start(), c)[1], 0)
        for c in range(N_CLASSES):
            @pl.when(run_len_ref[c] > 0)
            def _():
                pad_copy(c).wait()
        lax.fori_loop(n_used_ref[0], n_blocks, lambda blk, c: (block_copy(blk).wait(), c)[1], 0)

    for r in range(tile):
        slot = slot_ref[i * tile + r]
        pltpu.make_async_copy(x_ref.at[_tile_of(r)], out_hbm.at[_tile_of(slot)],
                              sem.at[0]).start(priority=r % DMA_THREADS)
    pltpu.make_async_copy(x_ref, out_hbm.at[pl.ds(0, x_ref.shape[0])], sem.at[0]).wait()


def _dispatch(x_tiled, slot_of_tok, run_start, run_len, n_used, n_slots):
    total = slot_of_tok.shape[0]
    tile = DISPATCH_TILE
    grid_spec = pltpu.PrefetchScalarGridSpec(
        num_scalar_prefetch=4,
        grid=(total // tile,),
        in_specs=[pl.BlockSpec((tile * TILES_PER_ROW, LANES), lambda i, *_: (i, 0))],
        out_specs=pl.BlockSpec(memory_space=pl.ANY),
        scratch_shapes=[
            pltpu.VMEM((MOE_ROWS * TILES_PER_ROW, LANES), jnp.float32),
            pltpu.SemaphoreType.DMA((1,)),
            pltpu.SemaphoreType.DMA((1,)),
        ],
    )
    return pl.pallas_call(
        _dispatch_kernel,
        grid_spec=grid_spec,
        out_shape=jax.ShapeDtypeStruct((n_slots * TILES_PER_ROW, LANES), jnp.float32),
        compiler_params=pltpu.CompilerParams(
            dimension_semantics=("arbitrary",),
            vmem_limit_bytes=VMEM_LIMIT),
        name="moe_dispatch",
    )(slot_of_tok, run_start, run_len, n_used, x_tiled)


def _moe_kernel(dst_ref, nv_ref, ea_ref, eb_ref, xstep_ref,
                x_ref, wts_ref, gain_ref, fgain_ref, *rest, final_norm, total):
    n_w = 2 * len(MOE_WEIGHTS) * MOE_STEP_BLOCKS
    w_refs = rest[:n_w]
    out_hbm, obuf, sem_out = rest[n_w:]
    i = pl.program_id(0)
    blk0 = i * MOE_STEP_BLOCKS
    n_first = nv_ref[blk0]
    n_first_prev = nv_ref[jnp.maximum(blk0 - MOE_STEP_BLOCKS, 0)]
    osl = i % MOE_OBUFS
    step_rows = MOE_STEP_BLOCKS * MOE_ROWS

    def counts(step):
        first = jnp.maximum(step, 0) * MOE_STEP_BLOCKS
        return [nv_ref[first + j] for j in range(MOE_STEP_BLOCKS)]

    def out_copy(sl, r, tok):
        if final_norm:
            return pltpu.make_async_copy(obuf.at[sl, pl.ds(r, 1)], out_hbm.at[pl.ds(tok, 1)],
                                         sem_out.at[sl])
        return pltpu.make_async_copy(obuf.at[sl, _tile_of(r)], out_hbm.at[_tile_of(tok)],
                                     sem_out.at[sl])

    def start_scatter(step, sl, rows):
        def unrolled(j):
            for r in range(j * MOE_ROWS, (j + 1) * MOE_ROWS):
                out_copy(sl, r, dst_ref[step * step_rows + r]).start(priority=r % DMA_THREADS)

        for j in range(MOE_STEP_BLOCKS):
            if not final_norm:
                unrolled(j)
                continue
            pl.when(rows[j] == MOE_ROWS)(functools.partial(unrolled, j))

            @pl.when(rows[j] < MOE_ROWS)
            def _():
                def body(r, c):
                    out_copy(sl, j * MOE_ROWS + r, dst_ref[step * step_rows + j * MOE_ROWS + r]).start()
                    return c
                lax.fori_loop(0, rows[j], body, 0)

    def wait_scatter(sl, rows):
        if not final_norm:
            pltpu.make_async_copy(obuf.at[sl], out_hbm.at[pl.ds(0, obuf.shape[1])],
                                  sem_out.at[sl]).wait()
            return
        for j in range(MOE_STEP_BLOCKS):
            tiled = pl.multiple_of((rows[j] // SUBLANES) * SUBLANES, SUBLANES)

            @pl.when(tiled > 0)
            def _():
                pltpu.make_async_copy(obuf.at[sl, pl.ds(0, tiled)], out_hbm.at[pl.ds(0, tiled)],
                                      sem_out.at[sl]).wait()

            def body(r, c):
                out_copy(sl, 0, 0).wait()
                return c
            lax.fori_loop(tiled, rows[j], body, 0)

    if not final_norm:
        @pl.when(i == 0)
        def _():
            obuf[...] = jnp.zeros(obuf.shape, jnp.float32)
            for r in range(step_rows):
                out_copy(0, r, total + r).start()

    @pl.when(n_first > 0)
    def _():
        tile_rows = MOE_ROWS * TILES_PER_ROW
        blocks = range(MOE_STEP_BLOCKS)
        if not final_norm:
            start_scatter(i, 1 - osl, None)
        xs = [_load_rows(x_ref.at[pl.ds(j * tile_rows, tile_rows)], MOE_ROWS) for j in blocks]
        hs = [_rms(x, gain_ref[...]).astype(jnp.bfloat16) for x in xs]
        wts = [wts_ref[j].T for j in blocks]
        def weight(j, e, kind):
            return w_refs[(2 * j + e) * len(MOE_WEIGHTS) + MOE_WEIGHTS.index(kind)][0, 0]

        experts = [(j, e) for j in blocks for e in range(2)]
        gates = {je: _dot(hs[je[0]], weight(*je, "gate")) for je in experts}
        ups = {je: _dot(hs[je[0]], weight(*je, "up")) for je in experts}
        hid = {je: (_silu(gates[je]) * ups[je]).astype(jnp.bfloat16) for je in experts}
        outs = [xs[j] + wts[j][:, 0:1] * _dot(hid[j, 0], weight(j, 0, "down"))
                + wts[j][:, 1:2] * _dot(hid[j, 1], weight(j, 1, "down")) for j in blocks]
        if final_norm:
            pl.when(i >= MOE_OBUFS)(lambda: wait_scatter(osl, counts(i - MOE_OBUFS)))
            for j in blocks:
                obuf[osl, j * MOE_ROWS:(j + 1) * MOE_ROWS] = _rms(outs[j], fgain_ref[...])
        else:
            wait_scatter(osl, None)
            for j in blocks:
                _store_rows(obuf.at[osl, pl.ds(j * tile_rows, tile_rows)], outs[j])
        if final_norm:
            start_scatter(i, osl, counts(i))

    @pl.when(jnp.logical_and(n_first == 0, n_first_prev > 0))
    def _():
        if final_norm:
            wait_scatter(1 - osl, counts(i - 1))
            pl.when(i >= MOE_OBUFS)(lambda: wait_scatter(osl, counts(i - MOE_OBUFS)))
        else:
            start_scatter(i, 1 - osl, None)
            for sl in range(MOE_OBUFS):
                wait_scatter(sl, None)


def _class_tables():
    ea = np.zeros((N_CLASSES,), np.int32)
    eb = np.zeros((N_CLASSES,), np.int32)
    for g in range(MOE_GROUPS):
        c = g * PAIRS_PER_GROUP
        for a in range(MOE_EPG):
            for b in range(a + 1, MOE_EPG):
                ea[c], eb[c] = g * MOE_EPG + a, g * MOE_EPG + b
                c += 1
    return ea, eb


def _moe_layer(x_tiled, route, gain, fgain, weights, layer, final_norm, order=None):
    total = route.shape[1]
    order_of_tok, tok_of_order = order or ((lambda t: t), (lambda p: p))
    nb = (total // MOE_ROWS + N_CLASSES) // MOE_STEP_BLOCKS * MOE_STEP_BLOCKS + MOE_STEP_BLOCKS
    n_steps = nb // MOE_STEP_BLOCKS
    step_rows = MOE_STEP_BLOCKS * MOE_ROWS
    n_slots = nb * MOE_ROWS
    w_lo, w_hi = route[2], route[3]
    rank = route[4].astype(jnp.int32)
    cls = route[5].astype(jnp.int32)
    cls_ids = jnp.arange(N_CLASSES, dtype=jnp.int32)
    onehot_tok = (cls[:, None] == cls_ids[None, :]).astype(jnp.int32)
    counts = jnp.sum(onehot_tok, axis=0)
    nblk = (counts + MOE_ROWS - 1) // MOE_ROWS
    npad = nblk * MOE_ROWS - counts
    blk_end = jnp.cumsum(nblk)
    cls_start = (blk_end - nblk) * MOE_ROWS
    slot_of_tok = jnp.sum(onehot_tok * cls_start[None, :], axis=1) + rank
    x_sorted = _dispatch(x_tiled, slot_of_tok, cls_start + counts, npad, blk_end[-1:], n_slots)

    max_pad = MOE_ROWS - 1
    cand_cls = jnp.repeat(cls_ids, max_pad)
    cand_on = jnp.tile(jnp.arange(max_pad, dtype=jnp.int32), N_CLASSES) < jnp.repeat(npad, max_pad)
    n_fill = n_slots - total - N_CLASSES * max_pad
    tok_bits = 16
    assert total <= 1 << tok_bits
    keys = jnp.concatenate([
        (2 * cls << tok_bits) + order_of_tok(jnp.arange(total, dtype=jnp.int32)),
        jnp.where(cand_on, 2 * cand_cls + 1, 2 * N_CLASSES + 1) << tok_bits,
        jnp.full((n_fill,), (2 * N_CLASSES + 1) << tok_bits, jnp.int32)])
    zeros = jnp.zeros((n_slots - total,), jnp.float32)
    key_s, wlo_s, whi_s = lax.sort(
        (keys, jnp.concatenate([w_lo, zeros]), jnp.concatenate([w_hi, zeros])),
        num_keys=1, is_stable=False)
    is_pad = ((key_s >> tok_bits) & 1) == 1
    tok_s = tok_of_order(key_s & ((1 << tok_bits) - 1))
    slot = jnp.arange(n_slots, dtype=jnp.int32)
    dump = total + ((slot // step_rows) % MOE_OBUFS) * step_rows + slot % step_rows
    dst = jnp.where(is_pad, dump, tok_s)
    if not final_norm:
        dst = jnp.concatenate([total + step_rows + jnp.arange(step_rows, dtype=jnp.int32),
                               dst[:-step_rows]])
    nvalid = jnp.sum((~is_pad).reshape(nb, MOE_ROWS).astype(jnp.int32), axis=1)
    wts = jnp.concatenate(
        [wlo_s.reshape(nb, 1, MOE_ROWS), whi_s.reshape(nb, 1, MOE_ROWS),
         jnp.zeros((nb, SUBLANES - 2, MOE_ROWS), jnp.float32)], axis=1)
    bidx = jnp.arange(nb, dtype=jnp.int32)
    xblk = jnp.minimum(bidx, blk_end[-1] - 1)
    c_blk = jnp.sum((blk_end[None, :] <= xblk[:, None]).astype(jnp.int32), axis=1)
    xstep = jnp.minimum(jnp.arange(n_steps, dtype=jnp.int32), (blk_end[-1] - 1) // MOE_STEP_BLOCKS)
    ea_tab, eb_tab = _class_tables()
    onehot_blk = (c_blk[:, None] == cls_ids[None, :]).astype(jnp.int32)
    ea = jnp.sum(onehot_blk * jnp.asarray(ea_tab)[None, :], axis=1)
    eb = jnp.sum(onehot_blk * jnp.asarray(eb_tab)[None, :], axis=1)

    tiled_block = (step_rows * TILES_PER_ROW, LANES)
    if final_norm:
        out_block = (step_rows, D_MODEL)
        out_shape = (total, D_MODEL)
    else:
        out_block = tiled_block
        out_shape = ((total + MOE_OBUFS * step_rows) * TILES_PER_ROW, LANES)

    def weight_specs(j):
        def spec(shape, tab):
            return pl.BlockSpec(shape, lambda i, dst, nv, ea, eb, xstep:
                                (layer, (ea, eb)[tab][i * MOE_STEP_BLOCKS + j], 0, 0))
        shapes = {"gate": (1, 1, D_MODEL, MOE_D_FF), "up": (1, 1, D_MODEL, MOE_D_FF),
                  "down": (1, 1, MOE_D_FF, D_MODEL)}
        return [spec(shapes[kind], tab) for tab in range(2) for kind in MOE_WEIGHTS]

    grid_spec = pltpu.PrefetchScalarGridSpec(
        num_scalar_prefetch=5,
        grid=(n_steps,),
        in_specs=[
            pl.BlockSpec(tiled_block, lambda i, dst, nv, ea, eb, xstep: (xstep[i], 0)),
            pl.BlockSpec((MOE_STEP_BLOCKS, SUBLANES, MOE_ROWS), lambda i, *_: (i, 0, 0)),
            pl.BlockSpec((1, D_MODEL), lambda i, *_: (0, 0)),
            pl.BlockSpec((1, D_MODEL), lambda i, *_: (0, 0)),
        ] + [s for j in range(MOE_STEP_BLOCKS) for s in weight_specs(j)],
        out_specs=pl.BlockSpec(memory_space=pl.ANY),
        scratch_shapes=[
            pltpu.VMEM((MOE_OBUFS,) + out_block, jnp.float32),
            pltpu.SemaphoreType.DMA((MOE_OBUFS,)),
        ],
    )
    return pl.pallas_call(
        functools.partial(_moe_kernel, final_norm=final_norm, total=total),
        grid_spec=grid_spec,
        out_shape=jax.ShapeDtypeStruct(out_shape, jnp.float32),
        compiler_params=pltpu.CompilerParams(
            dimension_semantics=("arbitrary",),
            vmem_limit_bytes=VMEM_LIMIT),
        name="moe_layer",
    )(dst, nvalid, ea, eb, xstep, x_sorted, wts, gain, fgain, *(list(weights) * (2 * MOE_STEP_BLOCKS)))


def _router_params(w_group, b_group, w_expert, b_expert):
    gap = ROUTE_EXPERT_COL - MOE_GROUPS
    pad = LANES - ROUTE_EXPERT_COL - MOE_EXPERTS
    wr = jnp.concatenate([w_group, jnp.zeros((D_MODEL, gap), jnp.float32), w_expert,
                          jnp.zeros((D_MODEL, pad), jnp.float32)], axis=1)
    br = jnp.concatenate([b_group, jnp.zeros((gap,), jnp.float32), b_expert.reshape(-1),
                          jnp.zeros((pad,), jnp.float32)])[None, :]
    wr_hi = wr.astype(jnp.bfloat16)
    wr_lo = (wr - wr_hi.astype(jnp.float32)).astype(jnp.bfloat16)
    return jnp.concatenate([wr_hi, wr_lo], axis=1), br


def kernel(x, norm_mix, norm_ffn, ret_w_in, ret_w_out, pool_w, pool_scale, moe_w_group, moe_b_group, moe_w_expert, moe_b_expert, moe_w_gate, moe_w_up, moe_w_down, final_norm):
    batch, seq, _ = x.shape
    bf = jnp.bfloat16
    weights = (moe_w_gate.astype(bf), moe_w_up.astype(bf), moe_w_down.astype(bf))
    fgain = final_norm[None, :]

    wr0, br0 = _router_params(moe_w_group[0], moe_b_group[0], moe_w_expert[0], moe_b_expert[0])
    x1, route0 = _retention_layer(x, norm_mix[0][None, :], ret_w_in[0].astype(bf),
                                  ret_w_out[0].astype(bf), norm_ffn[0][None, :], wr0, br0)
    n_s = seq // RET_TILE
    route0 = route0.reshape(SUBLANES, batch // RET_SEQS, n_s, RET_SEQS, RET_TILE)
    route0 = route0.transpose(0, 1, 3, 2, 4).reshape(SUBLANES, batch * seq)

    def split(i, *sizes):
        digits = []
        for size in reversed(sizes):
            digits.append(i % size)
            i = i // size
        return [i] + digits[::-1]

    def order_of_tok(t):
        grp, sq, tile, row = split(t, RET_SEQS, n_s, RET_TILE)
        return ((grp * n_s + tile) * RET_SEQS + sq) * RET_TILE + row

    def tok_of_order(p):
        grp, tile, sq, row = split(p, n_s, RET_SEQS, RET_TILE)
        return ((grp * RET_SEQS + sq) * n_s + tile) * RET_TILE + row

    x1 = x1.reshape(batch * seq * TILES_PER_ROW, LANES)
    x2 = _moe_layer(x1, route0, norm_ffn[0][None, :], fgain, weights, 0, final_norm=False,
                    order=(order_of_tok, tok_of_order))

    wr1, br1 = _router_params(moe_w_group[1], moe_b_group[1], moe_w_expert[1], moe_b_expert[1])
    x3, route1 = _pool_layer(x2, batch, seq, norm_mix[1][None, :], pool_w[0].astype(bf),
                             pool_scale[0][None, :], norm_ffn[1][None, :], wr1, br1)
    out = _moe_layer(x3, route1, norm_ffn[1][None, :], fgain, weights, 1, final_norm=True)
    return out.reshape(batch, seq, D_MODEL)
```

```python
import functools

import numpy as np
import jax
import jax.numpy as jnp
from jax import lax
from jax.experimental import pallas as pl
from jax.experimental.pallas import tpu as pltpu

D_MODEL = 1024
EPS = 1e-6
CHUNK = 64

RET_HEADS = 4
RET_DK = 256
RET_DV = 512
RET_QK_DIM = RET_HEADS * RET_DK
RET_V_DIM = RET_HEADS * RET_DV
ROPE_BASE = 10000.0

POOL_WINDOWS = (2, 4, 8, 16)
POOL_GROUP_DIM = 256
POOL_HIST = 16

MOE_GROUPS = 4
MOE_EPG = 8
MOE_EXPERTS = MOE_GROUPS * MOE_EPG
MOE_D_FF = 256
PAIRS_PER_GROUP = MOE_EPG * (MOE_EPG - 1) // 2
N_CLASSES = MOE_GROUPS * PAIRS_PER_GROUP

LANES = 128
SUBLANES = 8

RET_TILE = 256
RET_BLOCK = 256
RET_SEQS = 2
POOL_TILE = 1024
MOE_ROWS = 128
MOE_STEP_BLOCKS = 2
MOE_WEIGHTS = ("gate", "up", "down")
DISPATCH_TILE = 4096
MOE_OBUFS = 2
DMA_THREADS = 2
VMEM_LIMIT = 56 * 1024 * 1024


def _rms(x, gain):
    ms = jnp.mean(x * x, axis=-1, keepdims=True)
    return x * lax.rsqrt(ms + EPS) * gain


def _silu(x):
    return x * jax.nn.sigmoid(x)


def _dot(a, b):
    return jnp.dot(a, b, preferred_element_type=jnp.float32)


ROUTE_EXPERT_COL = 8
TILES_PER_ROW = D_MODEL // LANES


def _load_rows(ref, rows):
    return jnp.concatenate(
        [ref[pl.ds(s, rows, stride=TILES_PER_ROW), :] for s in range(TILES_PER_ROW)], axis=1)


def _store_rows(ref, val):
    rows = val.shape[0]
    for s in range(TILES_PER_ROW):
        ref[pl.ds(s, rows, stride=TILES_PER_ROW), :] = val[:, s * LANES:(s + 1) * LANES]


def _route(x1, gain2, wr2, br, tri, cnt_ref):
    rows = x1.shape[0]
    h2 = _rms(x1, gain2)
    h_hi = h2.astype(jnp.bfloat16)
    h_lo = (h2 - h_hi.astype(jnp.float32)).astype(jnp.bfloat16)
    parts = _dot(jnp.concatenate([h_hi, h_lo], axis=0), wr2)
    logits = (parts[:rows, :LANES] + parts[rows:, :LANES]) + parts[:rows, LANES:] + br
    lt = logits.T
    sub = lax.broadcasted_iota(jnp.int32, (SUBLANES, rows), 0)
    neg = jnp.float32(-jnp.inf)
    gl = jnp.where(sub < MOE_GROUPS, lt[0:SUBLANES], neg)
    gmax = jnp.max(gl, axis=0, keepdims=True)
    grp = jnp.min(jnp.where(gl == gmax, sub, SUBLANES), axis=0, keepdims=True)
    g_gate = 1.0 / jnp.sum(jnp.exp(gl - gmax), axis=0, keepdims=True)
    el = lt[ROUTE_EXPERT_COL:ROUTE_EXPERT_COL + MOE_EPG]
    for g in range(1, MOE_GROUPS):
        first = ROUTE_EXPERT_COL + g * MOE_EPG
        el = jnp.where(grp == g, lt[first:first + MOE_EPG], el)
    m1 = jnp.max(el, axis=0, keepdims=True)
    i1 = jnp.min(jnp.where(el == m1, sub, SUBLANES), axis=0, keepdims=True)
    el2 = jnp.where(sub == i1, neg, el)
    m2 = jnp.max(el2, axis=0, keepdims=True)
    i2 = jnp.min(jnp.where(el2 == m2, sub, SUBLANES), axis=0, keepdims=True)
    t = jnp.exp(m2 - m1)
    w1 = g_gate / (1.0 + t)
    w2 = g_gate * t / (1.0 + t)
    first_lo = i1 < i2
    a = jnp.where(first_lo, i1, i2)
    bb = jnp.where(first_lo, i2, i1)
    base = grp * MOE_EPG
    lo = (base + a).astype(jnp.float32)
    hi = (base + bb).astype(jnp.float32)
    w_lo = jnp.where(first_lo, w1, w2)
    w_hi = jnp.where(first_lo, w2, w1)
    cls = grp * PAIRS_PER_GROUP + ((a * (2 * MOE_EPG - 1 - a)) >> 1) + (bb - a - 1)
    onehot = (lax.broadcasted_iota(jnp.int32, (LANES, rows), 0) == cls).astype(jnp.float32)
    earlier = _dot(onehot.astype(jnp.bfloat16), tri) + cnt_ref[...]
    rank = jnp.sum(onehot * earlier, axis=0, keepdims=True)
    cnt_ref[...] += jnp.sum(onehot, axis=1, keepdims=True)
    out = jnp.where(sub == 0, lo, jnp.where(sub == 1, hi, jnp.where(sub == 2, w_lo,
                    jnp.where(sub == 3, w_hi, 0.0))))
    return jnp.where(sub == 4, rank, jnp.where(sub == 5, cls.astype(jnp.float32), out))


def _ret_kernel(x_ref, gain_ref, win_ref, wout_ref, cos_ref, sin_ref, dmat_ref,
                qdec_ref, kdec_ref, cdec_ref, gain2_ref, wr_ref, br_ref, tri_ref,
                x1_ref, route_ref, state_ref, gated_ref, cnt_ref):
    @pl.when(pl.program_id(1) == 0)
    def _():
        state_ref[...] = jnp.zeros_like(state_ref)

    @pl.when(jnp.logical_and(pl.program_id(0) == 0, pl.program_id(1) == 0))
    def _():
        cnt_ref[...] = jnp.zeros_like(cnt_ref)

    tb = x_ref.shape[1]
    x = jnp.concatenate([x_ref[sq] for sq in range(RET_SEQS)], axis=0)
    h = _rms(x, gain_ref[...]).astype(jnp.bfloat16)
    cos = cos_ref[...]
    sin = sin_ref[...]
    half = RET_DK // 2

    def rope(t):
        t1, t2 = t[:, :half], t[:, half:]
        return jnp.concatenate([t1 * cos - t2 * sin, t1 * sin + t2 * cos], axis=1)

    for hd in range(RET_HEADS):
        proj = []
        for sq in range(RET_SEQS):
            hs = h[sq * tb:(sq + 1) * tb]
            q = _dot(hs, win_ref[:, hd * RET_DK:(hd + 1) * RET_DK])
            k = _dot(hs, win_ref[:, RET_QK_DIM + hd * RET_DK:RET_QK_DIM + (hd + 1) * RET_DK])
            v0 = 2 * RET_QK_DIM + hd * RET_DV
            v = _dot(hs, win_ref[:, v0:v0 + RET_DV])
            g0 = 2 * RET_QK_DIM + RET_V_DIM + hd * RET_DV
            g = _dot(hs, win_ref[:, g0:g0 + RET_DV])
            proj.append((q, k, v, g))
        for sq in range(RET_SEQS):
            q, k, v, g = proj[sq]
            q = rope(q)
            k = rope(k) * (RET_DK ** -0.5)
            for blk in range(tb // RET_BLOCK):
                rs = slice(blk * RET_BLOCK, (blk + 1) * RET_BLOCK)
                qb = q[rs].astype(jnp.bfloat16)
                kb = k[rs].astype(jnp.bfloat16)
                vb = v[rs].astype(jnp.bfloat16)
                sc = lax.dot_general(qb, kb, (((1,), (1,)), ((), ())),
                                     preferred_element_type=jnp.float32)
                sc = sc * dmat_ref[hd]
                intra = _dot(sc.astype(jnp.bfloat16), vb)
                st = state_ref[sq, hd]
                cross = _dot(qb, st.astype(jnp.bfloat16)) * qdec_ref[hd]
                o = intra + cross
                kd = (k[rs] * kdec_ref[hd]).astype(jnp.bfloat16)
                upd = lax.dot_general(kd, vb, (((0,), (0,)), ((), ())),
                                      preferred_element_type=jnp.float32)
                state_ref[sq, hd] = st * cdec_ref[hd] + upd
                mu = jnp.mean(o, axis=-1, keepdims=True)
                oc = o - mu
                var = jnp.mean(oc * oc, axis=-1, keepdims=True)
                on = oc * lax.rsqrt(var + EPS)
                rows = slice(sq * tb + blk * RET_BLOCK, sq * tb + (blk + 1) * RET_BLOCK)
                gated_ref[rows, hd * RET_DV:(hd + 1) * RET_DV] = (_silu(g[rs]) * on).astype(jnp.bfloat16)

    x1 = x + _dot(gated_ref[...], wout_ref[...])
    for sq in range(RET_SEQS):
        _store_rows(x1_ref.at[sq], x1[sq * tb:(sq + 1) * tb])
    route_ref[...] = _route(x1, gain2_ref[...], wr_ref[...], br_ref[...], tri_ref[...], cnt_ref)


def _tri(n):
    i = jnp.arange(n)
    return (i[:, None] < i[None, :]).astype(jnp.bfloat16)


def _const_spec(shape):
    nd = len(shape)
    return pl.BlockSpec(shape, lambda *_: (0,) * nd, pipeline_mode=pl.Buffered(1))


def _retention_layer(x, gain, w_in, w_out, gain2, wr, br):
    batch, seq, _ = x.shape
    tb = RET_TILE
    n_s = seq // tb
    total = batch * seq
    rb = RET_BLOCK
    log_g = jnp.log(1.0 - jnp.exp2(-5.0 - jnp.arange(RET_HEADS, dtype=jnp.float32)))
    n = jnp.arange(rb, dtype=jnp.float32)
    diff = n[:, None] - n[None, :]
    cn = jnp.arange(rb)[:, None] // CHUNK
    cm = jnp.arange(rb)[None, :] // CHUNK
    expo = jnp.where(cn == cm, jnp.abs(diff), diff)
    dmat = jnp.where((cm <= cn)[None], jnp.exp(log_g[:, None, None] * expo[None]), 0.0)
    qdec = jnp.exp(log_g[:, None] * (n + 1.0)[None])[:, :, None]
    kdec = jnp.exp(log_g[:, None] * (rb - 1.0 - n)[None])[:, :, None]
    cdec = jnp.broadcast_to(jnp.exp(log_g * rb)[:, None, None], (RET_HEADS, 1, RET_DV))
    half = RET_DK // 2
    inv_freq = 1.0 / (ROPE_BASE ** (jnp.arange(half, dtype=jnp.float32) / half))
    ang = jnp.arange(seq, dtype=jnp.float32)[:, None] * inv_freq[None, :]
    cos, sin = jnp.cos(ang), jnp.sin(ang)

    tile = lambda b, s: (b, s, 0)
    return pl.pallas_call(
        _ret_kernel,
        grid=(batch // RET_SEQS, n_s),
        in_specs=[
            pl.BlockSpec((RET_SEQS, tb, D_MODEL), tile),
            _const_spec((1, D_MODEL)),
            _const_spec(w_in.shape),
            _const_spec(w_out.shape),
            pl.BlockSpec((tb, half), lambda b, s: (s, 0)),
            pl.BlockSpec((tb, half), lambda b, s: (s, 0)),
            _const_spec(dmat.shape),
            _const_spec(qdec.shape),
            _const_spec(kdec.shape),
            _const_spec(cdec.shape),
            _const_spec((1, D_MODEL)),
            _const_spec(wr.shape),
            _const_spec(br.shape),
            _const_spec((RET_SEQS * tb, RET_SEQS * tb)),
        ],
        out_specs=[
            pl.BlockSpec((RET_SEQS, tb * TILES_PER_ROW, LANES), tile),
            pl.BlockSpec((SUBLANES, RET_SEQS * tb), lambda b, s: (0, b * n_s + s)),
        ],
        out_shape=[
            jax.ShapeDtypeStruct((batch, seq * TILES_PER_ROW, LANES), jnp.float32),
            jax.ShapeDtypeStruct((SUBLANES, total), jnp.float32),
        ],
        scratch_shapes=[
            pltpu.VMEM((RET_SEQS, RET_HEADS, RET_DK, RET_DV), jnp.float32),
            pltpu.VMEM((RET_SEQS * tb, RET_V_DIM), jnp.bfloat16),
            pltpu.VMEM((LANES, 1), jnp.float32),
        ],
        compiler_params=pltpu.CompilerParams(
            dimension_semantics=("arbitrary", "arbitrary"),
            vmem_limit_bytes=VMEM_LIMIT),
        name="retention_layer",
    )(x, gain, w_in, w_out, cos, sin, dmat, qdec, kdec, cdec, gain2, wr, br, _tri(RET_SEQS * tb))


def _pool_kernel(x_ref, gain_ref, pw_ref, scale_ref, gain2_ref, wr_ref, br_ref, tri_ref,
                 x1_ref, route_ref, ext_ref, y_ref, cnt_ref):
    s = pl.program_id(1)
    tb = x_ref.shape[0] // TILES_PER_ROW

    @pl.when(s == 0)
    def _():
        ext_ref[0:POOL_HIST, :] = jnp.zeros((POOL_HIST, D_MODEL), jnp.float32)

    @pl.when(jnp.logical_and(pl.program_id(0) == 0, s == 0))
    def _():
        cnt_ref[...] = jnp.zeros_like(cnt_ref)

    x = _load_rows(x_ref, tb)
    ext_ref[POOL_HIST:, :] = _rms(x, gain_ref[...])
    pos = s * tb + lax.broadcasted_iota(jnp.int32, (POOL_HIST, 1), 0)
    for g, w in enumerate(POOL_WINDOWS):
        cs = slice(g * POOL_GROUP_DIM, (g + 1) * POOL_GROUP_DIM)
        e = ext_ref[:, cs]
        acc = e
        k = 1
        while k < w:
            acc = acc + pltpu.roll(acc, k, axis=0)
            k *= 2
        inv_cnt = 1.0 / jnp.minimum(pos + 1, w).astype(jnp.float32)
        head = slice(POOL_HIST, 2 * POOL_HIST)
        pooled = jnp.concatenate(
            [acc[head, :] * inv_cnt - e[head, :],
             acc[2 * POOL_HIST:, :] * (1.0 / w) - e[2 * POOL_HIST:, :]], axis=0)
        y_ref[:, cs] = _dot(pooled.astype(jnp.bfloat16), pw_ref[g])
    ext_ref[0:POOL_HIST, :] = ext_ref[tb:tb + POOL_HIST, :]
    x1 = x + y_ref[...] * scale_ref[...]
    _store_rows(x1_ref, x1)
    route_ref[...] = _route(x1, gain2_ref[...], wr_ref[...], br_ref[...], tri_ref[...], cnt_ref)


def _pool_layer(x2d, batch, seq, gain, pw, scale, gain2, wr, br):
    tb = POOL_TILE
    n_s = seq // tb
    total = batch * seq
    tile = lambda b, s: (b * n_s + s, 0)
    return pl.pallas_call(
        _pool_kernel,
        grid=(batch, n_s),
        in_specs=[
            pl.BlockSpec((tb * TILES_PER_ROW, LANES), tile),
            _const_spec((1, D_MODEL)),
            _const_spec(pw.shape),
            _const_spec((1, D_MODEL)),
            _const_spec((1, D_MODEL)),
            _const_spec(wr.shape),
            _const_spec(br.shape),
            _const_spec((tb, tb)),
        ],
        out_specs=[
            pl.BlockSpec((tb * TILES_PER_ROW, LANES), tile),
            pl.BlockSpec((SUBLANES, tb), lambda b, s: (0, b * n_s + s)),
        ],
        out_shape=[
            jax.ShapeDtypeStruct((total * TILES_PER_ROW, LANES), jnp.float32),
            jax.ShapeDtypeStruct((SUBLANES, total), jnp.float32),
        ],
        scratch_shapes=[
            pltpu.VMEM((POOL_HIST + tb, D_MODEL), jnp.float32),
            pltpu.VMEM((tb, D_MODEL), jnp.float32),
            pltpu.VMEM((LANES, 1), jnp.float32),
        ],
        compiler_params=pltpu.CompilerParams(
            dimension_semantics=("arbitrary", "arbitrary"),
            vmem_limit_bytes=VMEM_LIMIT),
        name="pool_layer",
    )(x2d, gain, pw, scale, gain2, wr, br, _tri(tb))


def _tile_of(row):
    return pl.ds(pl.multiple_of(row * TILES_PER_ROW, TILES_PER_ROW), TILES_PER_ROW)


def _dispatch_kernel(slot_ref, run_start_ref, run_len_ref, n_used_ref,
                     x_ref, out_hbm, zero_ref, sem, sem_pad):
    i = pl.program_id(0)
    tile = x_ref.shape[0] // TILES_PER_ROW

    @pl.when(i == 0)
    def _():
        zero_ref[...] = jnp.zeros_like(zero_ref)

        def pad_copy(c):
            rows = pl.multiple_of(run_len_ref[c] * TILES_PER_ROW, TILES_PER_ROW)
            first = pl.multiple_of(run_start_ref[c] * TILES_PER_ROW, TILES_PER_ROW)
            return pltpu.make_async_copy(zero_ref.at[pl.ds(0, rows)], out_hbm.at[pl.ds(first, rows)],
                                         sem_pad.at[0])

        def block_copy(blk):
            first = pl.multiple_of(blk * zero_ref.shape[0], zero_ref.shape[0])
            return pltpu.make_async_copy(zero_ref, out_hbm.at[pl.ds(first, zero_ref.shape[0])],
                                         sem_pad.at[0])

        n_blocks = out_hbm.shape[0] // zero_ref.shape[0]
        for c in range(N_CLASSES):
            @pl.when(run_len_ref[c] > 0)
            def _():
                pad_copy(c).start()
        lax.fori_loop(n_used_ref[0], n_blocks, lambda blk, c: (block_copy(blk).start(), c)[1], 0)
        for c in range(N_CLASSES):
            @pl.when(run_len_ref[c] > 0)
            def _():
                pad_copy(c).wait()
        lax.fori_loop(n_used_ref[0], n_blocks, lambda blk, c: (block_copy(blk).wait(), c)[1], 0)

    for r in range(tile):
        slot = slot_ref[i * tile + r]
        pltpu.make_async_copy(x_ref.at[_tile_of(r)], out_hbm.at[_tile_of(slot)],
                              sem.at[0]).start(priority=r % DMA_THREADS)
    pltpu.make_async_copy(x_ref, out_hbm.at[pl.ds(0, x_ref.shape[0])], sem.at[0]).wait()


def _dispatch(x_tiled, slot_of_tok, run_start, run_len, n_used, n_slots):
    total = slot_of_tok.shape[0]
    tile = DISPATCH_TILE
    grid_spec = pltpu.PrefetchScalarGridSpec(
        num_scalar_prefetch=4,
        grid=(total // tile,),
        in_specs=[pl.BlockSpec((tile * TILES_PER_ROW, LANES), lambda i, *_: (i, 0))],
        out_specs=pl.BlockSpec(memory_space=pl.ANY),
        scratch_shapes=[
            pltpu.VMEM((MOE_ROWS * TILES_PER_ROW, LANES), jnp.float32),
            pltpu.SemaphoreType.DMA((1,)),
            pltpu.SemaphoreType.DMA((1,)),
        ],
    )
    return pl.pallas_call(
        _dispatch_kernel,
        grid_spec=grid_spec,
        out_shape=jax.ShapeDtypeStruct((n_slots * TILES_PER_ROW, LANES), jnp.float32),
        compiler_params=pltpu.CompilerParams(
            dimension_semantics=("arbitrary",),
            vmem_limit_bytes=VMEM_LIMIT),
        name="moe_dispatch",
    )(slot_of_tok, run_start, run_len, n_used, x_tiled)


def _moe_kernel(dst_ref, nv_ref, ea_ref, eb_ref, xstep_ref,
                x_ref, wts_ref, gain_ref, fgain_ref, *rest, final_norm, total):
    n_w = 2 * len(MOE_WEIGHTS) * MOE_STEP_BLOCKS
    w_refs = rest[:n_w]
    out_hbm, obuf, sem_out = rest[n_w:]
    i = pl.program_id(0)
    blk0 = i * MOE_STEP_BLOCKS
    n_first = nv_ref[blk0]
    n_first_prev = nv_ref[jnp.maximum(blk0 - MOE_STEP_BLOCKS, 0)]
    osl = i % MOE_OBUFS
    step_rows = MOE_STEP_BLOCKS * MOE_ROWS

    def counts(step):
        first = jnp.maximum(step, 0) * MOE_STEP_BLOCKS
        return [nv_ref[first + j] for j in range(MOE_STEP_BLOCKS)]

    def out_copy(sl, r, tok):
        if final_norm:
            return pltpu.make_async_copy(obuf.at[sl, pl.ds(r, 1)], out_hbm.at[pl.ds(tok, 1)],
                                         sem_out.at[sl])
        return pltpu.make_async_copy(obuf.at[sl, _tile_of(r)], out_hbm.at[_tile_of(tok)],
                                     sem_out.at[sl])

    def start_scatter(step, sl, rows):
        def unrolled(j):
            for r in range(j * MOE_ROWS, (j + 1) * MOE_ROWS):
                out_copy(sl, r, dst_ref[step * step_rows + r]).start(priority=DMA_THREADS - 1)

        for j in range(MOE_STEP_BLOCKS):
            if not final_norm:
                unrolled(j)
                continue
            pl.when(rows[j] == MOE_ROWS)(functools.partial(unrolled, j))

            @pl.when(rows[j] < MOE_ROWS)
            def _():
                def body(r, c):
                    out_copy(sl, j * MOE_ROWS + r, dst_ref[step * step_rows + j * MOE_ROWS + r]).start()
                    return c
                lax.fori_loop(0, rows[j], body, 0)

    def wait_scatter(sl, rows):
        if not final_norm:
            pltpu.make_async_copy(obuf.at[sl], out_hbm.at[pl.ds(0, obuf.shape[1])],
                                  sem_out.at[sl]).wait()
            return
        for j in range(MOE_STEP_BLOCKS):
            tiled = pl.multiple_of((rows[j] // SUBLANES) * SUBLANES, SUBLANES)

            @pl.when(tiled > 0)
            def _():
                pltpu.make_async_copy(obuf.at[sl, pl.ds(0, tiled)], out_hbm.at[pl.ds(0, tiled)],
                                      sem_out.at[sl]).wait()

            def body(r, c):
                out_copy(sl, 0, 0).wait()
                return c
            lax.fori_loop(tiled, rows[j], body, 0)

    if not final_norm:
        @pl.when(i == 0)
        def _():
            obuf[...] = jnp.zeros(obuf.shape, jnp.float32)
            for r in range(step_rows):
                out_copy(0, r, total + r).start()

    @pl.when(n_first > 0)
    def _():
        tile_rows = MOE_ROWS * TILES_PER_ROW
        blocks = range(MOE_STEP_BLOCKS)
        if not final_norm:
            start_scatter(i, 1 - osl, None)
        xs = [_load_rows(x_ref.at[pl.ds(j * tile_rows, tile_rows)], MOE_ROWS) for j in blocks]
        hs = [_rms(x, gain_ref[...]).astype(jnp.bfloat16) for x in xs]
        wts = [wts_ref[j].T for j in blocks]
        def weight(j, e, kind):
            return w_refs[(2 * j + e) * len(MOE_WEIGHTS) + MOE_WEIGHTS.index(kind)][0, 0]

        experts = [(j, e) for j in blocks for e in range(2)]
        gates = {je: _dot(hs[je[0]], weight(*je, "gate")) for je in experts}
        ups = {je: _dot(hs[je[0]], weight(*je, "up")) for je in experts}
        hid = {je: (_silu(gates[je]) * ups[je]).astype(jnp.bfloat16) for je in experts}
        outs = [xs[j] + wts[j][:, 0:1] * _dot(hid[j, 0], weight(j, 0, "down"))
                + wts[j][:, 1:2] * _dot(hid[j, 1], weight(j, 1, "down")) for j in blocks]
        if final_norm:
            pl.when(i >= MOE_OBUFS)(lambda: wait_scatter(osl, counts(i - MOE_OBUFS)))
            for j in blocks:
                obuf[osl, j * MOE_ROWS:(j + 1) * MOE_ROWS] = _rms(outs[j], fgain_ref[...])
        else:
            wait_scatter(osl, None)
            for j in blocks:
                _store_rows(obuf.at[osl, pl.ds(j * tile_rows, tile_rows)], outs[j])
        if final_norm:
            start_scatter(i, osl, counts(i))

    @pl.when(jnp.logical_and(n_first == 0, n_first_prev > 0))
    def _():
        if final_norm:
            wait_scatter(1 - osl, counts(i - 1))
            pl.when(i >= MOE_OBUFS)(lambda: wait_scatter(osl, counts(i - MOE_OBUFS)))
        else:
            start_scatter(i, 1 - osl, None)
            for sl in range(MOE_OBUFS):
                wait_scatter(sl, None)


def _class_tables():
    ea = np.zeros((N_CLASSES,), np.int32)
    eb = np.zeros((N_CLASSES,), np.int32)
    for g in range(MOE_GROUPS):
        c = g * PAIRS_PER_GROUP
        for a in range(MOE_EPG):
            for b in range(a + 1, MOE_EPG):
                ea[c], eb[c] = g * MOE_EPG + a, g * MOE_EPG + b
                c += 1
    return ea, eb


def _moe_layer(x_tiled, route, gain, fgain, weights, layer, final_norm, order=None):
    total = route.shape[1]
    order_of_tok, tok_of_order = order or ((lambda t: t), (lambda p: p))
    nb = (total // MOE_ROWS + N_CLASSES) // MOE_STEP_BLOCKS * MOE_STEP_BLOCKS + MOE_STEP_BLOCKS
    n_steps = nb // MOE_STEP_BLOCKS
    step_rows = MOE_STEP_BLOCKS * MOE_ROWS
    n_slots = nb * MOE_ROWS
    w_lo, w_hi = route[2], route[3]
    rank = route[4].astype(jnp.int32)
    cls = route[5].astype(jnp.int32)
    cls_ids = jnp.arange(N_CLASSES, dtype=jnp.int32)
    onehot_tok = (cls[:, None] == cls_ids[None, :]).astype(jnp.int32)
    counts = jnp.sum(onehot_tok, axis=0)
    nblk = (counts + MOE_ROWS - 1) // MOE_ROWS
    npad = nblk * MOE_ROWS - counts
    blk_end = jnp.cumsum(nblk)
    cls_start = (blk_end - nblk) * MOE_ROWS
    slot_of_tok = jnp.sum(onehot_tok * cls_start[None, :], axis=1) + rank
    x_sorted = _dispatch(x_tiled, slot_of_tok, cls_start + counts, npad, blk_end[-1:], n_slots)

    max_pad = MOE_ROWS - 1
    cand_cls = jnp.repeat(cls_ids, max_pad)
    cand_on = jnp.tile(jnp.arange(max_pad, dtype=jnp.int32), N_CLASSES) < jnp.repeat(npad, max_pad)
    n_fill = n_slots - total - N_CLASSES * max_pad
    tok_bits = 16
    assert total <= 1 << tok_bits
    keys = jnp.concatenate([
        (2 * cls << tok_bits) + order_of_tok(jnp.arange(total, dtype=jnp.int32)),
        jnp.where(cand_on, 2 * cand_cls + 1, 2 * N_CLASSES + 1) << tok_bits,
        jnp.full((n_fill,), (2 * N_CLASSES + 1) << tok_bits, jnp.int32)])
    zeros = jnp.zeros((n_slots - total,), jnp.float32)
    key_s, wlo_s, whi_s = lax.sort(
        (keys, jnp.concatenate([w_lo, zeros]), jnp.concatenate([w_hi, zeros])),
        num_keys=1, is_stable=False)
    is_pad = ((key_s >> tok_bits) & 1) == 1
    tok_s = tok_of_order(key_s & ((1 << tok_bits) - 1))
    slot = jnp.arange(n_slots, dtype=jnp.int32)
    dump = total + ((slot // step_rows) % MOE_OBUFS) * step_rows + slot % step_rows
    dst = jnp.where(is_pad, dump, tok_s)
    if not final_norm:
        dst = jnp.concatenate([total + step_rows + jnp.arange(step_rows, dtype=jnp.int32),
                               dst[:-step_rows]])
    nvalid = jnp.sum((~is_pad).reshape(nb, MOE_ROWS).astype(jnp.int32), axis=1)
    wts = jnp.concatenate(
        [wlo_s.reshape(nb, 1, MOE_ROWS), whi_s.reshape(nb, 1, MOE_ROWS),
         jnp.zeros((nb, SUBLANES - 2, MOE_ROWS), jnp.float32)], axis=1)
    bidx = jnp.arange(nb, dtype=jnp.int32)
    xblk = jnp.minimum(bidx, blk_end[-1] - 1)
    c_blk = jnp.sum((blk_end[None, :] <= xblk[:, None]).astype(jnp.int32), axis=1)
    xstep = jnp.minimum(jnp.arange(n_steps, dtype=jnp.int32), (blk_end[-1] - 1) // MOE_STEP_BLOCKS)
    ea_tab, eb_tab = _class_tables()
    onehot_blk = (c_blk[:, None] == cls_ids[None, :]).astype(jnp.int32)
    ea = jnp.sum(onehot_blk * jnp.asarray(ea_tab)[None, :], axis=1)
    eb = jnp.sum(onehot_blk * jnp.asarray(eb_tab)[None, :], axis=1)

    tiled_block = (step_rows * TILES_PER_ROW, LANES)
    if final_norm:
        out_block = (step_rows, D_MODEL)
        out_shape = (total, D_MODEL)
    else:
        out_block = tiled_block
        out_shape = ((total + MOE_OBUFS * step_rows) * TILES_PER_ROW, LANES)

    def weight_specs(j):
        def spec(shape, tab):
            return pl.BlockSpec(shape, lambda i, dst, nv, ea, eb, xstep:
                                (layer, (ea, eb)[tab][i * MOE_STEP_BLOCKS + j], 0, 0))
        shapes = {"gate": (1, 1, D_MODEL, MOE_D_FF), "up": (1, 1, D_MODEL, MOE_D_FF),
                  "down": (1, 1, MOE_D_FF, D_MODEL)}
        return [spec(shapes[kind], tab) for tab in range(2) for kind in MOE_WEIGHTS]

    grid_spec = pltpu.PrefetchScalarGridSpec(
        num_scalar_prefetch=5,
        grid=(n_steps,),
        in_specs=[
            pl.BlockSpec(tiled_block, lambda i, dst, nv, ea, eb, xstep: (xstep[i], 0)),
            pl.BlockSpec((MOE_STEP_BLOCKS, SUBLANES, MOE_ROWS), lambda i, *_: (i, 0, 0)),
            pl.BlockSpec((1, D_MODEL), lambda i, *_: (0, 0)),
            pl.BlockSpec((1, D_MODEL), lambda i, *_: (0, 0)),
        ] + [s for j in range(MOE_STEP_BLOCKS) for s in weight_specs(j)],
        out_specs=pl.BlockSpec(memory_space=pl.ANY),
        scratch_shapes=[
            pltpu.VMEM((MOE_OBUFS,) + out_block, jnp.float32),
            pltpu.SemaphoreType.DMA((MOE_OBUFS,)),
        ],
    )
    return pl.pallas_call(
        functools.partial(_moe_kernel, final_norm=final_norm, total=total),
        grid_spec=grid_spec,
        out_shape=jax.ShapeDtypeStruct(out_shape, jnp.float32),
        compiler_params=pltpu.CompilerParams(
            dimension_semantics=("arbitrary",),
            vmem_limit_bytes=VMEM_LIMIT),
        name="moe_layer",
    )(dst, nvalid, ea, eb, xstep, x_sorted, wts, gain, fgain, *(list(weights) * (2 * MOE_STEP_BLOCKS)))


def _router_params(w_group, b_group, w_expert, b_expert):
    gap = ROUTE_EXPERT_COL - MOE_GROUPS
    pad = LANES - ROUTE_EXPERT_COL - MOE_EXPERTS
    wr = jnp.concatenate([w_group, jnp.zeros((D_MODEL, gap), jnp.float32), w_expert,
                          jnp.zeros((D_MODEL, pad), jnp.float32)], axis=1)
    br = jnp.concatenate([b_group, jnp.zeros((gap,), jnp.float32), b_expert.reshape(-1),
                          jnp.zeros((pad,), jnp.float32)])[None, :]
    wr_hi = wr.astype(jnp.bfloat16)
    wr_lo = (wr - wr_hi.astype(jnp.float32)).astype(jnp.bfloat16)
    return jnp.concatenate([wr_hi, wr_lo], axis=1), br


def kernel(x, norm_mix, norm_ffn, ret_w_in, ret_w_out, pool_w, pool_scale, moe_w_group, moe_b_group, moe_w_expert, moe_b_expert, moe_w_gate, moe_w_up, moe_w_down, final_norm):
    batch, seq, _ = x.shape
    bf = jnp.bfloat16
    weights = (moe_w_gate.astype(bf), moe_w_up.astype(bf), moe_w_down.astype(bf))
    fgain = final_norm[None, :]

    wr0, br0 = _router_params(moe_w_group[0], moe_b_group[0], moe_w_expert[0], moe_b_expert[0])
    x1, route0 = _retention_layer(x, norm_mix[0][None, :], ret_w_in[0].astype(bf),
                                  ret_w_out[0].astype(bf), norm_ffn[0][None, :], wr0, br0)
    n_s = seq // RET_TILE
    route0 = route0.reshape(SUBLANES, batch // RET_SEQS, n_s, RET_SEQS, RET_TILE)
    route0 = route0.transpose(0, 1, 3, 2, 4).reshape(SUBLANES, batch * seq)

    def split(i, *sizes):
        digits = []
        for size in reversed(sizes):
            digits.append(i % size)
            i = i // size
        return [i] + digits[::-1]

    def order_of_tok(t):
        grp, sq, tile, row = split(t, RET_SEQS, n_s, RET_TILE)
        return ((grp * n_s + tile) * RET_SEQS + sq) * RET_TILE + row

    def tok_of_order(p):
        grp, tile, sq, row = split(p, n_s, RET_SEQS, RET_TILE)
        return ((grp * RET_SEQS + sq) * n_s + tile) * RET_TILE + row

    x1 = x1.reshape(batch * seq * TILES_PER_ROW, LANES)
    x2 = _moe_layer(x1, route0, norm_ffn[0][None, :], fgain, weights, 0, final_norm=False,
                    order=(order_of_tok, tok_of_order))

    wr1, br1 = _router_params(moe_w_group[1], moe_b_group[1], moe_w_expert[1], moe_b_expert[1])
    x3, route1 = _pool_layer(x2, batch, seq, norm_mix[1][None, :], pool_w[0].astype(bf),
                             pool_scale[0][None, :], norm_ffn[1][None, :], wr1, br1)
    out = _moe_layer(x3, route1, norm_ffn[1][None, :], fgain, weights, 1, final_norm=True)
    return out.reshape(batch, seq, D_MODEL)
```
